```python
import jax, jax.numpy as jnp
from jax import lax
import numpy as np

D_MODEL = 1024
BATCH = 16
SEQ = 2048
DEPTH = 2

CHUNK = 64
D_MIX = D_MODEL
C_CONV = D_MIX // 4
CONV_WIDTH = 31
N_HEADS = 8
HEAD_DIM = 64
D_ATT = N_HEADS * HEAD_DIM
D_Q_LAT = 256
D_KV_LAT = 128
IDX_HEADS = 8
IDX_DIM = 64
TOPK_MAX = 256
Q_BLOCK = 128
C_POOL = D_MIX // 4
POOL_WINDOWS = (2, 4, 8, 16)
POOL_GROUP = C_POOL // 4
N_GROUPS = 4
EXPERTS_PER_GROUP = 8
N_EXPERTS = N_GROUPS * EXPERTS_PER_GROUP
D_EXPERT = 512
MOE_BLOCK = 128
EPS = 1e-6
IN_SPLITS = (C_CONV, C_CONV, D_Q_LAT, D_KV_LAT, IDX_HEADS * IDX_DIM, IDX_DIM, IDX_HEADS, C_POOL)
N_IN = sum(IN_SPLITS)

kernel_name = 'hybrid_chunk_causal_conv_dsa_pool_hmoe'


def _rmsnorm(x, g):
    xf = x.astype(jnp.float32)
    y = xf * lax.rsqrt(jnp.mean(xf * xf, axis=-1, keepdims=True) + EPS)
    return y.astype(x.dtype) * g


def _layernorm(x, g, b):
    xf = x.astype(jnp.float32)
    mu = jnp.mean(xf, axis=-1, keepdims=True)
    var = jnp.mean(jnp.square(xf - mu), axis=-1, keepdims=True)
    return ((xf - mu) * lax.rsqrt(var + EPS)).astype(x.dtype) * g + b


def _conv_module(a_val, a_gate, conv_k, conv_b, ln_g, ln_b):
    u = a_val * jax.nn.sigmoid(a_gate)
    u = lax.conv_general_dilated(
        u, conv_k[:, None, :].astype(u.dtype), window_strides=(1,),
        padding=[(CONV_WIDTH - 1, 0)], dimension_numbers=('NWC', 'WIO', 'NWC'),
        feature_group_count=C_CONV) + conv_b
    return jax.nn.silu(_layernorm(u, ln_g, ln_b))


def _dsa_attention(cq, ckv, qi, ki, wi, q_norm_g, kv_norm_g, w_uq, w_uk, w_uv):
    B, S, _ = ckv.shape
    topk = min(TOPK_MAX, S // 4)
    nb = S // Q_BLOCK
    cq = _rmsnorm(cq, q_norm_g)
    ckv = _rmsnorm(ckv, kv_norm_g)
    q = jnp.einsum('btc,chd->bthd', cq, w_uq)
    q_abs = jnp.einsum('bthd,khd->bthk', q, w_uk)
    qi = qi.reshape(B, S, IDX_HEADS, IDX_DIM)
    key_chunk = jnp.arange(S) // CHUNK
    idx_scale = (IDX_HEADS * IDX_DIM) ** -0.5
    att_scale = HEAD_DIM ** -0.5

    def to_blocks(a):
        return jnp.moveaxis(a.reshape(B, nb, Q_BLOCK, *a.shape[2:]), 1, 0)

    def block(args):
        qa, qib, wib, bi = args
        qchunk = (bi * Q_BLOCK + jnp.arange(Q_BLOCK)) // CHUNK
        allowed = key_chunk[None, :] <= qchunk[:, None]
        rel = jax.nn.relu(jnp.einsum('bqhd,bsd->bqhs', qib, ki).astype(jnp.float32))
        score = jnp.einsum('bqhs,bqh->bqs', rel, wib.astype(jnp.float32)) * idx_scale
        score = jnp.where(allowed[None], score, -jnp.inf)
        _, idx = lax.top_k(score, topk)
        kv_sel = jax.vmap(lambda a, i: a[i])(ckv, idx)
        logits = jnp.einsum('bqhc,bqkc->bqhk', qa, kv_sel).astype(jnp.float32) * att_scale
        valid = (idx // CHUNK) <= qchunk[None, :, None]
        logits = jnp.where(valid[:, :, None, :], logits, -jnp.inf)
        p = jax.nn.softmax(logits, axis=-1).astype(kv_sel.dtype)
        return jnp.einsum('bqhk,bqkc->bqhc', p, kv_sel)

    ctx = lax.map(block, (to_blocks(q_abs), to_blocks(qi), to_blocks(wi), jnp.arange(nb)))
    ctx = jnp.moveaxis(ctx, 0, 1).reshape(B, S, N_HEADS, D_KV_LAT)
    o = jnp.einsum('bthc,chd->bthd', ctx, w_uv)
    return o.reshape(B, S, D_ATT)


def _pool_mixer(u, pool_w, pool_scale):
    B, S, _ = u.shape
    n_pos = jnp.arange(1, S + 1, dtype=jnp.float32)
    outs = []
    for g, w in enumerate(POOL_WINDOWS):
        ug = u[..., g * POOL_GROUP:(g + 1) * POOL_GROUP]
        ugf = ug.astype(jnp.float32)
        cs = jnp.cumsum(ugf, axis=1)
        prev = jnp.pad(cs, ((0, 0), (w, 0), (0, 0)))[:, :S]
        mean = (cs - prev) / jnp.minimum(n_pos, w)[None, :, None]
        outs.append((mean - ugf).astype(u.dtype) @ pool_w[g])
    return jnp.concatenate(outs, axis=-1) * pool_scale


def _hier_moe(h, rg_w, rg_b, re_w, re_b, w1, w3, w2):
    B, S, D = h.shape
    T = B * S
    A = 2 * T
    hf = h.reshape(T, D)
    g_prob = jax.nn.softmax((hf @ rg_w).astype(jnp.float32) + rg_b, axis=-1)
    g_p, g_idx = lax.top_k(g_prob, 1)
    e_logits = jnp.einsum('td,dge->tge', hf, re_w).astype(jnp.float32) + re_b
    e_sel = jnp.take_along_axis(e_logits, g_idx[:, :, None], axis=1)[:, 0]
    e_p, e_idx = lax.top_k(jax.nn.softmax(e_sel, axis=-1), 2)
    e_p = e_p / jnp.sum(e_p, axis=-1, keepdims=True)
    gates = (g_p * e_p).reshape(A)
    a_exp = (g_idx * EXPERTS_PER_GROUP + e_idx).reshape(A).astype(jnp.int32)
    a_tok = jnp.repeat(jnp.arange(T, dtype=jnp.int32), 2)
    order = jnp.argsort(a_exp)
    s_exp = a_exp[order]
    counts = jnp.bincount(a_exp, length=N_EXPERTS)
    starts = jnp.cumsum(counts) - counts
    pcounts = (counts + MOE_BLOCK - 1) // MOE_BLOCK * MOE_BLOCK
    pends = jnp.cumsum(pcounts)
    pstarts = pends - pcounts
    dst = pstarts[s_exp] + jnp.arange(A) - starts[s_exp]
    n_blk = -(-(A + N_EXPERTS * (MOE_BLOCK - 1)) // MOE_BLOCK)
    P = n_blk * MOE_BLOCK
    slot_tok = jnp.zeros((P,), jnp.int32).at[dst].set(a_tok[order])
    slot_gate = jnp.zeros((P,), jnp.float32).at[dst].set(gates[order])
    blk_exp = jnp.minimum(jnp.searchsorted(pends, jnp.arange(n_blk) * MOE_BLOCK, side='right'), N_EXPERTS - 1)

    def run(args):
        tok, e = args
        hb = hf[tok]
        return (jax.nn.silu(hb @ w1[e]) * (hb @ w3[e])) @ w2[e]

    y = lax.map(run, (slot_tok.reshape(n_blk, MOE_BLOCK), blk_exp)).reshape(P, D)
    out = jnp.zeros((T, D), h.dtype).at[slot_tok].add(y * slot_gate[:, None].astype(y.dtype))
    return out.reshape(B, S, D)


def setup_inputs(seed: int = 0) -> dict:
    key = jax.random.key(seed)
    ks = jax.random.split(key, 27)

    def nrm(k, shape, scale):
        return jax.random.normal(k, shape, jnp.float32) * scale

    L = DEPTH
    return {
        'x': nrm(ks[0], (BATCH, SEQ, D_MODEL), 1.0),
        'c': nrm(ks[1], (BATCH, D_MODEL), 1.0),
        'mod_w': nrm(ks[2], (L, D_MODEL, 6 * D_MODEL), 0.5 * D_MODEL ** -0.5),
        'mod_b': nrm(ks[3], (L, 6 * D_MODEL), 0.1),
        'norm1_g': 1.0 + nrm(ks[4], (L, D_MODEL), 0.05),
        'w_in': nrm(ks[5], (L, D_MODEL, N_IN), D_MODEL ** -0.5),
        'conv_k': nrm(ks[6], (L, CONV_WIDTH, C_CONV), CONV_WIDTH ** -0.5),
        'conv_b': nrm(ks[7], (L, C_CONV), 0.02),
        'conv_ln_g': 1.0 + nrm(ks[8], (L, C_CONV), 0.05),
        'conv_ln_b': nrm(ks[9], (L, C_CONV), 0.02),
        'q_norm_g': 1.0 + nrm(ks[10], (L, D_Q_LAT), 0.05),
        'kv_norm_g': 1.0 + nrm(ks[11], (L, D_KV_LAT), 0.05),
        'w_uq': nrm(ks[12], (L, D_Q_LAT, N_HEADS, HEAD_DIM), D_Q_LAT ** -0.5),
        'w_uk': nrm(ks[13], (L, D_KV_LAT, N_HEADS, HEAD_DIM), D_KV_LAT ** -0.5),
        'w_uv': nrm(ks[14], (L, D_KV_LAT, N_HEADS, HEAD_DIM), D_KV_LAT ** -0.5),
        'pool_w': nrm(ks[15], (L, len(POOL_WINDOWS), POOL_GROUP, POOL_GROUP), POOL_GROUP ** -0.5),
        'pool_scale': 1.0 + nrm(ks[16], (L, C_POOL), 0.1),
        'w_out': nrm(ks[17], (L, D_MIX, D_MODEL), D_MIX ** -0.5),
        'norm2_g': 1.0 + nrm(ks[18], (L, D_MODEL), 0.05),
        'router_g_w': nrm(ks[19], (L, D_MODEL, N_GROUPS), D_MODEL ** -0.5),
        'router_g_b': nrm(ks[20], (L, N_GROUPS), 0.01),
        'router_e_w': nrm(ks[21], (L, D_MODEL, N_GROUPS, EXPERTS_PER_GROUP), D_MODEL ** -0.5),
        'router_e_b': nrm(ks[22], (L, N_GROUPS, EXPERTS_PER_GROUP), 0.01),
        'exp_w1': nrm(ks[23], (L, N_EXPERTS, D_MODEL, D_EXPERT), D_MODEL ** -0.5),
        'exp_w3': nrm(ks[24], (L, N_EXPERTS, D_MODEL, D_EXPERT), D_MODEL ** -0.5),
        'exp_w2': nrm(ks[25], (L, N_EXPERTS, D_EXPERT, D_MODEL), D_EXPERT ** -0.5),
        'final_g': 1.0 + nrm(ks[26], (D_MODEL,), 0.05),
    }


def reference(x, c, mod_w, mod_b, norm1_g, w_in, conv_k, conv_b, conv_ln_g, conv_ln_b,
              q_norm_g, kv_norm_g, w_uq, w_uk, w_uv, pool_w, pool_scale, w_out, norm2_g,
              router_g_w, router_g_b, router_e_w, router_e_b, exp_w1, exp_w3, exp_w2, final_g):
    cond = jax.nn.silu(c)
    split_at = np.cumsum(IN_SPLITS)[:-1].tolist()
    for l in range(DEPTH):
        mod = cond @ mod_w[l] + mod_b[l]
        sh1, sc1, g1, sh2, sc2, g2 = [m[:, None, :] for m in jnp.split(mod, 6, axis=-1)]
        h = _rmsnorm(x, norm1_g[l]) * (1 + sc1) + sh1
        z = h @ w_in[l]
        a_val, a_gate, cq, ckv, qi, ki, wi, u_pool = jnp.split(z, split_at, axis=-1)
        y_conv = _conv_module(a_val, a_gate, conv_k[l], conv_b[l], conv_ln_g[l], conv_ln_b[l])
        y_att = _dsa_attention(cq, ckv, qi, ki, wi, q_norm_g[l], kv_norm_g[l], w_uq[l], w_uk[l], w_uv[l])
        y_pool = _pool_mixer(u_pool, pool_w[l], pool_scale[l])
        mix = jnp.concatenate([y_conv, y_att, y_pool], axis=-1) @ w_out[l]
        x = x + g1 * mix
        h = _rmsnorm(x, norm2_g[l]) * (1 + sc2) + sh2
        x = x + g2 * _hier_moe(h, router_g_w[l], router_g_b[l], router_e_w[l], router_e_b[l],
                               exp_w1[l], exp_w3[l], exp_w2[l])
    return _rmsnorm(x, final_g)
```

```python
import functools

import jax
import jax.numpy as jnp
from jax import lax
from jax.experimental import pallas as pl
from jax.experimental.pallas import tpu as pltpu

F32 = jnp.float32
BF16 = jnp.bfloat16
HIGHEST = lax.Precision.HIGHEST

EPS = 1e-6
CHUNK = 64
CONV_WIDTH = 31
C_CONV = 256
C_POOL = 256
POOL_GROUP = 64
N_HEADS = 8
HEAD_DIM = 64
D_Q_LAT = 256
D_KV_LAT = 128
IDX_HEADS = 8
IDX_DIM = 64
TOPK_MAX = 256
N_GROUPS = 4
EXPERTS_PER_GROUP = 8
N_EXPERTS = 32
D_EXPERT = 512
N_IN = 1736
N_IN_PAD = 1792
KW_OFF = 1408
POOL_OFF = 1536
LANES = 128
HALO = 32
CONV_TILE = 64
FFN_BLOCK = 256
VMEM_LIMIT = 56 * 1024 * 1024


def _cparams(sem):
    return pltpu.CompilerParams(dimension_semantics=sem, vmem_limit_bytes=VMEM_LIMIT)


def _mod_kernel(c_ref, w_ref, b_ref, o_ref):
    c = c_ref[...]
    cond = c * jax.nn.sigmoid(c)
    o_ref[0] = jnp.dot(cond, w_ref[0], precision=HIGHEST, preferred_element_type=F32) + b_ref[0]


def _modulation(c, mod_w, mod_b):
    L, D, D6 = mod_w.shape
    B = c.shape[0]
    nj = D6 // D
    return pl.pallas_call(
        _mod_kernel,
        grid=(L, nj),
        in_specs=[
            pl.BlockSpec((B, D), lambda l, j: (0, 0)),
            pl.BlockSpec((1, D, D), lambda l, j: (l, 0, j)),
            pl.BlockSpec((1, 1, D), lambda l, j: (l, 0, j)),
        ],
        out_specs=pl.BlockSpec((1, B, D), lambda l, j: (l, 0, j)),
        out_shape=jax.ShapeDtypeStruct((L, B, D6), F32),
        compiler_params=_cparams(("arbitrary", "arbitrary")),
        name="modulation",
    )(c, mod_w, mod_b.reshape(L, 1, D6))


def _fold_kernel(uq_ref, ukt_ref, uv_ref, wo_ref, wq_ref, wvo_ref):
    wq_ref[0] = jnp.dot(uq_ref[0, 0], ukt_ref[0, 0], precision=HIGHEST,
                        preferred_element_type=F32).astype(BF16)
    wvo_ref[0] = jnp.dot(uv_ref[0, 0], wo_ref[0, 0], precision=HIGHEST,
                         preferred_element_type=F32).astype(BF16)


def _fold_weights(w_uq, w_uk, w_uv, w_out):
    L = w_uq.shape[0]
    D = w_out.shape[-1]
    uq = jnp.transpose(w_uq, (0, 2, 1, 3))
    ukt = jnp.transpose(w_uk, (0, 2, 3, 1))
    uv = jnp.transpose(w_uv, (0, 2, 1, 3))
    wo = w_out[:, C_CONV:C_CONV + N_HEADS * HEAD_DIM, :].reshape(L, N_HEADS, HEAD_DIM, D)
    return pl.pallas_call(
        _fold_kernel,
        grid=(L, N_HEADS),
        in_specs=[
            pl.BlockSpec((1, 1, D_Q_LAT, HEAD_DIM), lambda l, h: (l, h, 0, 0)),
            pl.BlockSpec((1, 1, HEAD_DIM, D_KV_LAT), lambda l, h: (l, h, 0, 0)),
            pl.BlockSpec((1, 1, D_KV_LAT, HEAD_DIM), lambda l, h: (l, h, 0, 0)),
            pl.BlockSpec((1, 1, HEAD_DIM, D), lambda l, h: (l, h, 0, 0)),
        ],
        out_specs=[
            pl.BlockSpec((1, D_Q_LAT, D_KV_LAT), lambda l, h: (l, 0, h)),
            pl.BlockSpec((1, D_KV_LAT, D), lambda l, h: (l, h, 0)),
        ],
        out_shape=[
            jax.ShapeDtypeStruct((L, D_Q_LAT, N_HEADS * D_KV_LAT), BF16),
            jax.ShapeDtypeStruct((L, N_HEADS * D_KV_LAT, D), BF16),
        ],
        compiler_params=_cparams(("arbitrary", "arbitrary")),
        name="fold_weights",
    )(uq, ukt, uv, wo)


def _rms(v):
    return v * lax.rsqrt(jnp.mean(v * v, axis=-1, keepdims=True) + EPS)


def _in_kernel(x_ref, sc_ref, sh_ref, g_ref, w_ref, qg_ref, kvg_ref, wq_ref,
               cu_ref, qabs_ref, ckv_ref, qi_ref, kw_ref):
    x = x_ref[0]
    h = _rms(x) * g_ref[...] * (1.0 + sc_ref[0]) + sh_ref[0]
    z = jnp.dot(h.astype(BF16), w_ref[...], preferred_element_type=F32)
    a_val = z[:, 0:C_CONV]
    a_gate = z[:, C_CONV:2 * C_CONV]
    cu_ref[0, :, 0:C_CONV] = a_val * jax.nn.sigmoid(a_gate)
    cu_ref[0, :, C_CONV:C_CONV + C_POOL] = z[:, POOL_OFF:POOL_OFF + C_POOL]
    o = 2 * C_CONV
    cq = _rms(z[:, o:o + D_Q_LAT]) * qg_ref[...]
    qabs = jnp.dot(cq.astype(BF16), wq_ref[...], preferred_element_type=F32)
    qabs_ref[0] = (qabs * (HEAD_DIM ** -0.5)).astype(BF16)
    o += D_Q_LAT
    ckv_ref[0] = (_rms(z[:, o:o + D_KV_LAT]) * kvg_ref[...]).astype(BF16)
    o += D_KV_LAT
    qi_ref[0] = z[:, o:o + IDX_HEADS * IDX_DIM].astype(BF16)
    kw_ref[0] = z[:, KW_OFF:KW_OFF + LANES]


def _in_proj(x, sc, sh, g, w_in_p, qg, kvg, wq, tm):
    B, S, D = x.shape
    tok = lambda n: pl.BlockSpec((1, tm, n), lambda b, i: (b, i, 0))
    per_b = pl.BlockSpec((1, 1, D), lambda b, i: (b, 0, 0))
    full = lambda a: pl.BlockSpec(a.shape, lambda b, i: (0,) * a.ndim)
    nq = IDX_HEADS * IDX_DIM
    return pl.pallas_call(
        _in_kernel,
        grid=(B, S // tm),
        in_specs=[tok(D), per_b, per_b, full(g), full(w_in_p), full(qg), full(kvg), full(wq)],
        out_specs=[tok(C_CONV + C_POOL), tok(N_HEADS * D_KV_LAT), tok(D_KV_LAT), tok(nq), tok(LANES)],
        out_shape=[
            jax.ShapeDtypeStruct((B, S, C_CONV + C_POOL), F32),
            jax.ShapeDtypeStruct((B, S, N_HEADS * D_KV_LAT), BF16),
            jax.ShapeDtypeStruct((B, S, D_KV_LAT), BF16),
            jax.ShapeDtypeStruct((B, S, nq), BF16),
            jax.ShapeDtypeStruct((B, S, LANES), F32),
        ],
        compiler_params=_cparams(("arbitrary", "arbitrary")),
        name="in_proj",
    )(x, sc, sh, g, w_in_p, qg, kvg, wq)


def _cp_kernel(cu_ref, ck_ref, cb_ref, lg_ref, lb_ref, pw_ref, ps_ref, o_ref, pad_ref):
    S = cu_ref.shape[1]
    TT = CONV_TILE
    pad_ref[0:HALO, :] = jnp.zeros((HALO, C_CONV + C_POOL), F32)
    pad_ref[HALO:HALO + S, :] = cu_ref[0]
    lane = lax.broadcasted_iota(jnp.int32, (TT, C_POOL), 1)
    row = lax.broadcasted_iota(jnp.int32, (TT, C_POOL), 0)
    win_len = jnp.where(lane < POOL_GROUP, 2,
                        jnp.where(lane < 2 * POOL_GROUP, 4, jnp.where(lane < 3 * POOL_GROUP, 8, 16)))

    def tile(i, carry):
        t0 = pl.multiple_of(i * TT, TT)
        win = pad_ref[pl.ds(t0, TT + HALO), :]
        wc = win[:, 0:C_CONV]
        wp = win[:, C_CONV:C_CONV + C_POOL]
        acc = jnp.zeros((TT, C_CONV), F32)
        for j in range(CONV_WIDTH):
            o = HALO - (CONV_WIDTH - 1) + j
            acc = acc + ck_ref[j:j + 1, :] * wc[o:o + TT, :]
        v = acc + cb_ref[...]
        mu = jnp.mean(v, axis=-1, keepdims=True)
        vc = v - mu
        var = jnp.mean(vc * vc, axis=-1, keepdims=True)
        yn = vc * lax.rsqrt(var + EPS) * lg_ref[...] + lb_ref[...]
        o_ref[0, pl.ds(t0, TT), 0:C_CONV] = (yn * jax.nn.sigmoid(yn)).astype(BF16)
        lo = HALO - 15
        s2 = wp[lo + 1:, :] + wp[lo:-1, :]
        s4 = s2[2:, :] + s2[:-2, :]
        s8 = s4[4:, :] + s4[:-4, :]
        s16 = s8[8:, :] + s8[:-8, :]
        u = wp[HALO:, :]
        sw = jnp.where(lane < POOL_GROUP, s2[14:, :],
                       jnp.where(lane < 2 * POOL_GROUP, s4[12:, :],
                                 jnp.where(lane < 3 * POOL_GROUP, s8[8:, :], s16)))
        n_pos = jnp.minimum(t0 + row + 1, win_len).astype(F32)
        d = (sw / n_pos - u).astype(BF16)
        yp = jnp.dot(d, pw_ref[...], preferred_element_type=F32) * ps_ref[...]
        o_ref[0, pl.ds(t0, TT), C_CONV:C_CONV + C_POOL] = yp.astype(BF16)
        return carry

    lax.fori_loop(0, S // TT, tile, 0)


def _conv_pool(cu, ck, cb, lg, lb, pw_bd, ps):
    B, S, W = cu.shape
    full = lambda a: pl.BlockSpec(a.shape, lambda b: (0,) * a.ndim)
    return pl.pallas_call(
        _cp_kernel,
        grid=(B,),
        in_specs=[pl.BlockSpec((1, S, W), lambda b: (b, 0, 0)),
                  full(ck), full(cb), full(lg), full(lb), full(pw_bd), full(ps)],
        out_specs=pl.BlockSpec((1, S, W), lambda b: (b, 0, 0)),
        out_shape=jax.ShapeDtypeStruct((B, S, W), BF16),
        scratch_shapes=[pltpu.VMEM((S + HALO, W), F32)],
        compiler_params=_cparams(("arbitrary",)),
        name="conv_pool",
    )(cu, ck, cb, lg, lb, pw_bd, ps)


def _attn_kernel(qi_ref, kwq_ref, kwk_ref, qabs_ref, ckv_ref, o_ref, key_ref, aux_ref, *, topk):
    QB = qi_ref.shape[1]
    S = ckv_ref.shape[1]
    j = pl.program_id(1)
    nt = (((0,), (0,)), ((), ()))
    dn_nt = (((1,), (1,)), ((), ()))

    ki = kwk_ref[0, :, 0:IDX_DIM].astype(BF16)
    wi = kwq_ref[0, :, IDX_DIM:IDX_DIM + IDX_HEADS]
    qi = qi_ref[0]
    score = jnp.zeros((QB, S), F32)
    for h in range(IDX_HEADS):
        rel = lax.dot_general(qi[:, h * IDX_DIM:(h + 1) * IDX_DIM], ki, dn_nt,
                              preferred_element_type=F32)
        score = score + jnp.maximum(rel, 0.0) * wi[:, h:h + 1]
    score = score * ((IDX_HEADS * IDX_DIM) ** -0.5)

    kidx = lax.broadcasted_iota(jnp.int32, (QB, S), 1)
    qpos = j * QB + lax.broadcasted_iota(jnp.int32, (QB, 1), 0)
    limit = (qpos // CHUNK + 1) * CHUNK
    allowed = kidx < limit
    score = jnp.where(score == 0.0, 0.0, score)
    score = jnp.where(allowed, score, -jnp.inf)
    bits = lax.bitcast_convert_type(score, jnp.int32)
    key_ref[...] = bits ^ ((bits >> 31) & jnp.int32(0x7FFFFFFF))

    kf = jnp.float32(topk)

    def count_ge(cand):
        return jnp.sum(jnp.where(key_ref[...] >= cand, 1.0, 0.0), axis=-1, keepdims=True)

    neg = jnp.full((QB, 1), jnp.iinfo(jnp.int32).min, jnp.int32)
    prefix = jnp.where(count_ge(jnp.zeros((QB, 1), jnp.int32)) >= kf, 0, neg)

    def bit_step(i, prefix):
        cand = prefix | (jnp.int32(1) << (30 - i))
        return jnp.where(count_ge(cand) >= kf, cand, prefix)

    thr = lax.fori_loop(0, 31, bit_step, prefix)

    key = key_ref[...]
    n_gt = jnp.sum(jnp.where(key > thr, 1.0, 0.0), axis=-1, keepdims=True)
    need = kf - n_gt
    aux_ref[...] = jnp.where(key == thr, kidx, jnp.int32(S))

    def idx_step(i, p):
        cand = p | (jnp.int32(1) << (S.bit_length() - 2 - i))
        cnt = jnp.sum(jnp.where(aux_ref[...] < cand, 1.0, 0.0), axis=-1, keepdims=True)
        return jnp.where(cnt < need, cand, p)

    cut = lax.fori_loop(0, S.bit_length() - 1, idx_step, jnp.zeros((QB, 1), jnp.int32))
    sel = ((key_ref[...] > thr) | (aux_ref[...] <= cut)) & allowed
    bias = jnp.where(sel, 0.0, -jnp.inf)

    ckv = ckv_ref[0]
    for h in range(N_HEADS):
        qa = qabs_ref[0, :, h * D_KV_LAT:(h + 1) * D_KV_LAT]
        logit = lax.dot_general(qa, ckv, dn_nt, preferred_element_type=F32) + bias
        m = jnp.max(logit, axis=-1, keepdims=True)
        p = jnp.exp(logit - m)
        l = jnp.sum(p, axis=-1, keepdims=True)
        ctx = jnp.dot(p.astype(BF16), ckv, preferred_element_type=F32)
        o_ref[0, :, h * D_KV_LAT:(h + 1) * D_KV_LAT] = (ctx / l).astype(BF16)


def _attention(qi, kw, qabs, ckv, qb):
    B, S, _ = ckv.shape
    topk = min(TOPK_MAX, S // 4)
    blk = lambda n: pl.BlockSpec((1, qb, n), lambda b, j: (b, j, 0))
    seq = lambda n: pl.BlockSpec((1, S, n), lambda b, j: (b, 0, 0))
    return pl.pallas_call(
        functools.partial(_attn_kernel, topk=topk),
        grid=(B, S // qb),
        in_specs=[blk(IDX_HEADS * IDX_DIM), blk(LANES), seq(LANES), blk(N_HEADS * D_KV_LAT), seq(D_KV_LAT)],
        out_specs=blk(N_HEADS * D_KV_LAT),
        out_shape=jax.ShapeDtypeStruct((B, S, N_HEADS * D_KV_LAT), BF16),
        scratch_shapes=[pltpu.VMEM((qb, S), jnp.int32), pltpu.VMEM((qb, S), jnp.int32)],
        compiler_params=_cparams(("arbitrary", "arbitrary")),
        name="dsa_attention",
    )(qi, kw, kw, qabs, ckv)


def _out_kernel(ycp_ref, ctx_ref, x_ref, g1_ref, sc_ref, sh_ref, ng_ref, wcp_ref, wvo_ref,
                rw_ref, rb_ref, tri_ref, x1_ref, h2_ref, route_ref, cnt_ref, carry_ref):
    first = (pl.program_id(0) == 0) & (pl.program_id(1) == 0)

    @pl.when(first)
    def _():
        carry_ref[...] = jnp.zeros_like(carry_ref)

    mix = jnp.dot(ycp_ref[0], wcp_ref[...], preferred_element_type=F32)
    mix = mix + jnp.dot(ctx_ref[0], wvo_ref[...], preferred_element_type=F32)
    x1 = x_ref[0] + g1_ref[0] * mix
    x1_ref[0] = x1
    h2 = _rms(x1) * ng_ref[...] * (1.0 + sc_ref[0]) + sh_ref[0]
    h2_ref[0] = h2
    logits = jnp.dot(h2, rw_ref[...], precision=HIGHEST, preferred_element_type=F32) + rb_ref[...]

    tm = logits.shape[0]
    lane = lax.broadcasted_iota(jnp.int32, (tm, LANES), 1)
    lanef = lane.astype(F32)
    ninf = -jnp.inf
    big = jnp.float32(LANES)
    glog = jnp.where(lane < N_GROUPS, logits, ninf)
    gmax = jnp.max(glog, axis=-1, keepdims=True)
    gidx = jnp.min(jnp.where(glog == gmax, lanef, big), axis=-1, keepdims=True)
    g_p = 1.0 / jnp.sum(jnp.exp(glog - gmax), axis=-1, keepdims=True)
    lo = N_GROUPS + EXPERTS_PER_GROUP * gidx
    elog = jnp.where((lanef >= lo) & (lanef < lo + EXPERTS_PER_GROUP), logits, ninf)
    m1 = jnp.max(elog, axis=-1, keepdims=True)
    i1 = jnp.min(jnp.where(elog == m1, lanef, big), axis=-1, keepdims=True)
    elog2 = jnp.where(lanef == i1, ninf, elog)
    m2 = jnp.max(elog2, axis=-1, keepdims=True)
    i2 = jnp.min(jnp.where(elog2 == m2, lanef, big), axis=-1, keepdims=True)
    r = jnp.exp(m2 - m1)
    gate1 = g_p / (1.0 + r)
    gate2 = g_p * r / (1.0 + r)
    e1 = i1 - N_GROUPS
    e2 = i2 - N_GROUPS
    oh1 = lanef == e1
    oh2 = lanef == e2
    oh1f = jnp.where(oh1, 1.0, 0.0)
    oh2f = jnp.where(oh2, 1.0, 0.0)
    pre1 = jnp.dot(tri_ref[...], oh1f.astype(BF16), preferred_element_type=F32)
    pre2 = jnp.dot(tri_ref[...], oh2f.astype(BF16), preferred_element_type=F32)
    carry = carry_ref[0:1, :]
    cnt1 = jnp.sum(oh1f, axis=0, keepdims=True)
    cnt2 = jnp.sum(oh2f, axis=0, keepdims=True)
    rank1 = jnp.sum(jnp.where(oh1, carry + pre1, 0.0), axis=-1, keepdims=True)
    rank2 = jnp.sum(jnp.where(oh2, carry + cnt1 + pre2, 0.0), axis=-1, keepdims=True)
    total = carry + cnt1 + cnt2
    carry_ref[...] = jnp.broadcast_to(total, carry_ref.shape)
    cnt_ref[...] = jnp.broadcast_to(total, cnt_ref.shape)
    route = jnp.where(lane == 0, e1, jnp.where(lane == 1, e2, jnp.where(lane == 2, gate1, jnp.where(
        lane == 3, gate2, jnp.where(lane == 4, rank1, jnp.where(lane == 5, rank2, 0.0))))))
    route_ref[0] = route


def _out_proj(ycp, ctx, x, g1, sc2, sh2, ng, wcp, wvo, rw, rb, tm):
    B, S, D = x.shape
    tok = lambda n: pl.BlockSpec((1, tm, n), lambda b, i: (b, i, 0))
    per_b = pl.BlockSpec((1, 1, D), lambda b, i: (b, 0, 0))
    full = lambda a: pl.BlockSpec(a.shape, lambda b, i: (0,) * a.ndim)
    tri = jnp.tril(jnp.ones((tm, tm), BF16), -1)
    return pl.pallas_call(
        _out_kernel,
        grid=(B, S // tm),
        in_specs=[tok(C_CONV + C_POOL), tok(N_HEADS * D_KV_LAT), tok(D), per_b, per_b, per_b,
                  full(ng), full(wcp), full(wvo), full(rw), full(rb), full(tri)],
        out_specs=[tok(D), tok(D), tok(LANES), pl.BlockSpec((8, LANES), lambda b, i: (0, 0))],
        out_shape=[
            jax.ShapeDtypeStruct((B, S, D), F32),
            jax.ShapeDtypeStruct((B, S, D), F32),
            jax.ShapeDtypeStruct((B, S, LANES), F32),
            jax.ShapeDtypeStruct((8, LANES), F32),
        ],
        scratch_shapes=[pltpu.VMEM((8, LANES), F32)],
        compiler_params=_cparams(("arbitrary", "arbitrary")),
        name="out_proj_router",
    )(ycp, ctx, x, g1, sc2, sh2, ng, wcp, wvo, rw, rb, tri)


def _row_copy(src, i, dst, k, sem):
    return pltpu.make_async_copy(src.at[pl.ds(i, 1), :], dst.at[pl.ds(k, 1), :], sem)


def _dispatch_kernel(dst_ref, h_ref, xs_in_ref, xs_ref, sem):
    del xs_in_ref
    tm = h_ref.shape[0]

    def issue(r, c):
        _row_copy(h_ref, r, xs_ref, dst_ref[2 * r], sem).start()
        _row_copy(h_ref, r, xs_ref, dst_ref[2 * r + 1], sem).start()
        return c

    lax.fori_loop(0, tm, issue, 0)

    def drain(r, c):
        _row_copy(h_ref, r, xs_ref, dst_ref[2 * r], sem).wait()
        _row_copy(h_ref, r, xs_ref, dst_ref[2 * r + 1], sem).wait()
        return c

    lax.fori_loop(0, tm, drain, 0)


def _dispatch(dst_flat, h2, n_rows, tm):
    T, D = h2.shape
    xs0 = jnp.zeros((n_rows, D), F32)
    return pl.pallas_call(
        _dispatch_kernel,
        grid=(T // tm,),
        in_specs=[pl.BlockSpec((2 * tm,), lambda i: (i,), memory_space=pltpu.SMEM),
                  pl.BlockSpec((tm, D), lambda i: (i, 0)),
                  pl.BlockSpec(memory_space=pl.ANY)],
        out_specs=pl.BlockSpec(memory_space=pl.ANY),
        out_shape=jax.ShapeDtypeStruct((n_rows, D), F32),
        scratch_shapes=[pltpu.SemaphoreType.DMA(())],
        input_output_aliases={2: 0},
        compiler_params=_cparams(("arbitrary",)),
        name="moe_dispatch",
    )(dst_flat, h2, xs0)


def _ffn_kernel(be_ref, nu_ref, xs_ref, w13_ref, w2_ref, y_ref):
    i = pl.program_id(0)

    @pl.when(i < nu_ref[0])
    def _():
        xb = xs_ref[...].astype(BF16)
        h13 = jnp.dot(xb, w13_ref[0], preferred_element_type=F32)
        h1 = h13[:, 0:D_EXPERT]
        h3 = h13[:, D_EXPERT:2 * D_EXPERT]
        act = (h1 * jax.nn.sigmoid(h1)) * h3
        y_ref[...] = jnp.dot(act.astype(BF16), w2_ref[0], preferred_element_type=F32)

    @pl.when(i >= nu_ref[0])
    def _():
        y_ref[...] = jnp.zeros_like(y_ref)


def _expert_ffn(blk_exp, n_used, xs, w13, w2):
    P, D = xs.shape
    nblk = P // FFN_BLOCK
    grid_spec = pltpu.PrefetchScalarGridSpec(
        num_scalar_prefetch=2,
        grid=(nblk,),
        in_specs=[
            pl.BlockSpec((FFN_BLOCK, D), lambda i, be, nu: (jnp.minimum(i, nu[0] - 1), 0)),
            pl.BlockSpec((1, D, 2 * D_EXPERT), lambda i, be, nu: (be[i], 0, 0)),
            pl.BlockSpec((1, D_EXPERT, D), lambda i, be, nu: (be[i], 0, 0)),
        ],
        out_specs=pl.BlockSpec((FFN_BLOCK, D), lambda i, be, nu: (i, 0)),
    )
    return pl.pallas_call(
        _ffn_kernel,
        grid_spec=grid_spec,
        out_shape=jax.ShapeDtypeStruct((P, D), F32),
        compiler_params=_cparams(("arbitrary",)),
        name="expert_ffn",
    )(blk_exp, n_used, xs, w13, w2)


def _combine_kernel(dst_ref, x1_ref, route_ref, g2_ref, fg_ref, y_ref, o_ref, ya_ref, yb_ref, sem,
                    *, final_norm):
    tm = x1_ref.shape[1]

    def issue(r, c):
        _row_copy(y_ref, dst_ref[2 * r], ya_ref, r, sem.at[0]).start()
        _row_copy(y_ref, dst_ref[2 * r + 1], yb_ref, r, sem.at[1]).start()
        return c

    lax.fori_loop(0, tm, issue, 0)

    def drain(r, c):
        _row_copy(y_ref, dst_ref[2 * r], ya_ref, r, sem.at[0]).wait()
        _row_copy(y_ref, dst_ref[2 * r + 1], yb_ref, r, sem.at[1]).wait()
        return c

    lax.fori_loop(0, tm, drain, 0)

    route = route_ref[0]
    moe = ya_ref[...] * route[:, 2:3] + yb_ref[...] * route[:, 3:4]
    out = x1_ref[0] + g2_ref[0] * moe
    if final_norm:
        out = _rms(out) * fg_ref[...]
    o_ref[0] = out


def _combine(dst_flat, x1, route, g2, fg, y, tm, final_norm):
    B, S, D = x1.shape
    nt = S // tm
    tok = lambda n: pl.BlockSpec((1, tm, n), lambda b, i: (b, i, 0))
    return pl.pallas_call(
        functools.partial(_combine_kernel, final_norm=final_norm),
        grid=(B, nt),
        in_specs=[pl.BlockSpec((2 * tm,), lambda b, i: (b * nt + i,), memory_space=pltpu.SMEM),
                  tok(D), tok(LANES),
                  pl.BlockSpec((1, 1, D), lambda b, i: (b, 0, 0)),
                  pl.BlockSpec((1, D), lambda b, i: (0, 0)),
                  pl.BlockSpec(memory_space=pl.ANY)],
        out_specs=tok(D),
        out_shape=jax.ShapeDtypeStruct((B, S, D), F32),
        scratch_shapes=[pltpu.VMEM((tm, D), F32), pltpu.VMEM((tm, D), F32), pltpu.SemaphoreType.DMA((2,))],
        compiler_params=_cparams(("arbitrary", "arbitrary")),
        name="moe_combine",
    )(dst_flat, x1, route, g2, fg, y)


def _pick_tile(n, pref):
    t = min(n, pref)
    while n % t:
        t //= 2
    return t


def kernel(x, c, mod_w, mod_b, norm1_g, w_in, conv_k, conv_b, conv_ln_g, conv_ln_b, q_norm_g, kv_norm_g, w_uq, w_uk, w_uv, pool_w, pool_scale, w_out, norm2_g, router_g_w, router_g_b, router_e_w, router_e_b, exp_w1, exp_w3, exp_w2, final_g):
    B, S, D = x.shape
    L = mod_w.shape[0]
    T = B * S
    tm = _pick_tile(S, 512)
    qb = _pick_tile(S, 256)
    tmc = _pick_tile(S, 256)
    n_rows = 2 * T + N_EXPERTS * FFN_BLOCK
    nblk = n_rows // FFN_BLOCK

    mod = _modulation(c, mod_w, mod_b)
    wq_all, wvo_all = _fold_weights(w_uq, w_uk, w_uv, w_out)
    row = lambda a: a.reshape(1, -1)

    for l in range(L):
        sh1, sc1, g1, sh2, sc2, g2 = [mod[l, :, k * D:(k + 1) * D].reshape(B, 1, D) for k in range(6)]
        w_in_p = jnp.concatenate(
            [w_in[l, :, :N_IN - C_POOL], jnp.zeros((D, N_IN_PAD - N_IN), F32), w_in[l, :, N_IN - C_POOL:]],
            axis=1).astype(BF16)
        cu, qabs, ckv, qi, kw = _in_proj(x, sc1, sh1, row(norm1_g[l]), w_in_p, row(q_norm_g[l]),
                                         row(kv_norm_g[l]), wq_all[l], tm)
        pw_bd = jax.scipy.linalg.block_diag(*[pool_w[l, g] for g in range(pool_w.shape[1])]).astype(BF16)
        ycp = _conv_pool(cu, conv_k[l], row(conv_b[l]), row(conv_ln_g[l]), row(conv_ln_b[l]),
                         pw_bd, row(pool_scale[l]))
        ctx = _attention(qi, kw, qabs, ckv, qb)
        wcp = jnp.concatenate([w_out[l, :C_CONV], w_out[l, D - C_POOL:]], axis=0).astype(BF16)
        n_r = N_GROUPS + N_EXPERTS
        rw = jnp.concatenate([router_g_w[l], router_e_w[l].reshape(D, N_EXPERTS),
                              jnp.zeros((D, LANES - n_r), F32)], axis=1)
        rb = jnp.concatenate([router_g_b[l], router_e_b[l].reshape(N_EXPERTS),
                              jnp.zeros((LANES - n_r,), F32)]).reshape(1, LANES)
        x1, h2, route, cnt = _out_proj(ycp, ctx, x, g1, sc2, sh2, row(norm2_g[l]), wcp, wvo_all[l],
                                       rw, rb, tm)
        counts = cnt[0, :N_EXPERTS].astype(jnp.int32)
        pcounts = (counts + FFN_BLOCK - 1) // FFN_BLOCK * FFN_BLOCK
        pends = jnp.cumsum(pcounts)
        pstarts = pends - pcounts
        route2 = route.reshape(T, LANES)
        e_idx = route2[:, 0:2].astype(jnp.int32)
        dst = (pstarts[e_idx] + route2[:, 4:6].astype(jnp.int32)).reshape(2 * T)
        n_used = (pends[-1] // FFN_BLOCK).astype(jnp.int32).reshape(1)
        blk_exp = jnp.minimum(
            jnp.searchsorted(pends, jnp.arange(nblk, dtype=jnp.int32) * FFN_BLOCK, side='right'),
            N_EXPERTS - 1).astype(jnp.int32)
        xs = _dispatch(dst, h2.reshape(T, D), n_rows, tmc)
        w13 = jnp.concatenate([exp_w1[l], exp_w3[l]], axis=-1).astype(BF16)
        y = _expert_ffn(blk_exp, n_used, xs, w13, exp_w2[l].astype(BF16))
        x = _combine(dst, x1, route, g2, row(final_g), y, tmc, final_norm=(l == L - 1))
    return x
```

```python
import functools

import jax
import jax.numpy as jnp
from jax import lax
from jax.experimental import pallas as pl
from jax.experimental.pallas import tpu as pltpu

F32 = jnp.float32
BF16 = jnp.bfloat16
HIGHEST = lax.Precision.HIGHEST

EPS = 1e-6
CHUNK = 64
CONV_WIDTH = 31
C_CONV = 256
C_POOL = 256
POOL_GROUP = 64
N_HEADS = 8
HEAD_DIM = 64
D_Q_LAT = 256
D_KV_LAT = 128
IDX_HEADS = 8
IDX_DIM = 64
TOPK_MAX = 256
N_GROUPS = 4
EXPERTS_PER_GROUP = 8
N_EXPERTS = 32
D_EXPERT = 512
N_IN = 1736
N_IN_PAD = 1792
KW_OFF = 1408
POOL_OFF = 1536
LANES = 128
HALO = 32
CONV_TILE = 64
FFN_BLOCK = 256
VMEM_LIMIT = 56 * 1024 * 1024


def _cparams(sem):
    return pltpu.CompilerParams(dimension_semantics=sem, vmem_limit_bytes=VMEM_LIMIT)


def _mod_kernel(c_ref, w_ref, b_ref, o_ref):
    c = c_ref[...]
    cond = c * jax.nn.sigmoid(c)
    o_ref[0] = jnp.dot(cond, w_ref[0], precision=HIGHEST, preferred_element_type=F32) + b_ref[0]


def _modulation(c, mod_w, mod_b):
    L, D, D6 = mod_w.shape
    B = c.shape[0]
    nj = D6 // D
    return pl.pallas_call(
        _mod_kernel,
        grid=(L, nj),
        in_specs=[
            pl.BlockSpec((B, D), lambda l, j: (0, 0)),
            pl.BlockSpec((1, D, D), lambda l, j: (l, 0, j)),
            pl.BlockSpec((1, 1, D), lambda l, j: (l, 0, j)),
        ],
        out_specs=pl.BlockSpec((1, B, D), lambda l, j: (l, 0, j)),
        out_shape=jax.ShapeDtypeStruct((L, B, D6), F32),
        compiler_params=_cparams(("arbitrary", "arbitrary")),
        name="modulation",
    )(c, mod_w, mod_b.reshape(L, 1, D6))


def _fold_kernel(uq_ref, ukt_ref, uv_ref, wo_ref, wq_ref, wvo_ref):
    wq_ref[0] = jnp.dot(uq_ref[0, 0], ukt_ref[0, 0], precision=HIGHEST,
                        preferred_element_type=F32).astype(BF16)
    wvo_ref[0] = jnp.dot(uv_ref[0, 0], wo_ref[0, 0], precision=HIGHEST,
                         preferred_element_type=F32).astype(BF16)


def _fold_weights(w_uq, w_uk, w_uv, w_out):
    L = w_uq.shape[0]
    D = w_out.shape[-1]
    uq = jnp.transpose(w_uq, (0, 2, 1, 3))
    ukt = jnp.transpose(w_uk, (0, 2, 3, 1))
    uv = jnp.transpose(w_uv, (0, 2, 1, 3))
    wo = w_out[:, C_CONV:C_CONV + N_HEADS * HEAD_DIM, :].reshape(L, N_HEADS, HEAD_DIM, D)
    return pl.pallas_call(
        _fold_kernel,
        grid=(L, N_HEADS),
        in_specs=[
            pl.BlockSpec((1, 1, D_Q_LAT, HEAD_DIM), lambda l, h: (l, h, 0, 0)),
            pl.BlockSpec((1, 1, HEAD_DIM, D_KV_LAT), lambda l, h: (l, h, 0, 0)),
            pl.BlockSpec((1, 1, D_KV_LAT, HEAD_DIM), lambda l, h: (l, h, 0, 0)),
            pl.BlockSpec((1, 1, HEAD_DIM, D), lambda l, h: (l, h, 0, 0)),
        ],
        out_specs=[
            pl.BlockSpec((1, D_Q_LAT, D_KV_LAT), lambda l, h: (l, 0, h)),
            pl.BlockSpec((1, D_KV_LAT, D), lambda l, h: (l, h, 0)),
        ],
        out_shape=[
            jax.ShapeDtypeStruct((L, D_Q_LAT, N_HEADS * D_KV_LAT), BF16),
            jax.ShapeDtypeStruct((L, N_HEADS * D_KV_LAT, D), BF16),
        ],
        compiler_params=_cparams(("arbitrary", "arbitrary")),
        name="fold_weights",
    )(uq, ukt, uv, wo)


def _rms(v):
    return v * lax.rsqrt(jnp.mean(v * v, axis=-1, keepdims=True) + EPS)


def _in_kernel(x_ref, sc_ref, sh_ref, g_ref, w_ref, qg_ref, kvg_ref, wq_ref,
               cu_ref, qabs_ref, ckv_ref, qi_ref, kw_ref):
    x = x_ref[0]
    h = _rms(x) * g_ref[...] * (1.0 + sc_ref[0]) + sh_ref[0]
    z = jnp.dot(h.astype(BF16), w_ref[...], preferred_element_type=F32)
    a_val = z[:, 0:C_CONV]
    a_gate = z[:, C_CONV:2 * C_CONV]
    cu_ref[0, :, 0:C_CONV] = a_val * jax.nn.sigmoid(a_gate)
    cu_ref[0, :, C_CONV:C_CONV + C_POOL] = z[:, POOL_OFF:POOL_OFF + C_POOL]
    o = 2 * C_CONV
    cq = _rms(z[:, o:o + D_Q_LAT]) * qg_ref[...]
    qabs = jnp.dot(cq.astype(BF16), wq_ref[...], preferred_element_type=F32)
    qabs_ref[0] = (qabs * (HEAD_DIM ** -0.5)).astype(BF16)
    o += D_Q_LAT
    ckv_ref[0] = (_rms(z[:, o:o + D_KV_LAT]) * kvg_ref[...]).astype(BF16)
    o += D_KV_LAT
    qi_ref[0] = z[:, o:o + IDX_HEADS * IDX_DIM].astype(BF16)
    kw_ref[0] = z[:, KW_OFF:KW_OFF + LANES]


def _in_proj(x, sc, sh, g, w_in_p, qg, kvg, wq, tm):
    B, S, D = x.shape
    tok = lambda n: pl.BlockSpec((1, tm, n), lambda b, i: (b, i, 0))
    per_b = pl.BlockSpec((1, 1, D), lambda b, i: (b, 0, 0))
    full = lambda a: pl.BlockSpec(a.shape, lambda b, i: (0,) * a.ndim)
    nq = IDX_HEADS * IDX_DIM
    return pl.pallas_call(
        _in_kernel,
        grid=(B, S // tm),
        in_specs=[tok(D), per_b, per_b, full(g), full(w_in_p), full(qg), full(kvg), full(wq)],
        out_specs=[tok(C_CONV + C_POOL), tok(N_HEADS * D_KV_LAT), tok(D_KV_LAT), tok(nq), tok(LANES)],
        out_shape=[
            jax.ShapeDtypeStruct((B, S, C_CONV + C_POOL), F32),
            jax.ShapeDtypeStruct((B, S, N_HEADS * D_KV_LAT), BF16),
            jax.ShapeDtypeStruct((B, S, D_KV_LAT), BF16),
            jax.ShapeDtypeStruct((B, S, nq), BF16),
            jax.ShapeDtypeStruct((B, S, LANES), F32),
        ],
        compiler_params=_cparams(("arbitrary", "arbitrary")),
        name="in_proj",
    )(x, sc, sh, g, w_in_p, qg, kvg, wq)


def _cp_kernel(cu_ref, ck_ref, cb_ref, lg_ref, lb_ref, pw_ref, ps_ref, o_ref, pad_ref):
    S = cu_ref.shape[1]
    TT = CONV_TILE
    pad_ref[0:HALO, :] = jnp.zeros((HALO, C_CONV + C_POOL), F32)
    pad_ref[HALO:HALO + S, :] = cu_ref[0]
    lane = lax.broadcasted_iota(jnp.int32, (TT, C_POOL), 1)
    row = lax.broadcasted_iota(jnp.int32, (TT, C_POOL), 0)
    win_len = jnp.where(lane < POOL_GROUP, 2,
                        jnp.where(lane < 2 * POOL_GROUP, 4, jnp.where(lane < 3 * POOL_GROUP, 8, 16)))

    def tile(i, carry):
        t0 = pl.multiple_of(i * TT, TT)
        win = pad_ref[pl.ds(t0, TT + HALO), :]
        wc = win[:, 0:C_CONV]
        wp = win[:, C_CONV:C_CONV + C_POOL]
        acc = jnp.zeros((TT, C_CONV), F32)
        for j in range(CONV_WIDTH):
            o = HALO - (CONV_WIDTH - 1) + j
            acc = acc + ck_ref[j:j + 1, :] * wc[o:o + TT, :]
        v = acc + cb_ref[...]
        mu = jnp.mean(v, axis=-1, keepdims=True)
        vc = v - mu
        var = jnp.mean(vc * vc, axis=-1, keepdims=True)
        yn = vc * lax.rsqrt(var + EPS) * lg_ref[...] + lb_ref[...]
        o_ref[0, pl.ds(t0, TT), 0:C_CONV] = (yn * jax.nn.sigmoid(yn)).astype(BF16)
        lo = HALO - 15
        s2 = wp[lo + 1:, :] + wp[lo:-1, :]
        s4 = s2[2:, :] + s2[:-2, :]
        s8 = s4[4:, :] + s4[:-4, :]
        s16 = s8[8:, :] + s8[:-8, :]
        u = wp[HALO:, :]
        sw = jnp.where(lane < POOL_GROUP, s2[14:, :],
                       jnp.where(lane < 2 * POOL_GROUP, s4[12:, :],
                                 jnp.where(lane < 3 * POOL_GROUP, s8[8:, :], s16)))
        n_pos = jnp.minimum(t0 + row + 1, win_len).astype(F32)
        d = (sw / n_pos - u).astype(BF16)
        yp = jnp.dot(d, pw_ref[...], preferred_element_type=F32) * ps_ref[...]
        o_ref[0, pl.ds(t0, TT), C_CONV:C_CONV + C_POOL] = yp.astype(BF16)
        return carry

    lax.fori_loop(0, S // TT, tile, 0)


def _conv_pool(cu, ck, cb, lg, lb, pw_bd, ps):
    B, S, W = cu.shape
    full = lambda a: pl.BlockSpec(a.shape, lambda b: (0,) * a.ndim)
    return pl.pallas_call(
        _cp_kernel,
        grid=(B,),
        in_specs=[pl.BlockSpec((1, S, W), lambda b: (b, 0, 0)),
                  full(ck), full(cb), full(lg), full(lb), full(pw_bd), full(ps)],
        out_specs=pl.BlockSpec((1, S, W), lambda b: (b, 0, 0)),
        out_shape=jax.ShapeDtypeStruct((B, S, W), BF16),
        scratch_shapes=[pltpu.VMEM((S + HALO, W), F32)],
        compiler_params=_cparams(("arbitrary",)),
        name="conv_pool",
    )(cu, ck, cb, lg, lb, pw_bd, ps)


def _attn_kernel(qi_ref, kwq_ref, kwk_ref, qabs_ref, ckv_ref, o_ref, key_ref, aux_ref, *, topk, q0):
    QB = qi_ref.shape[1]
    LK = ckv_ref.shape[1]
    dn_nt = (((1,), (1,)), ((), ()))

    ki = kwk_ref[0, :, 0:IDX_DIM].astype(BF16)
    wi = kwq_ref[0, :, IDX_DIM:IDX_DIM + IDX_HEADS]
    qi = qi_ref[0]
    kidx = lax.broadcasted_iota(jnp.int32, (QB, LK), 1)
    qpos = q0 + lax.broadcasted_iota(jnp.int32, (QB, 1), 0)
    allowed = kidx < (qpos // CHUNK + 1) * CHUNK

    if LK <= topk:
        sel = allowed
    else:
        score = jnp.zeros((QB, LK), F32)
        for h in range(IDX_HEADS):
            rel = lax.dot_general(qi[:, h * IDX_DIM:(h + 1) * IDX_DIM], ki, dn_nt,
                                  preferred_element_type=F32)
            score = score + jnp.maximum(rel, 0.0) * wi[:, h:h + 1]
        score = score * ((IDX_HEADS * IDX_DIM) ** -0.5)
        score = jnp.where(score == 0.0, 0.0, score)
        score = jnp.where(allowed, score, -jnp.inf)
        bits = lax.bitcast_convert_type(score, jnp.int32)
        key_ref[...] = bits ^ ((bits >> 31) & jnp.int32(0x7FFFFFFF))

        kf = jnp.float32(topk)

        def count_ge(cand):
            return jnp.sum(jnp.where(key_ref[...] >= cand, 1.0, 0.0), axis=-1, keepdims=True)

        neg = jnp.full((QB, 1), jnp.iinfo(jnp.int32).min, jnp.int32)
        prefix = jnp.where(count_ge(jnp.zeros((QB, 1), jnp.int32)) >= kf, 0, neg)

        def bit_step(i, prefix):
            cand = prefix | (jnp.int32(1) << (30 - i))
            return jnp.where(count_ge(cand) >= kf, cand, prefix)

        thr = lax.fori_loop(0, 31, bit_step, prefix)

        key = key_ref[...]
        n_gt = jnp.sum(jnp.where(key > thr, 1.0, 0.0), axis=-1, keepdims=True)
        need = kf - n_gt
        aux_ref[...] = jnp.where(key == thr, kidx, jnp.int32(LK))
        nbits = (LK - 1).bit_length()

        def idx_step(i, p):
            cand = p | (jnp.int32(1) << (nbits - 1 - i))
            cnt = jnp.sum(jnp.where(aux_ref[...] < cand, 1.0, 0.0), axis=-1, keepdims=True)
            return jnp.where(cnt < need, cand, p)

        cut = lax.fori_loop(0, nbits, idx_step, jnp.zeros((QB, 1), jnp.int32))
        sel = ((key_ref[...] > thr) | (aux_ref[...] <= cut)) & allowed
    bias = jnp.where(sel, 0.0, -jnp.inf)

    ckv = ckv_ref[0]
    for h in range(N_HEADS):
        qa = qabs_ref[0, :, h * D_KV_LAT:(h + 1) * D_KV_LAT]
        logit = lax.dot_general(qa, ckv, dn_nt, preferred_element_type=F32) + bias
        m = jnp.max(logit, axis=-1, keepdims=True)
        p = jnp.exp(logit - m)
        l = jnp.sum(p, axis=-1, keepdims=True)
        ctx = jnp.dot(p.astype(BF16), ckv, preferred_element_type=F32)
        o_ref[0, :, h * D_KV_LAT:(h + 1) * D_KV_LAT] = (ctx / l).astype(BF16)


def _attention(qi, kw, qabs, ckv, qb):
    B, S, _ = ckv.shape
    topk = min(TOPK_MAX, S // 4)
    outs = []
    for j in range(S // qb):
        lk = (j + 1) * qb
        blk = lambda n, j=j: pl.BlockSpec((1, qb, n), lambda b: (b, j, 0))
        keys = lambda n, lk=lk: pl.BlockSpec((1, lk, n), lambda b: (b, 0, 0))
        outs.append(pl.pallas_call(
            functools.partial(_attn_kernel, topk=topk, q0=j * qb),
            grid=(B,),
            in_specs=[blk(IDX_HEADS * IDX_DIM), blk(LANES), keys(LANES), blk(N_HEADS * D_KV_LAT),
                      keys(D_KV_LAT)],
            out_specs=pl.BlockSpec((1, qb, N_HEADS * D_KV_LAT), lambda b: (b, 0, 0)),
            out_shape=jax.ShapeDtypeStruct((B, qb, N_HEADS * D_KV_LAT), BF16),
            scratch_shapes=[pltpu.VMEM((qb, lk), jnp.int32), pltpu.VMEM((qb, lk), jnp.int32)],
            compiler_params=_cparams(("arbitrary",)),
            name=f"dsa_attention_q{j}",
        )(qi, kw, kw, qabs, ckv))
    return jnp.concatenate(outs, axis=1)


def _out_kernel(ycp_ref, ctx_ref, x_ref, g1_ref, sc_ref, sh_ref, ng_ref, wcp_ref, wvo_ref,
                rw_ref, rb_ref, tri_ref, x1_ref, h2_ref, route_ref, cnt_ref, carry_ref):
    first = (pl.program_id(0) == 0) & (pl.program_id(1) == 0)

    @pl.when(first)
    def _():
        carry_ref[...] = jnp.zeros_like(carry_ref)

    mix = jnp.dot(ycp_ref[0], wcp_ref[...], preferred_element_type=F32)
    mix = mix + jnp.dot(ctx_ref[0], wvo_ref[...], preferred_element_type=F32)
    x1 = x_ref[0] + g1_ref[0] * mix
    x1_ref[0] = x1
    h2 = _rms(x1) * ng_ref[...] * (1.0 + sc_ref[0]) + sh_ref[0]
    h2_ref[0] = h2
    logits = jnp.dot(h2, rw_ref[...], precision=HIGHEST, preferred_element_type=F32) + rb_ref[...]

    tm = logits.shape[0]
    lane = lax.broadcasted_iota(jnp.int32, (tm, LANES), 1)
    lanef = lane.astype(F32)
    ninf = -jnp.inf
    big = jnp.float32(LANES)
    glog = jnp.where(lane < N_GROUPS, logits, ninf)
    gmax = jnp.max(glog, axis=-1, keepdims=True)
    gidx = jnp.min(jnp.where(glog == gmax, lanef, big), axis=-1, keepdims=True)
    g_p = 1.0 / jnp.sum(jnp.exp(glog - gmax), axis=-1, keepdims=True)
    lo = N_GROUPS + EXPERTS_PER_GROUP * gidx
    elog = jnp.where((lanef >= lo) & (lanef < lo + EXPERTS_PER_GROUP), logits, ninf)
    m1 = jnp.max(elog, axis=-1, keepdims=True)
    i1 = jnp.min(jnp.where(elog == m1, lanef, big), axis=-1, keepdims=True)
    elog2 = jnp.where(lanef == i1, ninf, elog)
    m2 = jnp.max(elog2, axis=-1, keepdims=True)
    i2 = jnp.min(jnp.where(elog2 == m2, lanef, big), axis=-1, keepdims=True)
    r = jnp.exp(m2 - m1)
    gate1 = g_p / (1.0 + r)
    gate2 = g_p * r / (1.0 + r)
    e1 = i1 - N_GROUPS
    e2 = i2 - N_GROUPS
    oh1 = lanef == e1
    oh2 = lanef == e2
    oh1f = jnp.where(oh1, 1.0, 0.0)
    oh2f = jnp.where(oh2, 1.0, 0.0)
    pre1 = jnp.dot(tri_ref[...], oh1f.astype(BF16), preferred_element_type=F32)
    pre2 = jnp.dot(tri_ref[...], oh2f.astype(BF16), preferred_element_type=F32)
    carry = carry_ref[0:1, :]
    cnt1 = jnp.sum(oh1f, axis=0, keepdims=True)
    cnt2 = jnp.sum(oh2f, axis=0, keepdims=True)
    rank1 = jnp.sum(jnp.where(oh1, carry + pre1, 0.0), axis=-1, keepdims=True)
    rank2 = jnp.sum(jnp.where(oh2, carry + cnt1 + pre2, 0.0), axis=-1, keepdims=True)
    total = carry + cnt1 + cnt2
    carry_ref[...] = jnp.broadcast_to(total, carry_ref.shape)
    cnt_ref[...] = jnp.broadcast_to(total, cnt_ref.shape)
    route = jnp.where(lane == 0, e1, jnp.where(lane == 1, e2, jnp.where(lane == 2, gate1, jnp.where(
        lane == 3, gate2, jnp.where(lane == 4, rank1, jnp.where(lane == 5, rank2, 0.0))))))
    route_ref[0] = route


def _out_proj(ycp, ctx, x, g1, sc2, sh2, ng, wcp, wvo, rw, rb, tm):
    B, S, D = x.shape
    tok = lambda n: pl.BlockSpec((1, tm, n), lambda b, i: (b, i, 0))
    per_b = pl.BlockSpec((1, 1, D), lambda b, i: (b, 0, 0))
    full = lambda a: pl.BlockSpec(a.shape, lambda b, i: (0,) * a.ndim)
    tri = jnp.tril(jnp.ones((tm, tm), BF16), -1)
    return pl.pallas_call(
        _out_kernel,
        grid=(B, S // tm),
        in_specs=[tok(C_CONV + C_POOL), tok(N_HEADS * D_KV_LAT), tok(D), per_b, per_b, per_b,
                  full(ng), full(wcp), full(wvo), full(rw), full(rb), full(tri)],
        out_specs=[tok(D), tok(D), tok(LANES), pl.BlockSpec((8, LANES), lambda b, i: (0, 0))],
        out_shape=[
            jax.ShapeDtypeStruct((B, S, D), F32),
            jax.ShapeDtypeStruct((B, S, D), F32),
            jax.ShapeDtypeStruct((B, S, LANES), F32),
            jax.ShapeDtypeStruct((8, LANES), F32),
        ],
        scratch_shapes=[pltpu.VMEM((8, LANES), F32)],
        compiler_params=_cparams(("arbitrary", "arbitrary")),
        name="out_proj_router",
    )(ycp, ctx, x, g1, sc2, sh2, ng, wcp, wvo, rw, rb, tri)


def _row_copy(src, i, dst, k, sem):
    return pltpu.make_async_copy(src.at[pl.ds(i, 1), :], dst.at[pl.ds(k, 1), :], sem)


def _dispatch_kernel(dst_ref, h_ref, xs_in_ref, xs_ref, sem):
    del xs_in_ref
    tm = h_ref.shape[0]

    def issue(r, c):
        _row_copy(h_ref, r, xs_ref, dst_ref[2 * r], sem).start()
        _row_copy(h_ref, r, xs_ref, dst_ref[2 * r + 1], sem).start()
        return c

    lax.fori_loop(0, tm, issue, 0)

    def drain(r, c):
        _row_copy(h_ref, r, xs_ref, dst_ref[2 * r], sem).wait()
        _row_copy(h_ref, r, xs_ref, dst_ref[2 * r + 1], sem).wait()
        return c

    lax.fori_loop(0, tm, drain, 0)


def _dispatch(dst_flat, h2, n_rows, tm):
    T, D = h2.shape
    xs0 = jnp.zeros((n_rows, D), F32)
    return pl.pallas_call(
        _dispatch_kernel,
        grid=(T // tm,),
        in_specs=[pl.BlockSpec((2 * tm,), lambda i: (i,), memory_space=pltpu.SMEM),
                  pl.BlockSpec((tm, D), lambda i: (i, 0)),
                  pl.BlockSpec(memory_space=pl.ANY)],
        out_specs=pl.BlockSpec(memory_space=pl.ANY),
        out_shape=jax.ShapeDtypeStruct((n_rows, D), F32),
        scratch_shapes=[pltpu.SemaphoreType.DMA(())],
        input_output_aliases={2: 0},
        compiler_params=_cparams(("arbitrary",)),
        name="moe_dispatch",
    )(dst_flat, h2, xs0)


def _ffn_kernel(be_ref, nu_ref, xs_ref, w13_ref, w2_ref, y_ref):
    i = pl.program_id(0)

    @pl.when(i < nu_ref[0])
    def _():
        xb = xs_ref[...].astype(BF16)
        h13 = jnp.dot(xb, w13_ref[0], preferred_element_type=F32)
        h1 = h13[:, 0:D_EXPERT]
        h3 = h13[:, D_EXPERT:2 * D_EXPERT]
        act = (h1 * jax.nn.sigmoid(h1)) * h3
        y_ref[...] = jnp.dot(act.astype(BF16), w2_ref[0], preferred_element_type=F32)

    @pl.when(i >= nu_ref[0])
    def _():
        y_ref[...] = jnp.zeros_like(y_ref)


def _expert_ffn(blk_exp, n_used, xs, w13, w2):
    P, D = xs.shape
    nblk = P // FFN_BLOCK
    grid_spec = pltpu.PrefetchScalarGridSpec(
        num_scalar_prefetch=2,
        grid=(nblk,),
        in_specs=[
            pl.BlockSpec((FFN_BLOCK, D), lambda i, be, nu: (jnp.minimum(i, nu[0] - 1), 0)),
            pl.BlockSpec((1, D, 2 * D_EXPERT), lambda i, be, nu: (be[i], 0, 0)),
            pl.BlockSpec((1, D_EXPERT, D), lambda i, be, nu: (be[i], 0, 0)),
        ],
        out_specs=pl.BlockSpec((FFN_BLOCK, D), lambda i, be, nu: (i, 0)),
    )
    return pl.pallas_call(
        _ffn_kernel,
        grid_spec=grid_spec,
        out_shape=jax.ShapeDtypeStruct((P, D), F32),
        compiler_params=_cparams(("arbitrary",)),
        name="expert_ffn",
    )(blk_exp, n_used, xs, w13, w2)


def _combine_kernel(dst_ref, x1_ref, route_ref, g2_ref, fg_ref, y_ref, o_ref, ya_ref, yb_ref, sem,
                    *, final_norm):
    tm = x1_ref.shape[1]

    def issue(r, c):
        _row_copy(y_ref, dst_ref[2 * r], ya_ref, r, sem.at[0]).start()
        _row_copy(y_ref, dst_ref[2 * r + 1], yb_ref, r, sem.at[1]).start()
        return c

    lax.fori_loop(0, tm, issue, 0)

    def drain(r, c):
        _row_copy(y_ref, dst_ref[2 * r], ya_ref, r, sem.at[0]).wait()
        _row_copy(y_ref, dst_ref[2 * r + 1], yb_ref, r, sem.at[1]).wait()
        return c

    lax.fori_loop(0, tm, drain, 0)

    route = route_ref[0]
    moe = ya_ref[...] * route[:, 2:3] + yb_ref[...] * route[:, 3:4]
    out = x1_ref[0] + g2_ref[0] * moe
    if final_norm:
        out = _rms(out) * fg_ref[...]
    o_ref[0] = out


def _combine(dst_flat, x1, route, g2, fg, y, tm, final_norm):
    B, S, D = x1.shape
    nt = S // tm
    tok = lambda n: pl.BlockSpec((1, tm, n), lambda b, i: (b, i, 0))
    return pl.pallas_call(
        functools.partial(_combine_kernel, final_norm=final_norm),
        grid=(B, nt),
        in_specs=[pl.BlockSpec((2 * tm,), lambda b, i: (b * nt + i,), memory_space=pltpu.SMEM),
                  tok(D), tok(LANES),
                  pl.BlockSpec((1, 1, D), lambda b, i: (b, 0, 0)),
                  pl.BlockSpec((1, D), lambda b, i: (0, 0)),
                  pl.BlockSpec(memory_space=pl.ANY)],
        out_specs=tok(D),
        out_shape=jax.ShapeDtypeStruct((B, S, D), F32),
        scratch_shapes=[pltpu.VMEM((tm, D), F32), pltpu.VMEM((tm, D), F32), pltpu.SemaphoreType.DMA((2,))],
        compiler_params=_cparams(("arbitrary", "arbitrary")),
        name="moe_combine",
    )(dst_flat, x1, route, g2, fg, y)


def _pick_tile(n, pref):
    t = min(n, pref)
    while n % t:
        t //= 2
    return t


def kernel(x, c, mod_w, mod_b, norm1_g, w_in, conv_k, conv_b, conv_ln_g, conv_ln_b, q_norm_g, kv_norm_g, w_uq, w_uk, w_uv, pool_w, pool_scale, w_out, norm2_g, router_g_w, router_g_b, router_e_w, router_e_b, exp_w1, exp_w3, exp_w2, final_g):
    B, S, D = x.shape
    L = mod_w.shape[0]
    T = B * S
    tm = _pick_tile(S, 512)
    qb = _pick_tile(S, 256)
    tmc = _pick_tile(S, 256)
    n_rows = 2 * T + N_EXPERTS * FFN_BLOCK
    nblk = n_rows // FFN_BLOCK

    mod = _modulation(c, mod_w, mod_b)
    wq_all, wvo_all = _fold_weights(w_uq, w_uk, w_uv, w_out)
    row = lambda a: a.reshape(1, -1)

    for l in range(L):
        sh1, sc1, g1, sh2, sc2, g2 = [mod[l, :, k * D:(k + 1) * D].reshape(B, 1, D) for k in range(6)]
        w_in_p = jnp.concatenate(
            [w_in[l, :, :N_IN - C_POOL], jnp.zeros((D, N_IN_PAD - N_IN), F32), w_in[l, :, N_IN - C_POOL:]],
            axis=1).astype(BF16)
        cu, qabs, ckv, qi, kw = _in_proj(x, sc1, sh1, row(norm1_g[l]), w_in_p, row(q_norm_g[l]),
                                         row(kv_norm_g[l]), wq_all[l], tm)
        pw_bd = jax.scipy.linalg.block_diag(*[pool_w[l, g] for g in range(pool_w.shape[1])]).astype(BF16)
        ycp = _conv_pool(cu, conv_k[l], row(conv_b[l]), row(conv_ln_g[l]), row(conv_ln_b[l]),
                         pw_bd, row(pool_scale[l]))
        ctx = _attention(qi, kw, qabs, ckv, qb)
        wcp = jnp.concatenate([w_out[l, :C_CONV], w_out[l, D - C_POOL:]], axis=0).astype(BF16)
        n_r = N_GROUPS + N_EXPERTS
        rw = jnp.concatenate([router_g_w[l], router_e_w[l].reshape(D, N_EXPERTS),
                              jnp.zeros((D, LANES - n_r), F32)], axis=1)
        rb = jnp.concatenate([router_g_b[l], router_e_b[l].reshape(N_EXPERTS),
                              jnp.zeros((LANES - n_r,), F32)]).reshape(1, LANES)
        x1, h2, route, cnt = _out_proj(ycp, ctx, x, g1, sc2, sh2, row(norm2_g[l]), wcp, wvo_all[l],
                                       rw, rb, tm)
        counts = cnt[0, :N_EXPERTS].astype(jnp.int32)
        pcounts = (counts + FFN_BLOCK - 1) // FFN_BLOCK * FFN_BLOCK
        pends = jnp.cumsum(pcounts)
        pstarts = pends - pcounts
        route2 = route.reshape(T, LANES)
        e_idx = route2[:, 0:2].astype(jnp.int32)
        dst = (pstarts[e_idx] + route2[:, 4:6].astype(jnp.int32)).reshape(2 * T)
        n_used = (pends[-1] // FFN_BLOCK).astype(jnp.int32).reshape(1)
        blk_start = jnp.arange(nblk, dtype=jnp.int32) * FFN_BLOCK
        blk_exp = jnp.minimum(jnp.sum((pends[None, :] <= blk_start[:, None]).astype(jnp.int32), axis=1),
                              N_EXPERTS - 1)
        xs = _dispatch(dst, h2.reshape(T, D), n_rows, tmc)
        w13 = jnp.concatenate([exp_w1[l], exp_w3[l]], axis=-1).astype(BF16)
        y = _expert_ffn(blk_exp, n_used, xs, w13, exp_w2[l].astype(BF16))
        x = _combine(dst, x1, route, g2, row(final_g), y, tmc, final_norm=(l == L - 1))
    return x
```

```python
import functools

import jax
import jax.numpy as jnp
from jax import lax
from jax.experimental import pallas as pl
from jax.experimental.pallas import tpu as pltpu

F32 = jnp.float32
BF16 = jnp.bfloat16
HIGHEST = lax.Precision.HIGHEST

EPS = 1e-6
CHUNK = 64
CONV_WIDTH = 31
C_CONV = 256
C_POOL = 256
POOL_GROUP = 64
N_HEADS = 8
HEAD_DIM = 64
D_Q_LAT = 256
D_KV_LAT = 128
IDX_HEADS = 8
IDX_DIM = 64
TOPK_MAX = 256
N_GROUPS = 4
EXPERTS_PER_GROUP = 8
N_EXPERTS = 32
D_EXPERT = 512
N_IN = 1736
N_IN_PAD = 1792
KW_OFF = 1408
POOL_OFF = 1536
LANES = 128
HALO = 32
CONV_TILE = 64
FFN_BLOCK = 256
ROW_DMA_UNROLL = 8
VMEM_LIMIT = 56 * 1024 * 1024


def _cparams(sem, row_dma=False):
    return pltpu.CompilerParams(dimension_semantics=sem, vmem_limit_bytes=VMEM_LIMIT,
                                disable_bounds_checks=row_dma)


def _mod_kernel(c_ref, w_ref, b_ref, o_ref):
    c = c_ref[...]
    cond = c * jax.nn.sigmoid(c)
    o_ref[0] = jnp.dot(cond, w_ref[0], precision=HIGHEST, preferred_element_type=F32) + b_ref[0]


def _modulation(c, mod_w, mod_b):
    L, D, D6 = mod_w.shape
    B = c.shape[0]
    nj = D6 // D
    return pl.pallas_call(
        _mod_kernel,
        grid=(L, nj),
        in_specs=[
            pl.BlockSpec((B, D), lambda l, j: (0, 0)),
            pl.BlockSpec((1, D, D), lambda l, j: (l, 0, j)),
            pl.BlockSpec((1, 1, D), lambda l, j: (l, 0, j)),
        ],
        out_specs=pl.BlockSpec((1, B, D), lambda l, j: (l, 0, j)),
        out_shape=jax.ShapeDtypeStruct((L, B, D6), F32),
        compiler_params=_cparams(("arbitrary", "arbitrary")),
        name="modulation",
    )(c, mod_w, mod_b.reshape(L, 1, D6))


def _fold_kernel(uq_ref, ukt_ref, uv_ref, wo_ref, wq_ref, wvo_ref):
    wq_ref[0] = jnp.dot(uq_ref[0, 0], ukt_ref[0, 0], precision=HIGHEST,
                        preferred_element_type=F32).astype(BF16)
    wvo_ref[0] = jnp.dot(uv_ref[0, 0], wo_ref[0, 0], precision=HIGHEST,
                         preferred_element_type=F32).astype(BF16)


def _fold_weights(w_uq, w_uk, w_uv, w_out):
    L = w_uq.shape[0]
    D = w_out.shape[-1]
    uq = jnp.transpose(w_uq, (0, 2, 1, 3))
    ukt = jnp.transpose(w_uk, (0, 2, 3, 1))
    uv = jnp.transpose(w_uv, (0, 2, 1, 3))
    wo = w_out[:, C_CONV:C_CONV + N_HEADS * HEAD_DIM, :].reshape(L, N_HEADS, HEAD_DIM, D)
    return pl.pallas_call(
        _fold_kernel,
        grid=(L, N_HEADS),
        in_specs=[
            pl.BlockSpec((1, 1, D_Q_LAT, HEAD_DIM), lambda l, h: (l, h, 0, 0)),
            pl.BlockSpec((1, 1, HEAD_DIM, D_KV_LAT), lambda l, h: (l, h, 0, 0)),
            pl.BlockSpec((1, 1, D_KV_LAT, HEAD_DIM), lambda l, h: (l, h, 0, 0)),
            pl.BlockSpec((1, 1, HEAD_DIM, D), lambda l, h: (l, h, 0, 0)),
        ],
        out_specs=[
            pl.BlockSpec((1, D_Q_LAT, D_KV_LAT), lambda l, h: (l, 0, h)),
            pl.BlockSpec((1, D_KV_LAT, D), lambda l, h: (l, h, 0)),
        ],
        out_shape=[
            jax.ShapeDtypeStruct((L, D_Q_LAT, N_HEADS * D_KV_LAT), BF16),
            jax.ShapeDtypeStruct((L, N_HEADS * D_KV_LAT, D), BF16),
        ],
        compiler_params=_cparams(("arbitrary", "arbitrary")),
        name="fold_weights",
    )(uq, ukt, uv, wo)


def _rms(v):
    return v * lax.rsqrt(jnp.mean(v * v, axis=-1, keepdims=True) + EPS)


def _in_kernel(x_ref, sc_ref, sh_ref, g_ref, w_ref, qg_ref, kvg_ref, wq_ref,
               cu_ref, qabs_ref, ckv_ref, qi_ref, kw_ref):
    x = x_ref[0]
    h = _rms(x) * g_ref[...] * (1.0 + sc_ref[0]) + sh_ref[0]
    z = jnp.dot(h.astype(BF16), w_ref[...], preferred_element_type=F32)
    a_val = z[:, 0:C_CONV]
    a_gate = z[:, C_CONV:2 * C_CONV]
    cu_ref[0, :, 0:C_CONV] = a_val * jax.nn.sigmoid(a_gate)
    cu_ref[0, :, C_CONV:C_CONV + C_POOL] = z[:, POOL_OFF:POOL_OFF + C_POOL]
    o = 2 * C_CONV
    cq = _rms(z[:, o:o + D_Q_LAT]) * qg_ref[...]
    qabs = jnp.dot(cq.astype(BF16), wq_ref[...], preferred_element_type=F32)
    qabs_ref[0] = (qabs * (HEAD_DIM ** -0.5)).astype(BF16)
    o += D_Q_LAT
    ckv_ref[0] = (_rms(z[:, o:o + D_KV_LAT]) * kvg_ref[...]).astype(BF16)
    o += D_KV_LAT
    qi_ref[0] = z[:, o:o + IDX_HEADS * IDX_DIM].astype(BF16)
    kw_ref[0] = z[:, KW_OFF:KW_OFF + LANES]


def _in_proj(x, sc, sh, g, w_in_p, qg, kvg, wq, tm):
    B, S, D = x.shape
    tok = lambda n: pl.BlockSpec((1, tm, n), lambda b, i: (b, i, 0))
    per_b = pl.BlockSpec((1, 1, D), lambda b, i: (b, 0, 0))
    full = lambda a: pl.BlockSpec(a.shape, lambda b, i: (0,) * a.ndim)
    nq = IDX_HEADS * IDX_DIM
    return pl.pallas_call(
        _in_kernel,
        grid=(B, S // tm),
        in_specs=[tok(D), per_b, per_b, full(g), full(w_in_p), full(qg), full(kvg), full(wq)],
        out_specs=[tok(C_CONV + C_POOL), tok(N_HEADS * D_KV_LAT), tok(D_KV_LAT), tok(nq), tok(LANES)],
        out_shape=[
            jax.ShapeDtypeStruct((B, S, C_CONV + C_POOL), F32),
            jax.ShapeDtypeStruct((B, S, N_HEADS * D_KV_LAT), BF16),
            jax.ShapeDtypeStruct((B, S, D_KV_LAT), BF16),
            jax.ShapeDtypeStruct((B, S, nq), BF16),
            jax.ShapeDtypeStruct((B, S, LANES), F32),
        ],
        compiler_params=_cparams(("arbitrary", "arbitrary")),
        name="in_proj",
    )(x, sc, sh, g, w_in_p, qg, kvg, wq)


def _cp_kernel(cu_ref, ck_ref, cb_ref, lg_ref, lb_ref, pw_ref, ps_ref, o_ref, pad_ref):
    S = cu_ref.shape[1]
    TT = CONV_TILE
    pad_ref[0:HALO, :] = jnp.zeros((HALO, C_CONV + C_POOL), F32)
    pad_ref[HALO:HALO + S, :] = cu_ref[0]
    lane = lax.broadcasted_iota(jnp.int32, (TT, C_POOL), 1)
    row = lax.broadcasted_iota(jnp.int32, (TT, C_POOL), 0)
    win_len = jnp.where(lane < POOL_GROUP, 2,
                        jnp.where(lane < 2 * POOL_GROUP, 4, jnp.where(lane < 3 * POOL_GROUP, 8, 16)))

    def tile(i, carry):
        t0 = pl.multiple_of(i * TT, TT)
        win = pad_ref[pl.ds(t0, TT + HALO), :]
        wc = win[:, 0:C_CONV]
        wp = win[:, C_CONV:C_CONV + C_POOL]
        acc = jnp.zeros((TT, C_CONV), F32)
        for j in range(CONV_WIDTH):
            o = HALO - (CONV_WIDTH - 1) + j
            acc = acc + ck_ref[j:j + 1, :] * wc[o:o + TT, :]
        v = acc + cb_ref[...]
        mu = jnp.mean(v, axis=-1, keepdims=True)
        vc = v - mu
        var = jnp.mean(vc * vc, axis=-1, keepdims=True)
        yn = vc * lax.rsqrt(var + EPS) * lg_ref[...] + lb_ref[...]
        o_ref[0, pl.ds(t0, TT), 0:C_CONV] = (yn * jax.nn.sigmoid(yn)).astype(BF16)
        lo = HALO - 15
        s2 = wp[lo + 1:, :] + wp[lo:-1, :]
        s4 = s2[2:, :] + s2[:-2, :]
        s8 = s4[4:, :] + s4[:-4, :]
        s16 = s8[8:, :] + s8[:-8, :]
        u = wp[HALO:, :]
        sw = jnp.where(lane < POOL_GROUP, s2[14:, :],
                       jnp.where(lane < 2 * POOL_GROUP, s4[12:, :],
                                 jnp.where(lane < 3 * POOL_GROUP, s8[8:, :], s16)))
        n_pos = jnp.minimum(t0 + row + 1, win_len).astype(F32)
        d = (sw / n_pos - u).astype(BF16)
        yp = jnp.dot(d, pw_ref[...], preferred_element_type=F32) * ps_ref[...]
        o_ref[0, pl.ds(t0, TT), C_CONV:C_CONV + C_POOL] = yp.astype(BF16)
        return carry

    lax.fori_loop(0, S // TT, tile, 0)


def _conv_pool(cu, ck, cb, lg, lb, pw_bd, ps):
    B, S, W = cu.shape
    full = lambda a: pl.BlockSpec(a.shape, lambda b: (0,) * a.ndim)
    return pl.pallas_call(
        _cp_kernel,
        grid=(B,),
        in_specs=[pl.BlockSpec((1, S, W), lambda b: (b, 0, 0)),
                  full(ck), full(cb), full(lg), full(lb), full(pw_bd), full(ps)],
        out_specs=pl.BlockSpec((1, S, W), lambda b: (b, 0, 0)),
        out_shape=jax.ShapeDtypeStruct((B, S, W), BF16),
        scratch_shapes=[pltpu.VMEM((S + HALO, W), F32)],
        compiler_params=_cparams(("arbitrary",)),
        name="conv_pool",
    )(cu, ck, cb, lg, lb, pw_bd, ps)


def _attn_kernel(qi_ref, kwq_ref, kwk_ref, qabs_ref, ckv_ref, o_ref, key_ref, aux_ref, *, topk, q0):
    QB = qi_ref.shape[1]
    LK = ckv_ref.shape[1]
    dn_nt = (((1,), (1,)), ((), ()))

    ki = kwk_ref[0, :, 0:IDX_DIM].astype(BF16)
    wi = kwq_ref[0, :, IDX_DIM:IDX_DIM + IDX_HEADS]
    qi = qi_ref[0]
    kidx = lax.broadcasted_iota(jnp.int32, (QB, LK), 1)
    qpos = q0 + lax.broadcasted_iota(jnp.int32, (QB, 1), 0)
    allowed = kidx < (qpos // CHUNK + 1) * CHUNK

    if LK <= topk:
        sel = allowed
    else:
        score = jnp.zeros((QB, LK), F32)
        for h in range(IDX_HEADS):
            rel = lax.dot_general(qi[:, h * IDX_DIM:(h + 1) * IDX_DIM], ki, dn_nt,
                                  preferred_element_type=F32)
            score = score + jnp.maximum(rel, 0.0) * wi[:, h:h + 1]
        score = score * ((IDX_HEADS * IDX_DIM) ** -0.5)
        score = jnp.where(score == 0.0, 0.0, score)
        score = jnp.where(allowed, score, -jnp.inf)
        bits = lax.bitcast_convert_type(score, jnp.int32)
        key_ref[...] = bits ^ ((bits >> 31) & jnp.int32(0x7FFFFFFF))

        kf = jnp.float32(topk)

        def count_ge(cand):
            return jnp.sum(jnp.where(key_ref[...] >= cand, 1.0, 0.0), axis=-1, keepdims=True)

        neg = jnp.full((QB, 1), jnp.iinfo(jnp.int32).min, jnp.int32)
        prefix = jnp.where(count_ge(jnp.zeros((QB, 1), jnp.int32)) >= kf, 0, neg)

        def bit_step(i, prefix):
            cand = prefix | (jnp.int32(1) << (30 - i))
            return jnp.where(count_ge(cand) >= kf, cand, prefix)

        thr = lax.fori_loop(0, 31, bit_step, prefix)

        key = key_ref[...]
        n_gt = jnp.sum(jnp.where(key > thr, 1.0, 0.0), axis=-1, keepdims=True)
        need = kf - n_gt
        aux_ref[...] = jnp.where(key == thr, kidx, jnp.int32(LK))
        nbits = (LK - 1).bit_length()

        def idx_step(i, p):
            cand = p | (jnp.int32(1) << (nbits - 1 - i))
            cnt = jnp.sum(jnp.where(aux_ref[...] < cand, 1.0, 0.0), axis=-1, keepdims=True)
            return jnp.where(cnt < need, cand, p)

        cut = lax.fori_loop(0, nbits, idx_step, jnp.zeros((QB, 1), jnp.int32))
        sel = ((key_ref[...] > thr) | (aux_ref[...] <= cut)) & allowed
    bias = jnp.where(sel, 0.0, -jnp.inf)

    ckv = ckv_ref[0]
    for h in range(N_HEADS):
        qa = qabs_ref[0, :, h * D_KV_LAT:(h + 1) * D_KV_LAT]
        logit = lax.dot_general(qa, ckv, dn_nt, preferred_element_type=F32) + bias
        m = jnp.max(logit, axis=-1, keepdims=True)
        p = jnp.exp(logit - m)
        l = jnp.sum(p, axis=-1, keepdims=True)
        ctx = jnp.dot(p.astype(BF16), ckv, preferred_element_type=F32)
        o_ref[0, :, h * D_KV_LAT:(h + 1) * D_KV_LAT] = (ctx / l).astype(BF16)


def _attention(qi, kw, qabs, ckv, qb):
    B, S, _ = ckv.shape
    topk = min(TOPK_MAX, S // 4)
    outs = []
    for j in range(S // qb):
        lk = (j + 1) * qb
        blk = lambda n, j=j: pl.BlockSpec((1, qb, n), lambda b: (b, j, 0))
        keys = lambda n, lk=lk: pl.BlockSpec((1, lk, n), lambda b: (b, 0, 0))
        outs.append(pl.pallas_call(
            functools.partial(_attn_kernel, topk=topk, q0=j * qb),
            grid=(B,),
            in_specs=[blk(IDX_HEADS * IDX_DIM), blk(LANES), keys(LANES), blk(N_HEADS * D_KV_LAT),
                      keys(D_KV_LAT)],
            out_specs=pl.BlockSpec((1, qb, N_HEADS * D_KV_LAT), lambda b: (b, 0, 0)),
            out_shape=jax.ShapeDtypeStruct((B, qb, N_HEADS * D_KV_LAT), BF16),
            scratch_shapes=[pltpu.VMEM((qb, lk), jnp.int32), pltpu.VMEM((qb, lk), jnp.int32)],
            compiler_params=_cparams(("arbitrary",)),
            name=f"dsa_attention_q{j}",
        )(qi, kw, kw, qabs, ckv))
    return jnp.concatenate(outs, axis=1)


def _out_kernel(ycp_ref, ctx_ref, x_ref, g1_ref, sc_ref, sh_ref, ng_ref, wcp_ref, wvo_ref,
                rw_ref, rb_ref, tri_ref, x1_ref, h2_ref, route_ref, cnt_ref, carry_ref):
    first = (pl.program_id(0) == 0) & (pl.program_id(1) == 0)

    @pl.when(first)
    def _():
        carry_ref[...] = jnp.zeros_like(carry_ref)

    mix = jnp.dot(ycp_ref[0], wcp_ref[...], preferred_element_type=F32)
    mix = mix + jnp.dot(ctx_ref[0], wvo_ref[...], preferred_element_type=F32)
    x1 = x_ref[0] + g1_ref[0] * mix
    x1_ref[0] = x1
    h2 = _rms(x1) * ng_ref[...] * (1.0 + sc_ref[0]) + sh_ref[0]
    h2_ref[0] = h2
    rw = rw_ref[...]
    rw_hi = rw.astype(BF16)
    rw_lo = (rw - rw_hi.astype(F32)).astype(BF16)
    h_hi = h2.astype(BF16)
    h_lo = (h2 - h_hi.astype(F32)).astype(BF16)
    logits = (jnp.dot(h_hi, rw_hi, preferred_element_type=F32)
              + (jnp.dot(h_lo, rw_hi, preferred_element_type=F32)
                 + jnp.dot(h_hi, rw_lo, preferred_element_type=F32))) + rb_ref[...]

    lt = logits.T
    tm = lt.shape[1]
    rowi = lax.broadcasted_iota(jnp.int32, (LANES, tm), 0)
    rowf = rowi.astype(F32)
    ninf = -jnp.inf
    big = jnp.float32(LANES)
    glog = jnp.where(rowi < N_GROUPS, lt, ninf)
    gmax = jnp.max(glog, axis=0, keepdims=True)
    gidx = jnp.min(jnp.where(glog == gmax, rowf, big), axis=0, keepdims=True)
    g_p = 1.0 / jnp.sum(jnp.exp(glog - gmax), axis=0, keepdims=True)
    lo = N_GROUPS + EXPERTS_PER_GROUP * gidx
    elog = jnp.where((rowf >= lo) & (rowf < lo + EXPERTS_PER_GROUP), lt, ninf)
    m1 = jnp.max(elog, axis=0, keepdims=True)
    i1 = jnp.min(jnp.where(elog == m1, rowf, big), axis=0, keepdims=True)
    elog2 = jnp.where(rowf == i1, ninf, elog)
    m2 = jnp.max(elog2, axis=0, keepdims=True)
    i2 = jnp.min(jnp.where(elog2 == m2, rowf, big), axis=0, keepdims=True)
    r = jnp.exp(m2 - m1)
    gate1 = g_p / (1.0 + r)
    gate2 = g_p * r / (1.0 + r)
    e1 = i1 - N_GROUPS
    e2 = i2 - N_GROUPS
    oh1 = rowf == e1
    oh2 = rowf == e2
    oh1f = jnp.where(oh1, 1.0, 0.0)
    oh2f = jnp.where(oh2, 1.0, 0.0)
    pre1 = jnp.dot(oh1f.astype(BF16), tri_ref[...], preferred_element_type=F32)
    pre2 = jnp.dot(oh2f.astype(BF16), tri_ref[...], preferred_element_type=F32)
    carry = carry_ref[:, 0:1]
    cnt1 = jnp.sum(oh1f, axis=1, keepdims=True)
    cnt2 = jnp.sum(oh2f, axis=1, keepdims=True)
    rank1 = jnp.sum(jnp.where(oh1, carry + pre1, 0.0), axis=0, keepdims=True)
    rank2 = jnp.sum(jnp.where(oh2, carry + cnt1 + pre2, 0.0), axis=0, keepdims=True)
    total = carry + cnt1 + cnt2
    carry_ref[...] = jnp.broadcast_to(total, carry_ref.shape)
    cnt_ref[...] = jnp.broadcast_to(total, cnt_ref.shape)
    route_t = jnp.where(rowi == 0, e1, jnp.where(rowi == 1, e2, jnp.where(rowi == 2, gate1, jnp.where(
        rowi == 3, gate2, jnp.where(rowi == 4, rank1, jnp.where(rowi == 5, rank2, 0.0))))))
    route_ref[0] = route_t.T


def _out_proj(ycp, ctx, x, g1, sc2, sh2, ng, wcp, wvo, rw, rb, tm):
    B, S, D = x.shape
    tok = lambda n: pl.BlockSpec((1, tm, n), lambda b, i: (b, i, 0))
    per_b = pl.BlockSpec((1, 1, D), lambda b, i: (b, 0, 0))
    full = lambda a: pl.BlockSpec(a.shape, lambda b, i: (0,) * a.ndim)
    tri = jnp.triu(jnp.ones((tm, tm), BF16), 1)
    return pl.pallas_call(
        _out_kernel,
        grid=(B, S // tm),
        in_specs=[tok(C_CONV + C_POOL), tok(N_HEADS * D_KV_LAT), tok(D), per_b, per_b, per_b,
                  full(ng), full(wcp), full(wvo), full(rw), full(rb), full(tri)],
        out_specs=[tok(D), tok(D), tok(LANES), pl.BlockSpec((LANES, LANES), lambda b, i: (0, 0))],
        out_shape=[
            jax.ShapeDtypeStruct((B, S, D), F32),
            jax.ShapeDtypeStruct((B, S, D), F32),
            jax.ShapeDtypeStruct((B, S, LANES), F32),
            jax.ShapeDtypeStruct((LANES, LANES), F32),
        ],
        scratch_shapes=[pltpu.VMEM((LANES, LANES), F32)],
        compiler_params=_cparams(("arbitrary", "arbitrary")),
        name="out_proj_router",
    )(ycp, ctx, x, g1, sc2, sh2, ng, wcp, wvo, rw, rb, tri)


def _row_copy(src, i, dst, k, sem):
    return pltpu.make_async_copy(src.at[pl.ds(i, 1), :], dst.at[pl.ds(k, 1), :], sem)


def _dispatch_kernel(dst_ref, h_ref, xs_in_ref, xs_ref, sem):
    del xs_in_ref
    tm = h_ref.shape[0]

    def issue(r, c):
        _row_copy(h_ref, r, xs_ref, dst_ref[2 * r], sem).start()
        _row_copy(h_ref, r, xs_ref, dst_ref[2 * r + 1], sem).start()
        return c

    lax.fori_loop(0, tm, issue, 0, unroll=ROW_DMA_UNROLL)

    def drain(r, c):
        _row_copy(h_ref, r, xs_ref, dst_ref[2 * r], sem).wait()
        _row_copy(h_ref, r, xs_ref, dst_ref[2 * r + 1], sem).wait()
        return c

    lax.fori_loop(0, tm, drain, 0, unroll=ROW_DMA_UNROLL)


def _dispatch(dst_flat, h2, n_rows, tm):
    T, D = h2.shape
    xs0 = jnp.zeros((n_rows, D), F32)
    return pl.pallas_call(
        _dispatch_kernel,
        grid=(T // tm,),
        in_specs=[pl.BlockSpec((2 * tm,), lambda i: (i,), memory_space=pltpu.SMEM),
                  pl.BlockSpec((tm, D), lambda i: (i, 0)),
                  pl.BlockSpec(memory_space=pl.ANY)],
        out_specs=pl.BlockSpec(memory_space=pl.ANY),
        out_shape=jax.ShapeDtypeStruct((n_rows, D), F32),
        scratch_shapes=[pltpu.SemaphoreType.DMA(())],
        input_output_aliases={2: 0},
        compiler_params=_cparams(("arbitrary",), row_dma=True),
        name="moe_dispatch",
    )(dst_flat, h2, xs0)


def _ffn_kernel(be_ref, nu_ref, xs_ref, w1_ref, w3_ref, w2_ref, y_ref, w13_scr, w2_scr):
    i = pl.program_id(0)
    new_expert = (i == 0) | (be_ref[i] != be_ref[jnp.maximum(i - 1, 0)])

    @pl.when(new_expert)
    def _():
        w13_scr[:, 0:D_EXPERT] = w1_ref[0, 0].astype(BF16)
        w13_scr[:, D_EXPERT:2 * D_EXPERT] = w3_ref[0, 0].astype(BF16)
        w2_scr[...] = w2_ref[0, 0].astype(BF16)

    @pl.when(i < nu_ref[0])
    def _():
        xb = xs_ref[...].astype(BF16)
        h13 = jnp.dot(xb, w13_scr[...], preferred_element_type=F32)
        h1 = h13[:, 0:D_EXPERT]
        h3 = h13[:, D_EXPERT:2 * D_EXPERT]
        act = (h1 * jax.nn.sigmoid(h1)) * h3
        y_ref[...] = jnp.dot(act.astype(BF16), w2_scr[...], preferred_element_type=F32)

    @pl.when(i >= nu_ref[0])
    def _():
        y_ref[...] = jnp.zeros_like(y_ref)


def _expert_ffn(blk_exp, n_used, xs, w1, w3, w2, layer):
    P, D = xs.shape
    nblk = P // FFN_BLOCK
    grid_spec = pltpu.PrefetchScalarGridSpec(
        num_scalar_prefetch=2,
        grid=(nblk,),
        in_specs=[
            pl.BlockSpec((FFN_BLOCK, D), lambda i, be, nu: (jnp.maximum(jnp.minimum(i, nu[0] - 1), 0), 0)),
            pl.BlockSpec((1, 1, D, D_EXPERT), lambda i, be, nu: (layer, be[i], 0, 0)),
            pl.BlockSpec((1, 1, D, D_EXPERT), lambda i, be, nu: (layer, be[i], 0, 0)),
            pl.BlockSpec((1, 1, D_EXPERT, D), lambda i, be, nu: (layer, be[i], 0, 0)),
        ],
        out_specs=pl.BlockSpec((FFN_BLOCK, D), lambda i, be, nu: (i, 0)),
        scratch_shapes=[pltpu.VMEM((D, 2 * D_EXPERT), BF16), pltpu.VMEM((D_EXPERT, D), BF16)],
    )
    return pl.pallas_call(
        _ffn_kernel,
        grid_spec=grid_spec,
        out_shape=jax.ShapeDtypeStruct((P, D), F32),
        compiler_params=_cparams(("arbitrary",)),
        name="expert_ffn",
    )(blk_exp, n_used, xs, w1, w3, w2)


def _combine_kernel(dst_ref, x1_ref, route_ref, g2_ref, fg_ref, y_ref, o_ref, ya_ref, yb_ref, sem,
                    *, final_norm):
    tm = x1_ref.shape[1]

    def issue(r, c):
        _row_copy(y_ref, dst_ref[2 * r], ya_ref, r, sem.at[0]).start()
        _row_copy(y_ref, dst_ref[2 * r + 1], yb_ref, r, sem.at[1]).start()
        return c

    lax.fori_loop(0, tm, issue, 0, unroll=ROW_DMA_UNROLL)

    def drain(r, c):
        _row_copy(y_ref, dst_ref[2 * r], ya_ref, r, sem.at[0]).wait()
        _row_copy(y_ref, dst_ref[2 * r + 1], yb_ref, r, sem.at[1]).wait()
        return c

    lax.fori_loop(0, tm, drain, 0, unroll=ROW_DMA_UNROLL)

    route = route_ref[0]
    moe = ya_ref[...] * route[:, 2:3] + yb_ref[...] * route[:, 3:4]
    out = x1_ref[0] + g2_ref[0] * moe
    if final_norm:
        out = _rms(out) * fg_ref[...]
    o_ref[0] = out


def _combine(dst_flat, x1, route, g2, fg, y, tm, final_norm):
    B, S, D = x1.shape
    nt = S // tm
    tok = lambda n: pl.BlockSpec((1, tm, n), lambda b, i: (b, i, 0))
    return pl.pallas_call(
        functools.partial(_combine_kernel, final_norm=final_norm),
        grid=(B, nt),
        in_specs=[pl.BlockSpec((2 * tm,), lambda b, i: (b * nt + i,), memory_space=pltpu.SMEM),
                  tok(D), tok(LANES),
                  pl.BlockSpec((1, 1, D), lambda b, i: (b, 0, 0)),
                  pl.BlockSpec((1, D), lambda b, i: (0, 0)),
                  pl.BlockSpec(memory_space=pl.ANY)],
        out_specs=tok(D),
        out_shape=jax.ShapeDtypeStruct((B, S, D), F32),
        scratch_shapes=[pltpu.VMEM((tm, D), F32), pltpu.VMEM((tm, D), F32), pltpu.SemaphoreType.DMA((2,))],
        compiler_params=_cparams(("arbitrary", "arbitrary"), row_dma=True),
        name="moe_combine",
    )(dst_flat, x1, route, g2, fg, y)


def _pick_tile(n, pref):
    t = min(n, pref)
    while n % t:
        t //= 2
    return t


def kernel(x, c, mod_w, mod_b, norm1_g, w_in, conv_k, conv_b, conv_ln_g, conv_ln_b, q_norm_g, kv_norm_g, w_uq, w_uk, w_uv, pool_w, pool_scale, w_out, norm2_g, router_g_w, router_g_b, router_e_w, router_e_b, exp_w1, exp_w3, exp_w2, final_g):
    B, S, D = x.shape
    L = mod_w.shape[0]
    T = B * S
    tm = _pick_tile(S, 512)
    qb = _pick_tile(S, 256)
    tmc = _pick_tile(S, 512)
    n_rows = 2 * T + N_EXPERTS * FFN_BLOCK
    nblk = n_rows // FFN_BLOCK

    mod = _modulation(c, mod_w, mod_b)
    wq_all, wvo_all = _fold_weights(w_uq, w_uk, w_uv, w_out)
    row = lambda a: a.reshape(1, -1)

    for l in range(L):
        sh1, sc1, g1, sh2, sc2, g2 = [mod[l, :, k * D:(k + 1) * D].reshape(B, 1, D) for k in range(6)]
        w_in_p = jnp.concatenate(
            [w_in[l, :, :N_IN - C_POOL], jnp.zeros((D, N_IN_PAD - N_IN), F32), w_in[l, :, N_IN - C_POOL:]],
            axis=1).astype(BF16)
        cu, qabs, ckv, qi, kw = _in_proj(x, sc1, sh1, row(norm1_g[l]), w_in_p, row(q_norm_g[l]),
                                         row(kv_norm_g[l]), wq_all[l], tm)
        pw_bd = jax.scipy.linalg.block_diag(*[pool_w[l, g] for g in range(pool_w.shape[1])]).astype(BF16)
        ycp = _conv_pool(cu, conv_k[l], row(conv_b[l]), row(conv_ln_g[l]), row(conv_ln_b[l]),
                         pw_bd, row(pool_scale[l]))
        ctx = _attention(qi, kw, qabs, ckv, qb)
        wcp = jnp.concatenate([w_out[l, :C_CONV], w_out[l, D - C_POOL:]], axis=0).astype(BF16)
        n_r = N_GROUPS + N_EXPERTS
        rw = jnp.concatenate([router_g_w[l], router_e_w[l].reshape(D, N_EXPERTS),
                              jnp.zeros((D, LANES - n_r), F32)], axis=1)
        rb = jnp.concatenate([router_g_b[l], router_e_b[l].reshape(N_EXPERTS),
                              jnp.zeros((LANES - n_r,), F32)]).reshape(1, LANES)
        x1, h2, route, cnt = _out_proj(ycp, ctx, x, g1, sc2, sh2, row(norm2_g[l]), wcp, wvo_all[l],
                                       rw, rb, tm)
        counts = cnt[:N_EXPERTS, 0].astype(jnp.int32)
        pcounts = (counts + FFN_BLOCK - 1) // FFN_BLOCK * FFN_BLOCK
        pends = jnp.cumsum(pcounts)
        pstarts = pends - pcounts
        route2 = route.reshape(T, LANES)
        e_idx = route2[:, 0:2].astype(jnp.int32)
        dst = (pstarts[e_idx] + route2[:, 4:6].astype(jnp.int32)).reshape(2 * T)
        n_used = (pends[-1] // FFN_BLOCK).astype(jnp.int32).reshape(1)
        blk_start = jnp.arange(nblk, dtype=jnp.int32) * FFN_BLOCK
        blk_exp = jnp.minimum(jnp.sum((pends[None, :] <= blk_start[:, None]).astype(jnp.int32), axis=1),
                              N_EXPERTS - 1)
        xs = _dispatch(dst, h2.reshape(T, D), n_rows, tmc)
        y = _expert_ffn(blk_exp, n_used, xs, exp_w1, exp_w3, exp_w2, l)
        x = _combine(dst, x1, route, g2, row(final_g), y, tmc, final_norm=(l == L - 1))
    return x
```

```python
import functools

import jax
import jax.numpy as jnp
from jax import lax
from jax.experimental import pallas as pl
from jax.experimental.pallas import tpu as pltpu

F32 = jnp.float32
BF16 = jnp.bfloat16
HIGHEST = lax.Precision.HIGHEST

EPS = 1e-6
CHUNK = 64
CONV_WIDTH = 31
C_CONV = 256
C_POOL = 256
POOL_GROUP = 64
N_HEADS = 8
HEAD_DIM = 64
D_Q_LAT = 256
D_KV_LAT = 128
IDX_HEADS = 8
IDX_DIM = 64
TOPK_MAX = 256
N_GROUPS = 4
EXPERTS_PER_GROUP = 8
N_EXPERTS = 32
D_EXPERT = 512
N_IN = 1736
N_IN_PAD = 1792
KW_OFF = 1408
POOL_OFF = 1536
LANES = 128
HALO = 32
CONV_TILE = 64
FFN_BLOCK = 256
ROW_DMA_UNROLL = 8
VMEM_LIMIT = 56 * 1024 * 1024


def _cparams(sem, row_dma=False):
    return pltpu.CompilerParams(dimension_semantics=sem, vmem_limit_bytes=VMEM_LIMIT,
                                disable_bounds_checks=row_dma)


def _mod_kernel(c_ref, w_ref, b_ref, o_ref):
    c = c_ref[...]
    cond = c * jax.nn.sigmoid(c)
    o_ref[0] = jnp.dot(cond, w_ref[0], precision=HIGHEST, preferred_element_type=F32) + b_ref[0]


def _modulation(c, mod_w, mod_b):
    L, D, D6 = mod_w.shape
    B = c.shape[0]
    nj = D6 // D
    return pl.pallas_call(
        _mod_kernel,
        grid=(L, nj),
        in_specs=[
            pl.BlockSpec((B, D), lambda l, j: (0, 0)),
            pl.BlockSpec((1, D, D), lambda l, j: (l, 0, j)),
            pl.BlockSpec((1, 1, D), lambda l, j: (l, 0, j)),
        ],
        out_specs=pl.BlockSpec((1, B, D), lambda l, j: (l, 0, j)),
        out_shape=jax.ShapeDtypeStruct((L, B, D6), F32),
        compiler_params=_cparams(("arbitrary", "arbitrary")),
        name="modulation",
    )(c, mod_w, mod_b.reshape(L, 1, D6))


def _fold_kernel(uq_ref, ukt_ref, uv_ref, wo_ref, wq_ref, wvo_ref):
    wq_ref[0] = jnp.dot(uq_ref[0, 0], ukt_ref[0, 0], precision=HIGHEST,
                        preferred_element_type=F32).astype(BF16)
    wvo_ref[0] = jnp.dot(uv_ref[0, 0], wo_ref[0, 0], precision=HIGHEST,
                         preferred_element_type=F32).astype(BF16)


def _fold_weights(w_uq, w_uk, w_uv, w_out):
    L = w_uq.shape[0]
    D = w_out.shape[-1]
    uq = jnp.transpose(w_uq, (0, 2, 1, 3))
    ukt = jnp.transpose(w_uk, (0, 2, 3, 1))
    uv = jnp.transpose(w_uv, (0, 2, 1, 3))
    wo = w_out[:, C_CONV:C_CONV + N_HEADS * HEAD_DIM, :].reshape(L, N_HEADS, HEAD_DIM, D)
    return pl.pallas_call(
        _fold_kernel,
        grid=(L, N_HEADS),
        in_specs=[
            pl.BlockSpec((1, 1, D_Q_LAT, HEAD_DIM), lambda l, h: (l, h, 0, 0)),
            pl.BlockSpec((1, 1, HEAD_DIM, D_KV_LAT), lambda l, h: (l, h, 0, 0)),
            pl.BlockSpec((1, 1, D_KV_LAT, HEAD_DIM), lambda l, h: (l, h, 0, 0)),
            pl.BlockSpec((1, 1, HEAD_DIM, D), lambda l, h: (l, h, 0, 0)),
        ],
        out_specs=[
            pl.BlockSpec((1, D_Q_LAT, D_KV_LAT), lambda l, h: (l, 0, h)),
            pl.BlockSpec((1, D_KV_LAT, D), lambda l, h: (l, h, 0)),
        ],
        out_shape=[
            jax.ShapeDtypeStruct((L, D_Q_LAT, N_HEADS * D_KV_LAT), BF16),
            jax.ShapeDtypeStruct((L, N_HEADS * D_KV_LAT, D), BF16),
        ],
        compiler_params=_cparams(("arbitrary", "arbitrary")),
        name="fold_weights",
    )(uq, ukt, uv, wo)


def _rms(v):
    return v * lax.rsqrt(jnp.mean(v * v, axis=-1, keepdims=True) + EPS)


def _in_kernel(x_ref, sc_ref, sh_ref, g_ref, w_ref, qg_ref, kvg_ref, wq_ref,
               cu_ref, qabs_ref, ckv_ref, qi_ref, kw_ref):
    x = x_ref[0]
    h = _rms(x) * g_ref[...] * (1.0 + sc_ref[0]) + sh_ref[0]
    z = jnp.dot(h.astype(BF16), w_ref[...], preferred_element_type=F32)
    a_val = z[:, 0:C_CONV]
    a_gate = z[:, C_CONV:2 * C_CONV]
    cu_ref[0, :, 0:C_CONV] = a_val * jax.nn.sigmoid(a_gate)
    cu_ref[0, :, C_CONV:C_CONV + C_POOL] = z[:, POOL_OFF:POOL_OFF + C_POOL]
    o = 2 * C_CONV
    cq = _rms(z[:, o:o + D_Q_LAT]) * qg_ref[...]
    qabs = jnp.dot(cq.astype(BF16), wq_ref[...], preferred_element_type=F32)
    qabs_ref[0] = (qabs * (HEAD_DIM ** -0.5)).astype(BF16)
    o += D_Q_LAT
    ckv_ref[0] = (_rms(z[:, o:o + D_KV_LAT]) * kvg_ref[...]).astype(BF16)
    o += D_KV_LAT
    qi_ref[0] = z[:, o:o + IDX_HEADS * IDX_DIM].astype(BF16)
    kw_ref[0] = z[:, KW_OFF:KW_OFF + LANES]


def _in_proj(x, sc, sh, g, w_in_p, qg, kvg, wq, tm):
    B, S, D = x.shape
    tok = lambda n: pl.BlockSpec((1, tm, n), lambda b, i: (b, i, 0))
    per_b = pl.BlockSpec((1, 1, D), lambda b, i: (b, 0, 0))
    full = lambda a: pl.BlockSpec(a.shape, lambda b, i: (0,) * a.ndim)
    nq = IDX_HEADS * IDX_DIM
    return pl.pallas_call(
        _in_kernel,
        grid=(B, S // tm),
        in_specs=[tok(D), per_b, per_b, full(g), full(w_in_p), full(qg), full(kvg), full(wq)],
        out_specs=[tok(C_CONV + C_POOL), tok(N_HEADS * D_KV_LAT), tok(D_KV_LAT), tok(nq), tok(LANES)],
        out_shape=[
            jax.ShapeDtypeStruct((B, S, C_CONV + C_POOL), F32),
            jax.ShapeDtypeStruct((B, S, N_HEADS * D_KV_LAT), BF16),
            jax.ShapeDtypeStruct((B, S, D_KV_LAT), BF16),
            jax.ShapeDtypeStruct((B, S, nq), BF16),
            jax.ShapeDtypeStruct((B, S, LANES), F32),
        ],
        compiler_params=_cparams(("arbitrary", "arbitrary")),
        name="in_proj",
    )(x, sc, sh, g, w_in_p, qg, kvg, wq)


def _cp_kernel(cu_ref, ck_ref, cb_ref, lg_ref, lb_ref, pw_ref, ps_ref, o_ref, pad_ref):
    S = cu_ref.shape[1]
    TT = CONV_TILE
    pad_ref[0:HALO, :] = jnp.zeros((HALO, C_CONV + C_POOL), F32)
    pad_ref[HALO:HALO + S, :] = cu_ref[0]
    lane = lax.broadcasted_iota(jnp.int32, (TT, C_POOL), 1)
    row = lax.broadcasted_iota(jnp.int32, (TT, C_POOL), 0)
    win_len = jnp.where(lane < POOL_GROUP, 2,
                        jnp.where(lane < 2 * POOL_GROUP, 4, jnp.where(lane < 3 * POOL_GROUP, 8, 16)))

    def tile(i, carry):
        t0 = pl.multiple_of(i * TT, TT)
        win = pad_ref[pl.ds(t0, TT + HALO), :]
        wc = win[:, 0:C_CONV]
        wp = win[:, C_CONV:C_CONV + C_POOL]
        acc = jnp.zeros((TT, C_CONV), F32)
        for s in range(8):
            ws = wc if s == 0 else pltpu.roll(wc, TT + HALO - s, 0)
            for j in range(CONV_WIDTH):
                o = HALO - (CONV_WIDTH - 1) + j
                if o % 8 == s:
                    acc = acc + ck_ref[j:j + 1, :] * ws[o - s:o - s + TT, :]
        v = acc + cb_ref[...]
        mu = jnp.mean(v, axis=-1, keepdims=True)
        vc = v - mu
        var = jnp.mean(vc * vc, axis=-1, keepdims=True)
        yn = vc * lax.rsqrt(var + EPS) * lg_ref[...] + lb_ref[...]
        o_ref[0, pl.ds(t0, TT), 0:C_CONV] = (yn * jax.nn.sigmoid(yn)).astype(BF16)
        lo = HALO - 15
        s2 = wp[lo + 1:, :] + wp[lo:-1, :]
        s4 = s2[2:, :] + s2[:-2, :]
        s8 = s4[4:, :] + s4[:-4, :]
        s16 = s8[8:, :] + s8[:-8, :]
        u = wp[HALO:, :]
        sw = jnp.where(lane < POOL_GROUP, s2[14:, :],
                       jnp.where(lane < 2 * POOL_GROUP, s4[12:, :],
                                 jnp.where(lane < 3 * POOL_GROUP, s8[8:, :], s16)))
        n_pos = jnp.minimum(t0 + row + 1, win_len).astype(F32)
        d = (sw / n_pos - u).astype(BF16)
        yp = jnp.dot(d, pw_ref[...], preferred_element_type=F32) * ps_ref[...]
        o_ref[0, pl.ds(t0, TT), C_CONV:C_CONV + C_POOL] = yp.astype(BF16)
        return carry

    lax.fori_loop(0, S // TT, tile, 0)


def _conv_pool(cu, ck, cb, lg, lb, pw_bd, ps):
    B, S, W = cu.shape
    full = lambda a: pl.BlockSpec(a.shape, lambda b: (0,) * a.ndim)
    return pl.pallas_call(
        _cp_kernel,
        grid=(B,),
        in_specs=[pl.BlockSpec((1, S, W), lambda b: (b, 0, 0)),
                  full(ck), full(cb), full(lg), full(lb), full(pw_bd), full(ps)],
        out_specs=pl.BlockSpec((1, S, W), lambda b: (b, 0, 0)),
        out_shape=jax.ShapeDtypeStruct((B, S, W), BF16),
        scratch_shapes=[pltpu.VMEM((S + HALO, W), F32)],
        compiler_params=_cparams(("arbitrary",)),
        name="conv_pool",
    )(cu, ck, cb, lg, lb, pw_bd, ps)


def _attn_kernel(qi_ref, kwq_ref, kwk_ref, qabs_ref, ckv_ref, o_ref, key_ref, aux_ref, bias_ref, *, topk, q0):
    QB = qi_ref.shape[1]
    LK = ckv_ref.shape[1]
    dn_nt = (((1,), (1,)), ((), ()))

    ki = kwk_ref[0, :, 0:IDX_DIM].astype(BF16)
    wi = kwq_ref[0, :, IDX_DIM:IDX_DIM + IDX_HEADS]
    qi = qi_ref[0]
    kidx = lax.broadcasted_iota(jnp.int32, (QB, LK), 1)
    qpos = q0 + lax.broadcasted_iota(jnp.int32, (QB, 1), 0)
    allowed = kidx < (qpos // CHUNK + 1) * CHUNK

    if LK <= topk:
        sel = allowed
    else:
        score = jnp.zeros((QB, LK), F32)
        for h in range(IDX_HEADS):
            rel = lax.dot_general(qi[:, h * IDX_DIM:(h + 1) * IDX_DIM], ki, dn_nt,
                                  preferred_element_type=F32)
            score = score + jnp.maximum(rel, 0.0) * wi[:, h:h + 1]
        score = score * ((IDX_HEADS * IDX_DIM) ** -0.5)
        score = jnp.where(score == 0.0, 0.0, score)
        score = jnp.where(allowed, score, -jnp.inf)
        bits = lax.bitcast_convert_type(score, jnp.int32)
        key_ref[...] = bits ^ ((bits >> 31) & jnp.int32(0x7FFFFFFF))

        kf = jnp.float32(topk)

        def count_ge(cand):
            return jnp.sum(jnp.where(key_ref[...] >= cand, 1.0, 0.0), axis=-1, keepdims=True)

        H = QB // 2
        neg = jnp.full((QB, 1), jnp.iinfo(jnp.int32).min, jnp.int32)
        prefix = jnp.where(count_ge(jnp.zeros((QB, 1), jnp.int32)) >= kf, 0, neg)

        def partial_counts(r0, cand):
            acc = jnp.zeros((H, LANES), F32)
            for c in range(LK // LANES):
                k = key_ref[r0:r0 + H, c * LANES:(c + 1) * LANES]
                acc = acc + jnp.where(k >= cand, 1.0, 0.0)
            return acc

        def settle(acc, cand, pre):
            return jnp.where(jnp.sum(acc, axis=-1, keepdims=True) >= kf, cand, pre)

        def bit(i):
            return jnp.where(i <= 30, jnp.int32(1) << jnp.maximum(30 - i, 0), 0)

        def bit_step(i, carry):
            pre_a, pre_b, acc_b = carry
            cand_a = pre_a | bit(i)
            acc_a = partial_counts(0, cand_a)
            pre_b = settle(acc_b, pre_b | bit(i), pre_b)
            acc_b = partial_counts(H, pre_b | bit(i + 1))
            pre_a = settle(acc_a, cand_a, pre_a)
            return pre_a, pre_b, acc_b

        pre_a, pre_b = prefix[0:H], prefix[H:QB]
        pre_a, pre_b, _ = lax.fori_loop(0, 31, bit_step,
                                        (pre_a, pre_b, partial_counts(H, pre_b | bit(0))))
        thr = jnp.concatenate([pre_a, pre_b], axis=0)
        has_excess = jnp.max(count_ge(thr)) > kf

        @pl.when(jnp.logical_not(has_excess))
        def _():
            bias_ref[...] = jnp.where(key_ref[...] >= thr, 0.0, -jnp.inf)

        @pl.when(has_excess)
        def _():
            key = key_ref[...]
            n_gt = jnp.sum(jnp.where(key > thr, 1.0, 0.0), axis=-1, keepdims=True)
            need = kf - n_gt
            aux_ref[...] = jnp.where(key == thr, kidx, jnp.int32(LK))
            nbits = (LK - 1).bit_length()

            def idx_step(i, p):
                cand = p | (jnp.int32(1) << (nbits - 1 - i))
                cnt = jnp.sum(jnp.where(aux_ref[...] < cand, 1.0, 0.0), axis=-1, keepdims=True)
                return jnp.where(cnt < need, cand, p)

            cut = lax.fori_loop(0, nbits, idx_step, jnp.zeros((QB, 1), jnp.int32))
            sel = ((key_ref[...] > thr) | (aux_ref[...] <= cut)) & allowed
            bias_ref[...] = jnp.where(sel, 0.0, -jnp.inf)

        sel = None
    bias = jnp.where(sel, 0.0, -jnp.inf) if sel is not None else bias_ref[...]

    ckv = ckv_ref[0]
    for h in range(N_HEADS):
        qa = qabs_ref[0, :, h * D_KV_LAT:(h + 1) * D_KV_LAT]
        logit = lax.dot_general(qa, ckv, dn_nt, preferred_element_type=F32) + bias
        m = jnp.max(logit, axis=-1, keepdims=True)
        p = jnp.exp(logit - m)
        l = jnp.sum(p, axis=-1, keepdims=True)
        ctx = jnp.dot(p.astype(BF16), ckv, preferred_element_type=F32)
        o_ref[0, :, h * D_KV_LAT:(h + 1) * D_KV_LAT] = (ctx / l).astype(BF16)


def _attention(qi, kw, qabs, ckv, qb):
    B, S, _ = ckv.shape
    topk = min(TOPK_MAX, S // 4)
    buf = qabs
    for j in range(S // qb):
        lk = (j + 1) * qb
        blk = lambda n, j=j: pl.BlockSpec((1, qb, n), lambda b: (b, j, 0))
        keys = lambda n, lk=lk: pl.BlockSpec((1, lk, n), lambda b: (b, 0, 0))
        buf = pl.pallas_call(
            functools.partial(_attn_kernel, topk=topk, q0=j * qb),
            grid=(B,),
            in_specs=[blk(IDX_HEADS * IDX_DIM), blk(LANES), keys(LANES), blk(N_HEADS * D_KV_LAT),
                      keys(D_KV_LAT)],
            out_specs=blk(N_HEADS * D_KV_LAT),
            out_shape=jax.ShapeDtypeStruct(buf.shape, BF16),
            scratch_shapes=[pltpu.VMEM((qb, lk), jnp.int32), pltpu.VMEM((qb, lk), jnp.int32),
                            pltpu.VMEM((qb, lk), F32)],
            input_output_aliases={3: 0},
            compiler_params=_cparams(("arbitrary",)),
            name=f"dsa_attention_q{j}",
        )(qi, kw, kw, buf, ckv)
    return buf


def _out_kernel(ycp_ref, ctx_ref, x_ref, g1_ref, sc_ref, sh_ref, ng_ref, wcp_ref, wvo_ref,
                rw_ref, rb_ref, tri_ref, x1_ref, h2_ref, route_ref, cnt_ref, carry_ref):
    first = (pl.program_id(0) == 0) & (pl.program_id(1) == 0)

    @pl.when(first)
    def _():
        carry_ref[...] = jnp.zeros_like(carry_ref)

    mix = jnp.dot(ycp_ref[0], wcp_ref[...], preferred_element_type=F32)
    mix = mix + jnp.dot(ctx_ref[0], wvo_ref[...], preferred_element_type=F32)
    x1 = x_ref[0] + g1_ref[0] * mix
    x1_ref[0] = x1
    h2 = _rms(x1) * ng_ref[...] * (1.0 + sc_ref[0]) + sh_ref[0]
    h2_ref[0] = h2
    rw = rw_ref[...]
    rw_hi = rw.astype(BF16)
    rw_lo = (rw - rw_hi.astype(F32)).astype(BF16)
    h_hi = h2.astype(BF16)
    h_lo = (h2 - h_hi.astype(F32)).astype(BF16)
    logits = (jnp.dot(h_hi, rw_hi, preferred_element_type=F32)
              + (jnp.dot(h_lo, rw_hi, preferred_element_type=F32)
                 + jnp.dot(h_hi, rw_lo, preferred_element_type=F32))) + rb_ref[...]

    lt = logits.T
    tm = lt.shape[1]
    rowi = lax.broadcasted_iota(jnp.int32, (LANES, tm), 0)
    rowf = rowi.astype(F32)
    ninf = -jnp.inf
    big = jnp.float32(LANES)
    glog = jnp.where(rowi < N_GROUPS, lt, ninf)
    gmax = jnp.max(glog, axis=0, keepdims=True)
    gidx = jnp.min(jnp.where(glog == gmax, rowf, big), axis=0, keepdims=True)
    g_p = 1.0 / jnp.sum(jnp.exp(glog - gmax), axis=0, keepdims=True)
    lo = N_GROUPS + EXPERTS_PER_GROUP * gidx
    elog = jnp.where((rowf >= lo) & (rowf < lo + EXPERTS_PER_GROUP), lt, ninf)
    m1 = jnp.max(elog, axis=0, keepdims=True)
    i1 = jnp.min(jnp.where(elog == m1, rowf, big), axis=0, keepdims=True)
    elog2 = jnp.where(rowf == i1, ninf, elog)
    m2 = jnp.max(elog2, axis=0, keepdims=True)
    i2 = jnp.min(jnp.where(elog2 == m2, rowf, big), axis=0, keepdims=True)
    r = jnp.exp(m2 - m1)
    gate1 = g_p / (1.0 + r)
    gate2 = g_p * r / (1.0 + r)
    e1 = i1 - N_GROUPS
    e2 = i2 - N_GROUPS
    oh1 = rowf == e1
    oh2 = rowf == e2
    oh1f = jnp.where(oh1, 1.0, 0.0)
    oh2f = jnp.where(oh2, 1.0, 0.0)
    pre1 = jnp.dot(oh1f.astype(BF16), tri_ref[...], preferred_element_type=F32)
    pre2 = jnp.dot(oh2f.astype(BF16), tri_ref[...], preferred_element_type=F32)
    carry = carry_ref[:, 0:1]
    cnt1 = jnp.sum(oh1f, axis=1, keepdims=True)
    cnt2 = jnp.sum(oh2f, axis=1, keepdims=True)
    rank1 = jnp.sum(jnp.where(oh1, carry + pre1, 0.0), axis=0, keepdims=True)
    rank2 = jnp.sum(jnp.where(oh2, carry + cnt1 + pre2, 0.0), axis=0, keepdims=True)
    total = carry + cnt1 + cnt2
    carry_ref[...] = jnp.broadcast_to(total, carry_ref.shape)
    cnt_ref[...] = jnp.broadcast_to(total, cnt_ref.shape)
    route_t = jnp.where(rowi == 0, e1, jnp.where(rowi == 1, e2, jnp.where(rowi == 2, gate1, jnp.where(
        rowi == 3, gate2, jnp.where(rowi == 4, rank1, jnp.where(rowi == 5, rank2, 0.0))))))
    route_ref[0] = route_t.T


def _out_proj(ycp, ctx, x, g1, sc2, sh2, ng, wcp, wvo, rw, rb, tm):
    B, S, D = x.shape
    tok = lambda n: pl.BlockSpec((1, tm, n), lambda b, i: (b, i, 0))
    per_b = pl.BlockSpec((1, 1, D), lambda b, i: (b, 0, 0))
    full = lambda a: pl.BlockSpec(a.shape, lambda b, i: (0,) * a.ndim)
    tri = jnp.triu(jnp.ones((tm, tm), BF16), 1)
    return pl.pallas_call(
        _out_kernel,
        grid=(B, S // tm),
        in_specs=[tok(C_CONV + C_POOL), tok(N_HEADS * D_KV_LAT), tok(D), per_b, per_b, per_b,
                  full(ng), full(wcp), full(wvo), full(rw), full(rb), full(tri)],
        out_specs=[tok(D), tok(D), tok(LANES), pl.BlockSpec((LANES, LANES), lambda b, i: (0, 0))],
        out_shape=[
            jax.ShapeDtypeStruct((B, S, D), F32),
            jax.ShapeDtypeStruct((B, S, D), F32),
            jax.ShapeDtypeStruct((B, S, LANES), F32),
            jax.ShapeDtypeStruct((LANES, LANES), F32),
        ],
        scratch_shapes=[pltpu.VMEM((LANES, LANES), F32)],
        compiler_params=_cparams(("arbitrary", "arbitrary")),
        name="out_proj_router",
    )(ycp, ctx, x, g1, sc2, sh2, ng, wcp, wvo, rw, rb, tri)


def _row_copy(src, i, dst, k, sem):
    return pltpu.make_async_copy(src.at[pl.ds(i, 1), :], dst.at[pl.ds(k, 1), :], sem)


def _dispatch_kernel(dst_ref, h_ref, xs_in_ref, xs_ref, sem):
    del xs_in_ref
    tm = h_ref.shape[0]

    def issue(r, c):
        _row_copy(h_ref, r, xs_ref, dst_ref[2 * r], sem).start()
        _row_copy(h_ref, r, xs_ref, dst_ref[2 * r + 1], sem).start()
        return c

    lax.fori_loop(0, tm, issue, 0, unroll=ROW_DMA_UNROLL)

    def drain(r, c):
        _row_copy(h_ref, r, xs_ref, dst_ref[2 * r], sem).wait()
        _row_copy(h_ref, r, xs_ref, dst_ref[2 * r + 1], sem).wait()
        return c

    lax.fori_loop(0, tm, drain, 0, unroll=ROW_DMA_UNROLL)


def _dispatch(dst_flat, h2, xs0, tm):
    T, D = h2.shape
    n_rows = xs0.shape[0]
    return pl.pallas_call(
        _dispatch_kernel,
        grid=(T // tm,),
        in_specs=[pl.BlockSpec((2 * tm,), lambda i: (i,), memory_space=pltpu.SMEM),
                  pl.BlockSpec((tm, D), lambda i: (i, 0)),
                  pl.BlockSpec(memory_space=pl.ANY)],
        out_specs=pl.BlockSpec(memory_space=pl.ANY),
        out_shape=jax.ShapeDtypeStruct((n_rows, D), F32),
        scratch_shapes=[pltpu.SemaphoreType.DMA(())],
        input_output_aliases={2: 0},
        compiler_params=_cparams(("arbitrary",), row_dma=True),
        name="moe_dispatch",
    )(dst_flat, h2, xs0)


def _ffn_kernel(be_ref, nu_ref, xs_ref, w1_ref, w3_ref, w2_ref, y_ref, w13_scr, w2_scr):
    i = pl.program_id(0)
    new_expert = (i == 0) | (be_ref[i] != be_ref[jnp.maximum(i - 1, 0)])

    @pl.when(new_expert)
    def _():
        w13_scr[:, 0:D_EXPERT] = w1_ref[0, 0].astype(BF16)
        w13_scr[:, D_EXPERT:2 * D_EXPERT] = w3_ref[0, 0].astype(BF16)
        w2_scr[...] = w2_ref[0, 0].astype(BF16)

    @pl.when(i < nu_ref[0])
    def _():
        xb = xs_ref[...].astype(BF16)
        h13 = jnp.dot(xb, w13_scr[...], preferred_element_type=F32)
        h1 = h13[:, 0:D_EXPERT]
        h3 = h13[:, D_EXPERT:2 * D_EXPERT]
        act = (h1 * jax.nn.sigmoid(h1)) * h3
        y_ref[...] = jnp.dot(act.astype(BF16), w2_scr[...], preferred_element_type=F32)

    @pl.when(i >= nu_ref[0])
    def _():
        y_ref[...] = jnp.zeros_like(y_ref)


def _expert_ffn(blk_exp, n_used, xs, w1, w3, w2, layer):
    P, D = xs.shape
    nblk = P // FFN_BLOCK
    grid_spec = pltpu.PrefetchScalarGridSpec(
        num_scalar_prefetch=2,
        grid=(nblk,),
        in_specs=[
            pl.BlockSpec((FFN_BLOCK, D), lambda i, be, nu: (jnp.maximum(jnp.minimum(i, nu[0] - 1), 0), 0)),
            pl.BlockSpec((1, 1, D, D_EXPERT), lambda i, be, nu: (layer, be[i], 0, 0)),
            pl.BlockSpec((1, 1, D, D_EXPERT), lambda i, be, nu: (layer, be[i], 0, 0)),
            pl.BlockSpec((1, 1, D_EXPERT, D), lambda i, be, nu: (layer, be[i], 0, 0)),
        ],
        out_specs=pl.BlockSpec((FFN_BLOCK, D), lambda i, be, nu: (i, 0)),
        scratch_shapes=[pltpu.VMEM((D, 2 * D_EXPERT), BF16), pltpu.VMEM((D_EXPERT, D), BF16)],
    )
    return pl.pallas_call(
        _ffn_kernel,
        grid_spec=grid_spec,
        out_shape=jax.ShapeDtypeStruct((P, D), F32),
        compiler_params=_cparams(("arbitrary",)),
        name="expert_ffn",
    )(blk_exp, n_used, xs, w1, w3, w2)


def _combine_kernel(dst_ref, x1_ref, route_ref, g2_ref, fg_ref, y_ref, o_ref, ya_ref, yb_ref, sem,
                    *, final_norm):
    tm = x1_ref.shape[1]

    def issue(r, c):
        _row_copy(y_ref, dst_ref[2 * r], ya_ref, r, sem.at[0]).start()
        _row_copy(y_ref, dst_ref[2 * r + 1], yb_ref, r, sem.at[1]).start()
        return c

    lax.fori_loop(0, tm, issue, 0, unroll=ROW_DMA_UNROLL)

    def drain(r, c):
        _row_copy(y_ref, dst_ref[2 * r], ya_ref, r, sem.at[0]).wait()
        _row_copy(y_ref, dst_ref[2 * r + 1], yb_ref, r, sem.at[1]).wait()
        return c

    lax.fori_loop(0, tm, drain, 0, unroll=ROW_DMA_UNROLL)

    route = route_ref[0]
    moe = ya_ref[...] * route[:, 2:3] + yb_ref[...] * route[:, 3:4]
    out = x1_ref[0] + g2_ref[0] * moe
    if final_norm:
        out = _rms(out) * fg_ref[...]
    o_ref[0] = out


def _combine(dst_flat, x1, route, g2, fg, y, tm, final_norm):
    B, S, D = x1.shape
    nt = S // tm
    tok = lambda n: pl.BlockSpec((1, tm, n), lambda b, i: (b, i, 0))
    return pl.pallas_call(
        functools.partial(_combine_kernel, final_norm=final_norm),
        grid=(B, nt),
        in_specs=[pl.BlockSpec((2 * tm,), lambda b, i: (b * nt + i,), memory_space=pltpu.SMEM),
                  tok(D), tok(LANES),
                  pl.BlockSpec((1, 1, D), lambda b, i: (b, 0, 0)),
                  pl.BlockSpec((1, D), lambda b, i: (0, 0)),
                  pl.BlockSpec(memory_space=pl.ANY)],
        out_specs=tok(D),
        out_shape=jax.ShapeDtypeStruct((B, S, D), F32),
        scratch_shapes=[pltpu.VMEM((tm, D), F32), pltpu.VMEM((tm, D), F32), pltpu.SemaphoreType.DMA((2,))],
        compiler_params=_cparams(("arbitrary", "arbitrary"), row_dma=True),
        name="moe_combine",
    )(dst_flat, x1, route, g2, fg, y)


def _pick_tile(n, pref):
    t = min(n, pref)
    while n % t:
        t //= 2
    return t


def kernel(x, c, mod_w, mod_b, norm1_g, w_in, conv_k, conv_b, conv_ln_g, conv_ln_b, q_norm_g, kv_norm_g, w_uq, w_uk, w_uv, pool_w, pool_scale, w_out, norm2_g, router_g_w, router_g_b, router_e_w, router_e_b, exp_w1, exp_w3, exp_w2, final_g):
    B, S, D = x.shape
    L = mod_w.shape[0]
    T = B * S
    tm = _pick_tile(S, 512)
    qb = _pick_tile(S, 256)
    tmc = _pick_tile(S, 512)
    n_rows = 2 * T + N_EXPERTS * FFN_BLOCK
    nblk = n_rows // FFN_BLOCK

    mod = _modulation(c, mod_w, mod_b)
    wq_all, wvo_all = _fold_weights(w_uq, w_uk, w_uv, w_out)
    row = lambda a: a.reshape(1, -1)

    slot_buf = jnp.zeros((n_rows, D), F32)
    for l in range(L):
        sh1, sc1, g1, sh2, sc2, g2 = [mod[l, :, k * D:(k + 1) * D].reshape(B, 1, D) for k in range(6)]
        w_in_p = jnp.concatenate(
            [w_in[l, :, :N_IN - C_POOL], jnp.zeros((D, N_IN_PAD - N_IN), F32), w_in[l, :, N_IN - C_POOL:]],
            axis=1).astype(BF16)
        cu, qabs, ckv, qi, kw = _in_proj(x, sc1, sh1, row(norm1_g[l]), w_in_p, row(q_norm_g[l]),
                                         row(kv_norm_g[l]), wq_all[l], tm)
        pw_bd = jax.scipy.linalg.block_diag(*[pool_w[l, g] for g in range(pool_w.shape[1])]).astype(BF16)
        ycp = _conv_pool(cu, conv_k[l], row(conv_b[l]), row(conv_ln_g[l]), row(conv_ln_b[l]),
                         pw_bd, row(pool_scale[l]))
        ctx = _attention(qi, kw, qabs, ckv, qb)
        wcp = jnp.concatenate([w_out[l, :C_CONV], w_out[l, D - C_POOL:]], axis=0).astype(BF16)
        n_r = N_GROUPS + N_EXPERTS
        rw = jnp.concatenate([router_g_w[l], router_e_w[l].reshape(D, N_EXPERTS),
                              jnp.zeros((D, LANES - n_r), F32)], axis=1)
        rb = jnp.concatenate([router_g_b[l], router_e_b[l].reshape(N_EXPERTS),
                              jnp.zeros((LANES - n_r,), F32)]).reshape(1, LANES)
        x1, h2, route, cnt = _out_proj(ycp, ctx, x, g1, sc2, sh2, row(norm2_g[l]), wcp, wvo_all[l],
                                       rw, rb, tm)
        counts = cnt[:N_EXPERTS, 0].astype(jnp.int32)
        pcounts = (counts + FFN_BLOCK - 1) // FFN_BLOCK * FFN_BLOCK
        pends = jnp.cumsum(pcounts)
        pstarts = pends - pcounts
        route2 = route.reshape(T, LANES)
        e_idx = route2[:, 0:2].astype(jnp.int32)
        dst = (pstarts[e_idx] + route2[:, 4:6].astype(jnp.int32)).reshape(2 * T)
        n_used = (pends[-1] // FFN_BLOCK).astype(jnp.int32).reshape(1)
        blk_start = jnp.arange(nblk, dtype=jnp.int32) * FFN_BLOCK
        blk_exp = jnp.minimum(jnp.sum((pends[None, :] <= blk_start[:, None]).astype(jnp.int32), axis=1),
                              N_EXPERTS - 1)
        xs = _dispatch(dst, h2.reshape(T, D), slot_buf, tmc)
        y = _expert_ffn(blk_exp, n_used, xs, exp_w1, exp_w3, exp_w2, l)
        slot_buf = y
        x = _combine(dst, x1, route, g2, row(final_g), y, tmc, final_norm=(l == L - 1))
    return x
```

```python
import functools

import jax
import jax.numpy as jnp
from jax import lax
from jax.experimental import pallas as pl
from jax.experimental.pallas import tpu as pltpu

F32 = jnp.float32
BF16 = jnp.bfloat16
HIGHEST = lax.Precision.HIGHEST

EPS = 1e-6
CHUNK = 64
CONV_WIDTH = 31
C_CONV = 256
C_POOL = 256
POOL_GROUP = 64
N_HEADS = 8
HEAD_DIM = 64
D_Q_LAT = 256
D_KV_LAT = 128
IDX_HEADS = 8
IDX_DIM = 64
TOPK_MAX = 256
N_GROUPS = 4
EXPERTS_PER_GROUP = 8
N_EXPERTS = 32
D_EXPERT = 512
N_IN = 1736
N_IN_PAD = 1792
KW_OFF = 1408
POOL_OFF = 1536
LANES = 128
HALO = 32
CONV_TILE = 64
FFN_BLOCK = 256
ROW_DMA_UNROLL = 8
ATTN_BATCH = 2
LOG2E = 1.4426950408889634
VMEM_LIMIT = 56 * 1024 * 1024


def _cparams(sem, row_dma=False):
    return pltpu.CompilerParams(dimension_semantics=sem, vmem_limit_bytes=VMEM_LIMIT,
                                disable_bounds_checks=row_dma)


def _mod_kernel(c_ref, w_ref, b_ref, o_ref):
    c = c_ref[...]
    cond = c * jax.nn.sigmoid(c)
    o_ref[0] = jnp.dot(cond, w_ref[0], precision=HIGHEST, preferred_element_type=F32) + b_ref[0]


def _modulation(c, mod_w, mod_b):
    L, D, D6 = mod_w.shape
    B = c.shape[0]
    nj = D6 // D
    return pl.pallas_call(
        _mod_kernel,
        grid=(L, nj),
        in_specs=[
            pl.BlockSpec((B, D), lambda l, j: (0, 0)),
            pl.BlockSpec((1, D, D), lambda l, j: (l, 0, j)),
            pl.BlockSpec((1, 1, D), lambda l, j: (l, 0, j)),
        ],
        out_specs=pl.BlockSpec((1, B, D), lambda l, j: (l, 0, j)),
        out_shape=jax.ShapeDtypeStruct((L, B, D6), F32),
        compiler_params=_cparams(("arbitrary", "arbitrary")),
        name="modulation",
    )(c, mod_w, mod_b.reshape(L, 1, D6))


def _fold_kernel(uq_ref, ukt_ref, uv_ref, wo_ref, wq_ref, wvo_ref):
    wq_ref[0] = jnp.dot(uq_ref[0, 0], ukt_ref[0, 0], precision=HIGHEST,
                        preferred_element_type=F32).astype(BF16)
    wvo_ref[0] = jnp.dot(uv_ref[0, 0], wo_ref[0, 0], precision=HIGHEST,
                         preferred_element_type=F32).astype(BF16)


def _fold_weights(w_uq, w_uk, w_uv, w_out):
    L = w_uq.shape[0]
    D = w_out.shape[-1]
    uq = jnp.transpose(w_uq, (0, 2, 1, 3))
    ukt = jnp.transpose(w_uk, (0, 2, 3, 1))
    uv = jnp.transpose(w_uv, (0, 2, 1, 3))
    wo = w_out[:, C_CONV:C_CONV + N_HEADS * HEAD_DIM, :].reshape(L, N_HEADS, HEAD_DIM, D)
    return pl.pallas_call(
        _fold_kernel,
        grid=(L, N_HEADS),
        in_specs=[
            pl.BlockSpec((1, 1, D_Q_LAT, HEAD_DIM), lambda l, h: (l, h, 0, 0)),
            pl.BlockSpec((1, 1, HEAD_DIM, D_KV_LAT), lambda l, h: (l, h, 0, 0)),
            pl.BlockSpec((1, 1, D_KV_LAT, HEAD_DIM), lambda l, h: (l, h, 0, 0)),
            pl.BlockSpec((1, 1, HEAD_DIM, D), lambda l, h: (l, h, 0, 0)),
        ],
        out_specs=[
            pl.BlockSpec((1, D_Q_LAT, D_KV_LAT), lambda l, h: (l, 0, h)),
            pl.BlockSpec((1, D_KV_LAT, D), lambda l, h: (l, h, 0)),
        ],
        out_shape=[
            jax.ShapeDtypeStruct((L, D_Q_LAT, N_HEADS * D_KV_LAT), BF16),
            jax.ShapeDtypeStruct((L, N_HEADS * D_KV_LAT, D), BF16),
        ],
        compiler_params=_cparams(("arbitrary", "arbitrary")),
        name="fold_weights",
    )(uq, ukt, uv, wo)


def _rms(v):
    return v * lax.rsqrt(jnp.mean(v * v, axis=-1, keepdims=True) + EPS)


def _in_kernel(x_ref, sc_ref, sh_ref, g_ref, w_ref, qg_ref, kvg_ref, wq_ref,
               cu_ref, qabs_ref, ckv_ref, qi_ref, kw_ref):
    x = x_ref[0]
    h = _rms(x) * g_ref[...] * (1.0 + sc_ref[0]) + sh_ref[0]
    z = jnp.dot(h.astype(BF16), w_ref[...], preferred_element_type=F32)
    a_val = z[:, 0:C_CONV]
    a_gate = z[:, C_CONV:2 * C_CONV]
    cu_ref[0, :, 0:C_CONV] = a_val * jax.nn.sigmoid(a_gate)
    cu_ref[0, :, C_CONV:C_CONV + C_POOL] = z[:, POOL_OFF:POOL_OFF + C_POOL]
    o = 2 * C_CONV
    cq = _rms(z[:, o:o + D_Q_LAT]) * qg_ref[...]
    qabs = jnp.dot(cq.astype(BF16), wq_ref[...], preferred_element_type=F32)
    qabs_ref[0] = (qabs * (HEAD_DIM ** -0.5 * LOG2E)).astype(BF16)
    o += D_Q_LAT
    ckv_ref[0] = (_rms(z[:, o:o + D_KV_LAT]) * kvg_ref[...]).astype(BF16)
    o += D_KV_LAT
    qi_ref[0] = z[:, o:o + IDX_HEADS * IDX_DIM].astype(BF16)
    kw_ref[0] = z[:, KW_OFF:KW_OFF + LANES]


def _in_proj(x, sc, sh, g, w_in_p, qg, kvg, wq, tm):
    B, S, D = x.shape
    tok = lambda n: pl.BlockSpec((1, tm, n), lambda b, i: (b, i, 0))
    per_b = pl.BlockSpec((1, 1, D), lambda b, i: (b, 0, 0))
    full = lambda a: pl.BlockSpec(a.shape, lambda b, i: (0,) * a.ndim)
    nq = IDX_HEADS * IDX_DIM
    return pl.pallas_call(
        _in_kernel,
        grid=(B, S // tm),
        in_specs=[tok(D), per_b, per_b, full(g), full(w_in_p), full(qg), full(kvg), full(wq)],
        out_specs=[tok(C_CONV + C_POOL), tok(N_HEADS * D_KV_LAT), tok(D_KV_LAT), tok(nq), tok(LANES)],
        out_shape=[
            jax.ShapeDtypeStruct((B, S, C_CONV + C_POOL), F32),
            jax.ShapeDtypeStruct((B, S, N_HEADS * D_KV_LAT), BF16),
            jax.ShapeDtypeStruct((B, S, D_KV_LAT), BF16),
            jax.ShapeDtypeStruct((B, S, nq), BF16),
            jax.ShapeDtypeStruct((B, S, LANES), F32),
        ],
        compiler_params=_cparams(("arbitrary", "arbitrary")),
        name="in_proj",
    )(x, sc, sh, g, w_in_p, qg, kvg, wq)


def _cp_kernel(cu_ref, ck_ref, cb_ref, lg_ref, lb_ref, pw_ref, ps_ref, o_ref, pad_ref):
    S = cu_ref.shape[1]
    TT = CONV_TILE
    pad_ref[0:HALO, :] = jnp.zeros((HALO, C_CONV + C_POOL), F32)
    pad_ref[HALO:HALO + S, :] = cu_ref[0]
    lane = lax.broadcasted_iota(jnp.int32, (TT, C_POOL), 1)
    row = lax.broadcasted_iota(jnp.int32, (TT, C_POOL), 0)
    win_len = jnp.where(lane < POOL_GROUP, 2,
                        jnp.where(lane < 2 * POOL_GROUP, 4, jnp.where(lane < 3 * POOL_GROUP, 8, 16)))

    def tile(i, carry):
        t0 = pl.multiple_of(i * TT, TT)
        win = pad_ref[pl.ds(t0, TT + HALO), :]
        wc = win[:, 0:C_CONV]
        wp = win[:, C_CONV:C_CONV + C_POOL]
        acc = jnp.zeros((TT, C_CONV), F32)
        for s in range(8):
            ws = wc if s == 0 else pltpu.roll(wc, TT + HALO - s, 0)
            for j in range(CONV_WIDTH):
                o = HALO - (CONV_WIDTH - 1) + j
                if o % 8 == s:
                    acc = acc + ck_ref[j:j + 1, :] * ws[o - s:o - s + TT, :]
        v = acc + cb_ref[...]
        mu = jnp.mean(v, axis=-1, keepdims=True)
        vc = v - mu
        var = jnp.mean(vc * vc, axis=-1, keepdims=True)
        yn = vc * lax.rsqrt(var + EPS) * lg_ref[...] + lb_ref[...]
        o_ref[0, pl.ds(t0, TT), 0:C_CONV] = (yn * jax.nn.sigmoid(yn)).astype(BF16)
        lo = HALO - 15
        s2 = wp[lo + 1:, :] + wp[lo:-1, :]
        s4 = s2[2:, :] + s2[:-2, :]
        s8 = s4[4:, :] + s4[:-4, :]
        s16 = s8[8:, :] + s8[:-8, :]
        u = wp[HALO:, :]
        sw = jnp.where(lane < POOL_GROUP, s2[14:, :],
                       jnp.where(lane < 2 * POOL_GROUP, s4[12:, :],
                                 jnp.where(lane < 3 * POOL_GROUP, s8[8:, :], s16)))
        n_pos = jnp.minimum(t0 + row + 1, win_len).astype(F32)
        d = (sw / n_pos - u).astype(BF16)
        yp = jnp.dot(d, pw_ref[...], preferred_element_type=F32) * ps_ref[...]
        o_ref[0, pl.ds(t0, TT), C_CONV:C_CONV + C_POOL] = yp.astype(BF16)
        return carry

    lax.fori_loop(0, S // TT, tile, 0)


def _conv_pool(cu, ck, cb, lg, lb, pw_bd, ps):
    B, S, W = cu.shape
    full = lambda a: pl.BlockSpec(a.shape, lambda b: (0,) * a.ndim)
    return pl.pallas_call(
        _cp_kernel,
        grid=(B,),
        in_specs=[pl.BlockSpec((1, S, W), lambda b: (b, 0, 0)),
                  full(ck), full(cb), full(lg), full(lb), full(pw_bd), full(ps)],
        out_specs=pl.BlockSpec((1, S, W), lambda b: (b, 0, 0)),
        out_shape=jax.ShapeDtypeStruct((B, S, W), BF16),
        scratch_shapes=[pltpu.VMEM((S + HALO, W), F32)],
        compiler_params=_cparams(("arbitrary",)),
        name="conv_pool",
    )(cu, ck, cb, lg, lb, pw_bd, ps)


def _attn_kernel(qi_ref, kwq_ref, kwk_ref, qabs_ref, ckv_ref, o_ref, key_ref, aux_ref, bias_ref, *, topk, q0):
    G, QB = qi_ref.shape[0], qi_ref.shape[1]
    LK = ckv_ref.shape[1]
    R = G * QB
    dn_nt = (((1,), (1,)), ((), ()))

    def visible(rows):
        kidx = lax.broadcasted_iota(jnp.int32, (rows, LK), 1)
        qpos = q0 + lax.broadcasted_iota(jnp.int32, (rows, 1), 0) % QB
        return kidx, kidx < (qpos // CHUNK + 1) * CHUNK

    if LK > topk:
        _, allowed = visible(QB)
        for g in range(G):
            ki = kwk_ref[g, :, 0:IDX_DIM].astype(BF16)
            wi = kwq_ref[g, :, IDX_DIM:IDX_DIM + IDX_HEADS]
            qi = qi_ref[g]
            score = jnp.zeros((QB, LK), F32)
            for h in range(IDX_HEADS):
                rel = lax.dot_general(qi[:, h * IDX_DIM:(h + 1) * IDX_DIM], ki, dn_nt,
                                      preferred_element_type=F32)
                score = score + jnp.maximum(rel, 0.0) * wi[:, h:h + 1]
            score = score * ((IDX_HEADS * IDX_DIM) ** -0.5)
            score = jnp.where(score == 0.0, 0.0, score)
            score = jnp.where(allowed, score, -jnp.inf)
            bits = lax.bitcast_convert_type(score, jnp.int32)
            key_ref[g * QB:(g + 1) * QB, :] = bits ^ ((bits >> 31) & jnp.int32(0x7FFFFFFF))

        kf = jnp.float32(topk)

        def count_ge(cand):
            return jnp.sum(jnp.where(key_ref[...] >= cand, 1.0, 0.0), axis=-1, keepdims=True)

        H = R // 2
        neg = jnp.full((R, 1), jnp.iinfo(jnp.int32).min, jnp.int32)
        prefix = jnp.where(count_ge(jnp.zeros((R, 1), jnp.int32)) >= kf, 0, neg)

        def partial_counts(r0, cand):
            acc = jnp.zeros((H, LANES), F32)
            for c in range(LK // LANES):
                k = key_ref[r0:r0 + H, c * LANES:(c + 1) * LANES]
                acc = acc + jnp.where(k >= cand, 1.0, 0.0)
            return acc

        def settle(acc, cand, pre):
            return jnp.where(jnp.sum(acc, axis=-1, keepdims=True) >= kf, cand, pre)

        def bit(i):
            return jnp.where(i <= 30, jnp.int32(1) << jnp.maximum(30 - i, 0), 0)

        def bit_step(i, carry):
            pre_a, pre_b, acc_b = carry
            cand_a = pre_a | bit(i)
            acc_a = partial_counts(0, cand_a)
            pre_b = settle(acc_b, pre_b | bit(i), pre_b)
            acc_b = partial_counts(H, pre_b | bit(i + 1))
            pre_a = settle(acc_a, cand_a, pre_a)
            return pre_a, pre_b, acc_b

        pre_a, pre_b = prefix[0:H], prefix[H:R]
        pre_a, pre_b, _ = lax.fori_loop(0, 31, bit_step,
                                        (pre_a, pre_b, partial_counts(H, pre_b | bit(0))))
        thr = jnp.concatenate([pre_a, pre_b], axis=0)
        has_excess = jnp.max(count_ge(thr)) > kf

        @pl.when(jnp.logical_not(has_excess))
        def _():
            bias_ref[...] = jnp.where(key_ref[...] >= thr, 0.0, -jnp.inf)

        @pl.when(has_excess)
        def _():
            kidx, allowed_r = visible(R)
            key = key_ref[...]
            n_gt = jnp.sum(jnp.where(key > thr, 1.0, 0.0), axis=-1, keepdims=True)
            need = kf - n_gt
            aux_ref[...] = jnp.where(key == thr, kidx, jnp.int32(LK))
            nbits = (LK - 1).bit_length()

            def idx_step(i, p):
                cand = p | (jnp.int32(1) << (nbits - 1 - i))
                cnt = jnp.sum(jnp.where(aux_ref[...] < cand, 1.0, 0.0), axis=-1, keepdims=True)
                return jnp.where(cnt < need, cand, p)

            cut = lax.fori_loop(0, nbits, idx_step, jnp.zeros((R, 1), jnp.int32))
            sel = ((key_ref[...] > thr) | (aux_ref[...] <= cut)) & allowed_r
            bias_ref[...] = jnp.where(sel, 0.0, -jnp.inf)

    for g in range(G):
        if LK > topk:
            bias = bias_ref[g * QB:(g + 1) * QB, :]
        else:
            bias = jnp.where(visible(QB)[1], 0.0, -jnp.inf)
        ckv = ckv_ref[g]
        for h in range(N_HEADS):
            qa = qabs_ref[g, :, h * D_KV_LAT:(h + 1) * D_KV_LAT]
            logit = lax.dot_general(qa, ckv, dn_nt, preferred_element_type=F32) + bias
            m = jnp.max(logit, axis=-1, keepdims=True)
            p = jnp.exp2(logit - m)
            l = jnp.sum(p, axis=-1, keepdims=True)
            ctx = jnp.dot(p.astype(BF16), ckv, preferred_element_type=F32)
            o_ref[g, :, h * D_KV_LAT:(h + 1) * D_KV_LAT] = (ctx / l).astype(BF16)


def _attention(qi, kw, qabs, ckv, qb):
    B, S, _ = ckv.shape
    topk = min(TOPK_MAX, S // 4)
    buf = qabs
    G = ATTN_BATCH if B % ATTN_BATCH == 0 else 1
    for j in range(S // qb):
        lk = (j + 1) * qb
        blk = lambda n, j=j: pl.BlockSpec((G, qb, n), lambda b: (b, j, 0))
        keys = lambda n, lk=lk: pl.BlockSpec((G, lk, n), lambda b: (b, 0, 0))
        buf = pl.pallas_call(
            functools.partial(_attn_kernel, topk=topk, q0=j * qb),
            grid=(B // G,),
            in_specs=[blk(IDX_HEADS * IDX_DIM), blk(LANES), keys(LANES), blk(N_HEADS * D_KV_LAT),
                      keys(D_KV_LAT)],
            out_specs=blk(N_HEADS * D_KV_LAT),
            out_shape=jax.ShapeDtypeStruct(buf.shape, BF16),
            scratch_shapes=[pltpu.VMEM((G * qb, lk), jnp.int32), pltpu.VMEM((G * qb, lk), jnp.int32),
                            pltpu.VMEM((G * qb, lk), F32)],
            input_output_aliases={3: 0},
            compiler_params=_cparams(("arbitrary",)),
            name=f"dsa_attention_q{j}",
        )(qi, kw, kw, buf, ckv)
    return buf


def _out_kernel(ycp_ref, ctx_ref, x_ref, g1_ref, sc_ref, sh_ref, ng_ref, wcp_ref, wvo_ref,
                rw_ref, rb_ref, tri_ref, x1_ref, h2_ref, route_ref, cnt_ref, carry_ref):
    first = (pl.program_id(0) == 0) & (pl.program_id(1) == 0)

    @pl.when(first)
    def _():
        carry_ref[...] = jnp.zeros_like(carry_ref)

    mix = jnp.dot(ycp_ref[0], wcp_ref[...], preferred_element_type=F32)
    mix = mix + jnp.dot(ctx_ref[0], wvo_ref[...], preferred_element_type=F32)
    x1 = x_ref[0] + g1_ref[0] * mix
    x1_ref[0] = x1
    h2 = _rms(x1) * ng_ref[...] * (1.0 + sc_ref[0]) + sh_ref[0]
    h2_ref[0] = h2
    rw = rw_ref[...]
    rw_hi = rw.astype(BF16)
    rw_lo = (rw - rw_hi.astype(F32)).astype(BF16)
    h_hi = h2.astype(BF16)
    h_lo = (h2 - h_hi.astype(F32)).astype(BF16)
    logits = (jnp.dot(h_hi, rw_hi, preferred_element_type=F32)
              + (jnp.dot(h_lo, rw_hi, preferred_element_type=F32)
                 + jnp.dot(h_hi, rw_lo, preferred_element_type=F32))) + rb_ref[...]

    lt = logits.T
    tm = lt.shape[1]
    rowi = lax.broadcasted_iota(jnp.int32, (LANES, tm), 0)
    rowf = rowi.astype(F32)
    ninf = -jnp.inf
    big = jnp.float32(LANES)
    glog = jnp.where(rowi < N_GROUPS, lt, ninf)
    gmax = jnp.max(glog, axis=0, keepdims=True)
    gidx = jnp.min(jnp.where(glog == gmax, rowf, big), axis=0, keepdims=True)
    g_p = 1.0 / jnp.sum(jnp.exp(glog - gmax), axis=0, keepdims=True)
    lo = N_GROUPS + EXPERTS_PER_GROUP * gidx
    elog = jnp.where((rowf >= lo) & (rowf < lo + EXPERTS_PER_GROUP), lt, ninf)
    m1 = jnp.max(elog, axis=0, keepdims=True)
    i1 = jnp.min(jnp.where(elog == m1, rowf, big), axis=0, keepdims=True)
    elog2 = jnp.where(rowf == i1, ninf, elog)
    m2 = jnp.max(elog2, axis=0, keepdims=True)
    i2 = jnp.min(jnp.where(elog2 == m2, rowf, big), axis=0, keepdims=True)
    r = jnp.exp(m2 - m1)
    gate1 = g_p / (1.0 + r)
    gate2 = g_p * r / (1.0 + r)
    e1 = i1 - N_GROUPS
    e2 = i2 - N_GROUPS
    oh1 = rowf == e1
    oh2 = rowf == e2
    oh1f = jnp.where(oh1, 1.0, 0.0)
    oh2f = jnp.where(oh2, 1.0, 0.0)
    pre1 = jnp.dot(oh1f.astype(BF16), tri_ref[...], preferred_element_type=F32)
    pre2 = jnp.dot(oh2f.astype(BF16), tri_ref[...], preferred_element_type=F32)
    carry = carry_ref[:, 0:1]
    cnt1 = jnp.sum(oh1f, axis=1, keepdims=True)
    cnt2 = jnp.sum(oh2f, axis=1, keepdims=True)
    rank1 = jnp.sum(jnp.where(oh1, carry + pre1, 0.0), axis=0, keepdims=True)
    rank2 = jnp.sum(jnp.where(oh2, carry + cnt1 + pre2, 0.0), axis=0, keepdims=True)
    total = carry + cnt1 + cnt2
    carry_ref[...] = jnp.broadcast_to(total, carry_ref.shape)
    cnt_ref[...] = jnp.broadcast_to(total, cnt_ref.shape)
    route_t = jnp.where(rowi == 0, e1, jnp.where(rowi == 1, e2, jnp.where(rowi == 2, gate1, jnp.where(
        rowi == 3, gate2, jnp.where(rowi == 4, rank1, jnp.where(rowi == 5, rank2, 0.0))))))
    route_ref[0] = route_t.T


def _out_proj(ycp, ctx, x, g1, sc2, sh2, ng, wcp, wvo, rw, rb, tm):
    B, S, D = x.shape
    tok = lambda n: pl.BlockSpec((1, tm, n), lambda b, i: (b, i, 0))
    per_b = pl.BlockSpec((1, 1, D), lambda b, i: (b, 0, 0))
    full = lambda a: pl.BlockSpec(a.shape, lambda b, i: (0,) * a.ndim)
    tri = jnp.triu(jnp.ones((tm, tm), BF16), 1)
    return pl.pallas_call(
        _out_kernel,
        grid=(B, S // tm),
        in_specs=[tok(C_CONV + C_POOL), tok(N_HEADS * D_KV_LAT), tok(D), per_b, per_b, per_b,
                  full(ng), full(wcp), full(wvo), full(rw), full(rb), full(tri)],
        out_specs=[tok(D), tok(D), tok(LANES), pl.BlockSpec((LANES, LANES), lambda b, i: (0, 0))],
        out_shape=[
            jax.ShapeDtypeStruct((B, S, D), F32),
            jax.ShapeDtypeStruct((B, S, D), F32),
            jax.ShapeDtypeStruct((B, S, LANES), F32),
            jax.ShapeDtypeStruct((LANES, LANES), F32),
        ],
        scratch_shapes=[pltpu.VMEM((LANES, LANES), F32)],
        compiler_params=_cparams(("arbitrary", "arbitrary")),
        name="out_proj_router",
    )(ycp, ctx, x, g1, sc2, sh2, ng, wcp, wvo, rw, rb, tri)


def _row_copy(src, i, dst, k, sem):
    return pltpu.make_async_copy(src.at[pl.ds(i, 1), :], dst.at[pl.ds(k, 1), :], sem)


def _dispatch_kernel(dst_ref, h_ref, xs_in_ref, xs_ref, sem):
    del xs_in_ref
    tm = h_ref.shape[0]

    def issue(r, c):
        _row_copy(h_ref, r, xs_ref, dst_ref[2 * r], sem).start()
        _row_copy(h_ref, r, xs_ref, dst_ref[2 * r + 1], sem).start()
        return c

    lax.fori_loop(0, tm, issue, 0, unroll=ROW_DMA_UNROLL)

    def drain(r, c):
        _row_copy(h_ref, r, xs_ref, dst_ref[2 * r], sem).wait()
        _row_copy(h_ref, r, xs_ref, dst_ref[2 * r + 1], sem).wait()
        return c

    lax.fori_loop(0, tm, drain, 0, unroll=ROW_DMA_UNROLL)


def _dispatch(dst_flat, h2, xs0, tm):
    T, D = h2.shape
    n_rows = xs0.shape[0]
    return pl.pallas_call(
        _dispatch_kernel,
        grid=(T // tm,),
        in_specs=[pl.BlockSpec((2 * tm,), lambda i: (i,), memory_space=pltpu.SMEM),
                  pl.BlockSpec((tm, D), lambda i: (i, 0)),
                  pl.BlockSpec(memory_space=pl.ANY)],
        out_specs=pl.BlockSpec(memory_space=pl.ANY),
        out_shape=jax.ShapeDtypeStruct((n_rows, D), F32),
        scratch_shapes=[pltpu.SemaphoreType.DMA(())],
        input_output_aliases={2: 0},
        compiler_params=_cparams(("arbitrary",), row_dma=True),
        name="moe_dispatch",
    )(dst_flat, h2, xs0)


def _ffn_kernel(be_ref, nu_ref, xs_ref, w1_ref, w3_ref, w2_ref, y_ref, w13_scr, w2_scr):
    i = pl.program_id(0)
    new_expert = (i == 0) | (be_ref[i] != be_ref[jnp.maximum(i - 1, 0)])

    @pl.when(new_expert)
    def _():
        w13_scr[:, 0:D_EXPERT] = w1_ref[0, 0].astype(BF16)
        w13_scr[:, D_EXPERT:2 * D_EXPERT] = w3_ref[0, 0].astype(BF16)
        w2_scr[...] = w2_ref[0, 0].astype(BF16)

    @pl.when(i < nu_ref[0])
    def _():
        xb = xs_ref[...].astype(BF16)
        h13 = jnp.dot(xb, w13_scr[...], preferred_element_type=F32)
        h1 = h13[:, 0:D_EXPERT]
        h3 = h13[:, D_EXPERT:2 * D_EXPERT]
        act = (h1 * jax.nn.sigmoid(h1)) * h3
        y_ref[...] = jnp.dot(act.astype(BF16), w2_scr[...], preferred_element_type=F32)

    @pl.when(i >= nu_ref[0])
    def _():
        y_ref[...] = jnp.zeros_like(y_ref)


def _expert_ffn(blk_exp, n_used, xs, w1, w3, w2, layer):
    P, D = xs.shape
    nblk = P // FFN_BLOCK
    grid_spec = pltpu.PrefetchScalarGridSpec(
        num_scalar_prefetch=2,
        grid=(nblk,),
        in_specs=[
            pl.BlockSpec((FFN_BLOCK, D), lambda i, be, nu: (jnp.maximum(jnp.minimum(i, nu[0] - 1), 0), 0)),
            pl.BlockSpec((1, 1, D, D_EXPERT), lambda i, be, nu: (layer, be[i], 0, 0)),
            pl.BlockSpec((1, 1, D, D_EXPERT), lambda i, be, nu: (layer, be[i], 0, 0)),
            pl.BlockSpec((1, 1, D_EXPERT, D), lambda i, be, nu: (layer, be[i], 0, 0)),
        ],
        out_specs=pl.BlockSpec((FFN_BLOCK, D), lambda i, be, nu: (i, 0)),
        scratch_shapes=[pltpu.VMEM((D, 2 * D_EXPERT), BF16), pltpu.VMEM((D_EXPERT, D), BF16)],
    )
    return pl.pallas_call(
        _ffn_kernel,
        grid_spec=grid_spec,
        out_shape=jax.ShapeDtypeStruct((P, D), F32),
        compiler_params=_cparams(("arbitrary",)),
        name="expert_ffn",
    )(blk_exp, n_used, xs, w1, w3, w2)


def _combine_kernel(dst_ref, x1_ref, route_ref, g2_ref, fg_ref, y_ref, o_ref, ya_ref, yb_ref, sem,
                    *, final_norm):
    tm = x1_ref.shape[1]

    def issue(r, c):
        _row_copy(y_ref, dst_ref[2 * r], ya_ref, r, sem.at[0]).start()
        _row_copy(y_ref, dst_ref[2 * r + 1], yb_ref, r, sem.at[1]).start()
        return c

    lax.fori_loop(0, tm, issue, 0, unroll=ROW_DMA_UNROLL)

    def drain(r, c):
        _row_copy(y_ref, dst_ref[2 * r], ya_ref, r, sem.at[0]).wait()
        _row_copy(y_ref, dst_ref[2 * r + 1], yb_ref, r, sem.at[1]).wait()
        return c

    lax.fori_loop(0, tm, drain, 0, unroll=ROW_DMA_UNROLL)

    route = route_ref[0]
    moe = ya_ref[...] * route[:, 2:3] + yb_ref[...] * route[:, 3:4]
    out = x1_ref[0] + g2_ref[0] * moe
    if final_norm:
        out = _rms(out) * fg_ref[...]
    o_ref[0] = out


def _combine(dst_flat, x1, route, g2, fg, y, tm, final_norm):
    B, S, D = x1.shape
    nt = S // tm
    tok = lambda n: pl.BlockSpec((1, tm, n), lambda b, i: (b, i, 0))
    return pl.pallas_call(
        functools.partial(_combine_kernel, final_norm=final_norm),
        grid=(B, nt),
        in_specs=[pl.BlockSpec((2 * tm,), lambda b, i: (b * nt + i,), memory_space=pltpu.SMEM),
                  tok(D), tok(LANES),
                  pl.BlockSpec((1, 1, D), lambda b, i: (b, 0, 0)),
                  pl.BlockSpec((1, D), lambda b, i: (0, 0)),
                  pl.BlockSpec(memory_space=pl.ANY)],
        out_specs=tok(D),
        out_shape=jax.ShapeDtypeStruct((B, S, D), F32),
        scratch_shapes=[pltpu.VMEM((tm, D), F32), pltpu.VMEM((tm, D), F32), pltpu.SemaphoreType.DMA((2,))],
        compiler_params=_cparams(("arbitrary", "arbitrary"), row_dma=True),
        name="moe_combine",
    )(dst_flat, x1, route, g2, fg, y)


def _pick_tile(n, pref):
    t = min(n, pref)
    while n % t:
        t //= 2
    return t


def kernel(x, c, mod_w, mod_b, norm1_g, w_in, conv_k, conv_b, conv_ln_g, conv_ln_b, q_norm_g, kv_norm_g, w_uq, w_uk, w_uv, pool_w, pool_scale, w_out, norm2_g, router_g_w, router_g_b, router_e_w, router_e_b, exp_w1, exp_w3, exp_w2, final_g):
    B, S, D = x.shape
    L = mod_w.shape[0]
    T = B * S
    tm = _pick_tile(S, 512)
    qb = _pick_tile(S, 256)
    tmc = _pick_tile(S, 512)
    n_rows = 2 * T + N_EXPERTS * FFN_BLOCK
    nblk = n_rows // FFN_BLOCK

    mod = _modulation(c, mod_w, mod_b)
    wq_all, wvo_all = _fold_weights(w_uq, w_uk, w_uv, w_out)
    row = lambda a: a.reshape(1, -1)

    slot_buf = jnp.zeros((n_rows, D), F32)
    for l in range(L):
        sh1, sc1, g1, sh2, sc2, g2 = [mod[l, :, k * D:(k + 1) * D].reshape(B, 1, D) for k in range(6)]
        w_in_p = jnp.concatenate(
            [w_in[l, :, :N_IN - C_POOL], jnp.zeros((D, N_IN_PAD - N_IN), F32), w_in[l, :, N_IN - C_POOL:]],
            axis=1).astype(BF16)
        cu, qabs, ckv, qi, kw = _in_proj(x, sc1, sh1, row(norm1_g[l]), w_in_p, row(q_norm_g[l]),
                                         row(kv_norm_g[l]), wq_all[l], tm)
        pw_bd = jax.scipy.linalg.block_diag(*[pool_w[l, g] for g in range(pool_w.shape[1])]).astype(BF16)
        ycp = _conv_pool(cu, conv_k[l], row(conv_b[l]), row(conv_ln_g[l]), row(conv_ln_b[l]),
                         pw_bd, row(pool_scale[l]))
        ctx = _attention(qi, kw, qabs, ckv, qb)
        wcp = jnp.concatenate([w_out[l, :C_CONV], w_out[l, D - C_POOL:]], axis=0).astype(BF16)
        n_r = N_GROUPS + N_EXPERTS
        rw = jnp.concatenate([router_g_w[l], router_e_w[l].reshape(D, N_EXPERTS),
                              jnp.zeros((D, LANES - n_r), F32)], axis=1)
        rb = jnp.concatenate([router_g_b[l], router_e_b[l].reshape(N_EXPERTS),
                              jnp.zeros((LANES - n_r,), F32)]).reshape(1, LANES)
        x1, h2, route, cnt = _out_proj(ycp, ctx, x, g1, sc2, sh2, row(norm2_g[l]), wcp, wvo_all[l],
                                       rw, rb, tm)
        counts = cnt[:N_EXPERTS, 0].astype(jnp.int32)
        pcounts = (counts + FFN_BLOCK - 1) // FFN_BLOCK * FFN_BLOCK
        pends = jnp.cumsum(pcounts)
        pstarts = pends - pcounts
        route2 = route.reshape(T, LANES)
        e_idx = route2[:, 0:2].astype(jnp.int32)
        e_start = jnp.sum(jnp.where(e_idx[..., None] == jnp.arange(N_EXPERTS, dtype=jnp.int32), pstarts, 0),
                          axis=-1)
        dst = (e_start + route2[:, 4:6].astype(jnp.int32)).reshape(2 * T)
        n_used = (pends[-1] // FFN_BLOCK).astype(jnp.int32).reshape(1)
        blk_start = jnp.arange(nblk, dtype=jnp.int32) * FFN_BLOCK
        blk_exp = jnp.minimum(jnp.sum((pends[None, :] <= blk_start[:, None]).astype(jnp.int32), axis=1),
                              N_EXPERTS - 1)
        xs = _dispatch(dst, h2.reshape(T, D), slot_buf, tmc)
        y = _expert_ffn(blk_exp, n_used, xs, exp_w1, exp_w3, exp_w2, l)
        slot_buf = y
        x = _combine(dst, x1, route, g2, row(final_g), y, tmc, final_norm=(l == L - 1))
    return x
```

```python
import functools

import jax
import jax.numpy as jnp
from jax import lax
from jax.experimental import pallas as pl
from jax.experimental.pallas import tpu as pltpu

F32 = jnp.float32
BF16 = jnp.bfloat16
HIGHEST = lax.Precision.HIGHEST

EPS = 1e-6
CHUNK = 64
CONV_WIDTH = 31
C_CONV = 256
C_POOL = 256
POOL_GROUP = 64
N_HEADS = 8
HEAD_DIM = 64
D_Q_LAT = 256
D_KV_LAT = 128
IDX_HEADS = 8
IDX_DIM = 64
TOPK_MAX = 256
N_GROUPS = 4
EXPERTS_PER_GROUP = 8
N_EXPERTS = 32
D_EXPERT = 512
N_IN = 1736
N_IN_PAD = 1792
KW_OFF = 1408
POOL_OFF = 1536
LANES = 128
HALO = 32
CONV_TILE = 64
FFN_BLOCK = 256
ROW_DMA_UNROLL = 8
ATTN_BATCH = 2
ATTN_BATCH_MAX_KEYS = 1024
LOG2E = 1.4426950408889634
VMEM_LIMIT = 56 * 1024 * 1024


def _cparams(sem, row_dma=False):
    return pltpu.CompilerParams(dimension_semantics=sem, vmem_limit_bytes=VMEM_LIMIT,
                                disable_bounds_checks=row_dma)


def _mod_kernel(c_ref, w_ref, b_ref, o_ref):
    c = c_ref[...]
    cond = c * jax.nn.sigmoid(c)
    o_ref[0] = jnp.dot(cond, w_ref[0], precision=HIGHEST, preferred_element_type=F32) + b_ref[0]


def _modulation(c, mod_w, mod_b):
    L, D, D6 = mod_w.shape
    B = c.shape[0]
    nj = D6 // D
    return pl.pallas_call(
        _mod_kernel,
        grid=(L, nj),
        in_specs=[
            pl.BlockSpec((B, D), lambda l, j: (0, 0)),
            pl.BlockSpec((1, D, D), lambda l, j: (l, 0, j)),
            pl.BlockSpec((1, 1, D), lambda l, j: (l, 0, j)),
        ],
        out_specs=pl.BlockSpec((1, B, D), lambda l, j: (l, 0, j)),
        out_shape=jax.ShapeDtypeStruct((L, B, D6), F32),
        compiler_params=_cparams(("arbitrary", "arbitrary")),
        name="modulation",
    )(c, mod_w, mod_b.reshape(L, 1, D6))


def _fold_kernel(uq_ref, ukt_ref, uv_ref, wo_ref, wq_ref, wvo_ref):
    wq_ref[0] = jnp.dot(uq_ref[0, 0], ukt_ref[0, 0], precision=HIGHEST,
                        preferred_element_type=F32).astype(BF16)
    wvo_ref[0] = jnp.dot(uv_ref[0, 0], wo_ref[0, 0], precision=HIGHEST,
                         preferred_element_type=F32).astype(BF16)


def _fold_weights(w_uq, w_uk, w_uv, w_out):
    L = w_uq.shape[0]
    D = w_out.shape[-1]
    uq = jnp.transpose(w_uq, (0, 2, 1, 3))
    ukt = jnp.transpose(w_uk, (0, 2, 3, 1))
    uv = jnp.transpose(w_uv, (0, 2, 1, 3))
    wo = w_out[:, C_CONV:C_CONV + N_HEADS * HEAD_DIM, :].reshape(L, N_HEADS, HEAD_DIM, D)
    return pl.pallas_call(
        _fold_kernel,
        grid=(L, N_HEADS),
        in_specs=[
            pl.BlockSpec((1, 1, D_Q_LAT, HEAD_DIM), lambda l, h: (l, h, 0, 0)),
            pl.BlockSpec((1, 1, HEAD_DIM, D_KV_LAT), lambda l, h: (l, h, 0, 0)),
            pl.BlockSpec((1, 1, D_KV_LAT, HEAD_DIM), lambda l, h: (l, h, 0, 0)),
            pl.BlockSpec((1, 1, HEAD_DIM, D), lambda l, h: (l, h, 0, 0)),
        ],
        out_specs=[
            pl.BlockSpec((1, D_Q_LAT, D_KV_LAT), lambda l, h: (l, 0, h)),
            pl.BlockSpec((1, D_KV_LAT, D), lambda l, h: (l, h, 0)),
        ],
        out_shape=[
            jax.ShapeDtypeStruct((L, D_Q_LAT, N_HEADS * D_KV_LAT), BF16),
            jax.ShapeDtypeStruct((L, N_HEADS * D_KV_LAT, D), BF16),
        ],
        compiler_params=_cparams(("arbitrary", "arbitrary")),
        name="fold_weights",
    )(uq, ukt, uv, wo)


def _rms(v):
    return v * lax.rsqrt(jnp.mean(v * v, axis=-1, keepdims=True) + EPS)


def _in_kernel(x_ref, sc_ref, sh_ref, g_ref, w_ref, qg_ref, kvg_ref, wq_ref,
               cu_ref, qabs_ref, ckv_ref, qi_ref, kw_ref):
    x = x_ref[0]
    h = _rms(x) * g_ref[...] * (1.0 + sc_ref[0]) + sh_ref[0]
    z = jnp.dot(h.astype(BF16), w_ref[...], preferred_element_type=F32)
    a_val = z[:, 0:C_CONV]
    a_gate = z[:, C_CONV:2 * C_CONV]
    cu_ref[0, :, 0:C_CONV] = a_val * jax.nn.sigmoid(a_gate)
    cu_ref[0, :, C_CONV:C_CONV + C_POOL] = z[:, POOL_OFF:POOL_OFF + C_POOL]
    o = 2 * C_CONV
    cq = _rms(z[:, o:o + D_Q_LAT]) * qg_ref[...]
    qabs = jnp.dot(cq.astype(BF16), wq_ref[...], preferred_element_type=F32)
    qabs_ref[0] = (qabs * (HEAD_DIM ** -0.5 * LOG2E)).astype(BF16)
    o += D_Q_LAT
    ckv_ref[0] = (_rms(z[:, o:o + D_KV_LAT]) * kvg_ref[...]).astype(BF16)
    o += D_KV_LAT
    qi_ref[0] = z[:, o:o + IDX_HEADS * IDX_DIM].astype(BF16)
    kw_ref[0] = z[:, KW_OFF:KW_OFF + LANES]


def _in_proj(x, sc, sh, g, w_in_p, qg, kvg, wq, tm):
    B, S, D = x.shape
    tok = lambda n: pl.BlockSpec((1, tm, n), lambda b, i: (b, i, 0))
    per_b = pl.BlockSpec((1, 1, D), lambda b, i: (b, 0, 0))
    full = lambda a: pl.BlockSpec(a.shape, lambda b, i: (0,) * a.ndim)
    nq = IDX_HEADS * IDX_DIM
    return pl.pallas_call(
        _in_kernel,
        grid=(B, S // tm),
        in_specs=[tok(D), per_b, per_b, full(g), full(w_in_p), full(qg), full(kvg), full(wq)],
        out_specs=[tok(C_CONV + C_POOL), tok(N_HEADS * D_KV_LAT), tok(D_KV_LAT), tok(nq), tok(LANES)],
        out_shape=[
            jax.ShapeDtypeStruct((B, S, C_CONV + C_POOL), F32),
            jax.ShapeDtypeStruct((B, S, N_HEADS * D_KV_LAT), BF16),
            jax.ShapeDtypeStruct((B, S, D_KV_LAT), BF16),
            jax.ShapeDtypeStruct((B, S, nq), BF16),
            jax.ShapeDtypeStruct((B, S, LANES), F32),
        ],
        compiler_params=_cparams(("arbitrary", "arbitrary")),
        name="in_proj",
    )(x, sc, sh, g, w_in_p, qg, kvg, wq)


def _cp_kernel(cu_ref, ck_ref, cb_ref, lg_ref, lb_ref, pw_ref, ps_ref, o_ref, pad_ref):
    S = cu_ref.shape[1]
    TT = CONV_TILE
    pad_ref[0:HALO, :] = jnp.zeros((HALO, C_CONV + C_POOL), F32)
    pad_ref[HALO:HALO + S, :] = cu_ref[0]
    lane = lax.broadcasted_iota(jnp.int32, (TT, C_POOL), 1)
    row = lax.broadcasted_iota(jnp.int32, (TT, C_POOL), 0)
    win_len = jnp.where(lane < POOL_GROUP, 2,
                        jnp.where(lane < 2 * POOL_GROUP, 4, jnp.where(lane < 3 * POOL_GROUP, 8, 16)))

    def tile(i, carry):
        t0 = pl.multiple_of(i * TT, TT)
        win = pad_ref[pl.ds(t0, TT + HALO), :]
        wc = win[:, 0:C_CONV]
        wp = win[:, C_CONV:C_CONV + C_POOL]
        acc = jnp.zeros((TT, C_CONV), F32)
        for s in range(8):
            ws = wc if s == 0 else pltpu.roll(wc, TT + HALO - s, 0)
            for j in range(CONV_WIDTH):
                o = HALO - (CONV_WIDTH - 1) + j
                if o % 8 == s:
                    acc = acc + ck_ref[j:j + 1, :] * ws[o - s:o - s + TT, :]
        v = acc + cb_ref[...]
        mu = jnp.mean(v, axis=-1, keepdims=True)
        vc = v - mu
        var = jnp.mean(vc * vc, axis=-1, keepdims=True)
        yn = vc * lax.rsqrt(var + EPS) * lg_ref[...] + lb_ref[...]
        o_ref[0, pl.ds(t0, TT), 0:C_CONV] = (yn * jax.nn.sigmoid(yn)).astype(BF16)
        lo = HALO - 15
        s2 = wp[lo + 1:, :] + wp[lo:-1, :]
        s4 = s2[2:, :] + s2[:-2, :]
        s8 = s4[4:, :] + s4[:-4, :]
        s16 = s8[8:, :] + s8[:-8, :]
        u = wp[HALO:, :]
        sw = jnp.where(lane < POOL_GROUP, s2[14:, :],
                       jnp.where(lane < 2 * POOL_GROUP, s4[12:, :],
                                 jnp.where(lane < 3 * POOL_GROUP, s8[8:, :], s16)))
        n_pos = jnp.minimum(t0 + row + 1, win_len).astype(F32)
        d = (sw / n_pos - u).astype(BF16)
        yp = jnp.dot(d, pw_ref[...], preferred_element_type=F32) * ps_ref[...]
        o_ref[0, pl.ds(t0, TT), C_CONV:C_CONV + C_POOL] = yp.astype(BF16)
        return carry

    lax.fori_loop(0, S // TT, tile, 0)


def _conv_pool(cu, ck, cb, lg, lb, pw_bd, ps):
    B, S, W = cu.shape
    full = lambda a: pl.BlockSpec(a.shape, lambda b: (0,) * a.ndim)
    return pl.pallas_call(
        _cp_kernel,
        grid=(B,),
        in_specs=[pl.BlockSpec((1, S, W), lambda b: (b, 0, 0)),
                  full(ck), full(cb), full(lg), full(lb), full(pw_bd), full(ps)],
        out_specs=pl.BlockSpec((1, S, W), lambda b: (b, 0, 0)),
        out_shape=jax.ShapeDtypeStruct((B, S, W), BF16),
        scratch_shapes=[pltpu.VMEM((S + HALO, W), F32)],
        compiler_params=_cparams(("arbitrary",)),
        name="conv_pool",
    )(cu, ck, cb, lg, lb, pw_bd, ps)


def _attn_kernel(qi_ref, kwq_ref, kwk_ref, qabs_ref, ckv_ref, o_ref, key_ref, aux_ref, bias_ref, *, topk, q0):
    G, QB = qi_ref.shape[0], qi_ref.shape[1]
    LK = ckv_ref.shape[1]
    R = G * QB
    dn_nt = (((1,), (1,)), ((), ()))

    def visible(rows):
        kidx = lax.broadcasted_iota(jnp.int32, (rows, LK), 1)
        qpos = q0 + lax.broadcasted_iota(jnp.int32, (rows, 1), 0) % QB
        return kidx, kidx < (qpos // CHUNK + 1) * CHUNK

    if LK > topk:
        _, allowed = visible(QB)
        for g in range(G):
            ki = kwk_ref[g, :, 0:IDX_DIM].astype(BF16)
            wi = kwq_ref[g, :, IDX_DIM:IDX_DIM + IDX_HEADS]
            qi = qi_ref[g]
            score = jnp.zeros((QB, LK), F32)
            for h in range(IDX_HEADS):
                rel = lax.dot_general(qi[:, h * IDX_DIM:(h + 1) * IDX_DIM], ki, dn_nt,
                                      preferred_element_type=F32)
                score = score + jnp.maximum(rel, 0.0) * wi[:, h:h + 1]
            score = score * ((IDX_HEADS * IDX_DIM) ** -0.5)
            score = jnp.where(score == 0.0, 0.0, score)
            score = jnp.where(allowed, score, -jnp.inf)
            bits = lax.bitcast_convert_type(score, jnp.int32)
            key_ref[g * QB:(g + 1) * QB, :] = bits ^ ((bits >> 31) & jnp.int32(0x7FFFFFFF))

        kf = jnp.float32(topk)

        H = R // 2

        def partial_counts(r0, cand):
            acc = jnp.zeros((H, LANES), F32)
            for c in range(LK // LANES):
                k = key_ref[r0:r0 + H, c * LANES:(c + 1) * LANES]
                acc = acc + jnp.where(k >= cand, 1.0, 0.0)
            return acc

        def settle(acc, cand, pre):
            return jnp.where(jnp.sum(acc, axis=-1, keepdims=True) >= kf, cand, pre)

        def bit(i):
            return jnp.where(i <= 31, jnp.int32(1) << jnp.maximum(31 - i, 0), 0)

        def bit_step(i, carry):
            pre_a, pre_b, acc_b = carry
            cand_a = pre_a + bit(i)
            acc_a = partial_counts(0, cand_a)
            pre_b = settle(acc_b, pre_b + bit(i), pre_b)
            acc_b = partial_counts(H, pre_b + bit(i + 1))
            pre_a = settle(acc_a, cand_a, pre_a)
            return pre_a, pre_b, acc_b

        lowest = jnp.full((H, 1), jnp.iinfo(jnp.int32).min, jnp.int32)
        pre_a, pre_b, _ = lax.fori_loop(0, 32, bit_step,
                                        (lowest, lowest, partial_counts(H, lowest + bit(0))))
        thr = jnp.concatenate([pre_a, pre_b], axis=0)
        reached = key_ref[...] >= thr
        bias_ref[...] = jnp.where(reached, 0.0, -jnp.inf)
        has_excess = jnp.max(jnp.sum(jnp.where(reached, 1.0, 0.0), axis=-1, keepdims=True)) > kf

        @pl.when(has_excess)
        def _():
            kidx, allowed_r = visible(R)
            key = key_ref[...]
            n_gt = jnp.sum(jnp.where(key > thr, 1.0, 0.0), axis=-1, keepdims=True)
            need = kf - n_gt
            aux_ref[...] = jnp.where(key == thr, kidx, jnp.int32(LK))
            nbits = (LK - 1).bit_length()

            def idx_step(i, p):
                cand = p | (jnp.int32(1) << (nbits - 1 - i))
                cnt = jnp.sum(jnp.where(aux_ref[...] < cand, 1.0, 0.0), axis=-1, keepdims=True)
                return jnp.where(cnt < need, cand, p)

            cut = lax.fori_loop(0, nbits, idx_step, jnp.zeros((R, 1), jnp.int32))
            sel = ((key_ref[...] > thr) | (aux_ref[...] <= cut)) & allowed_r
            bias_ref[...] = jnp.where(sel, 0.0, -jnp.inf)

    for g in range(G):
        if LK > topk:
            bias = bias_ref[g * QB:(g + 1) * QB, :]
        else:
            bias = jnp.where(visible(QB)[1], 0.0, -jnp.inf)
        ckv = ckv_ref[g]
        for h in range(N_HEADS):
            qa = qabs_ref[g, :, h * D_KV_LAT:(h + 1) * D_KV_LAT]
            logit = lax.dot_general(qa, ckv, dn_nt, preferred_element_type=F32) + bias
            m = jnp.max(logit, axis=-1, keepdims=True)
            p = jnp.exp2(logit - m)
            l = jnp.sum(p, axis=-1, keepdims=True)
            ctx = jnp.dot(p.astype(BF16), ckv, preferred_element_type=F32)
            o_ref[g, :, h * D_KV_LAT:(h + 1) * D_KV_LAT] = (ctx / l).astype(BF16)


def _attention(qi, kw, qabs, ckv, qb):
    B, S, _ = ckv.shape
    topk = min(TOPK_MAX, S // 4)
    buf = qabs
    for j in range(S // qb):
        lk = (j + 1) * qb
        G = ATTN_BATCH if (B % ATTN_BATCH == 0 and lk <= ATTN_BATCH_MAX_KEYS) else 1
        blk = lambda n, j=j: pl.BlockSpec((G, qb, n), lambda b: (b, j, 0))
        keys = lambda n, lk=lk: pl.BlockSpec((G, lk, n), lambda b: (b, 0, 0))
        buf = pl.pallas_call(
            functools.partial(_attn_kernel, topk=topk, q0=j * qb),
            grid=(B // G,),
            in_specs=[blk(IDX_HEADS * IDX_DIM), blk(LANES), keys(LANES), blk(N_HEADS * D_KV_LAT),
                      keys(D_KV_LAT)],
            out_specs=blk(N_HEADS * D_KV_LAT),
            out_shape=jax.ShapeDtypeStruct(buf.shape, BF16),
            scratch_shapes=[pltpu.VMEM((G * qb, lk), jnp.int32), pltpu.VMEM((G * qb, lk), jnp.int32),
                            pltpu.VMEM((G * qb, lk), F32)],
            input_output_aliases={3: 0},
            compiler_params=_cparams(("arbitrary",)),
            name=f"dsa_attention_q{j}",
        )(qi, kw, kw, buf, ckv)
    return buf


def _out_kernel(ycp_ref, ctx_ref, x_ref, g1_ref, sc_ref, sh_ref, ng_ref, wcp_ref, wvo_ref,
                rw_ref, rb_ref, tri_ref, x1_ref, h2_ref, route_ref, cnt_ref, carry_ref):
    first = (pl.program_id(0) == 0) & (pl.program_id(1) == 0)

    @pl.when(first)
    def _():
        carry_ref[...] = jnp.zeros_like(carry_ref)

    mix = jnp.dot(ycp_ref[0], wcp_ref[...], preferred_element_type=F32)
    mix = mix + jnp.dot(ctx_ref[0], wvo_ref[...], preferred_element_type=F32)
    x1 = x_ref[0] + g1_ref[0] * mix
    x1_ref[0] = x1
    h2 = _rms(x1) * ng_ref[...] * (1.0 + sc_ref[0]) + sh_ref[0]
    h2_ref[0] = h2
    rw = rw_ref[...]
    rw_hi = rw.astype(BF16)
    rw_lo = (rw - rw_hi.astype(F32)).astype(BF16)
    h_hi = h2.astype(BF16)
    h_lo = (h2 - h_hi.astype(F32)).astype(BF16)
    logits = (jnp.dot(h_hi, rw_hi, preferred_element_type=F32)
              + (jnp.dot(h_lo, rw_hi, preferred_element_type=F32)
                 + jnp.dot(h_hi, rw_lo, preferred_element_type=F32))) + rb_ref[...]

    lt = logits.T
    tm = lt.shape[1]
    rowi = lax.broadcasted_iota(jnp.int32, (LANES, tm), 0)
    rowf = rowi.astype(F32)
    ninf = -jnp.inf
    big = jnp.float32(LANES)
    glog = jnp.where(rowi < N_GROUPS, lt, ninf)
    gmax = jnp.max(glog, axis=0, keepdims=True)
    gidx = jnp.min(jnp.where(glog == gmax, rowf, big), axis=0, keepdims=True)
    g_p = 1.0 / jnp.sum(jnp.exp(glog - gmax), axis=0, keepdims=True)
    lo = N_GROUPS + EXPERTS_PER_GROUP * gidx
    elog = jnp.where((rowf >= lo) & (rowf < lo + EXPERTS_PER_GROUP), lt, ninf)
    m1 = jnp.max(elog, axis=0, keepdims=True)
    i1 = jnp.min(jnp.where(elog == m1, rowf, big), axis=0, keepdims=True)
    elog2 = jnp.where(rowf == i1, ninf, elog)
    m2 = jnp.max(elog2, axis=0, keepdims=True)
    i2 = jnp.min(jnp.where(elog2 == m2, rowf, big), axis=0, keepdims=True)
    r = jnp.exp(m2 - m1)
    gate1 = g_p / (1.0 + r)
    gate2 = g_p * r / (1.0 + r)
    e1 = i1 - N_GROUPS
    e2 = i2 - N_GROUPS
    oh1 = rowf == e1
    oh2 = rowf == e2
    oh1f = jnp.where(oh1, 1.0, 0.0)
    oh2f = jnp.where(oh2, 1.0, 0.0)
    pre1 = jnp.dot(oh1f.astype(BF16), tri_ref[...], preferred_element_type=F32)
    pre2 = jnp.dot(oh2f.astype(BF16), tri_ref[...], preferred_element_type=F32)
    carry = carry_ref[:, 0:1]
    cnt1 = jnp.sum(oh1f, axis=1, keepdims=True)
    cnt2 = jnp.sum(oh2f, axis=1, keepdims=True)
    rank1 = jnp.sum(jnp.where(oh1, carry + pre1, 0.0), axis=0, keepdims=True)
    rank2 = jnp.sum(jnp.where(oh2, carry + cnt1 + pre2, 0.0), axis=0, keepdims=True)
    total = carry + cnt1 + cnt2
    carry_ref[...] = jnp.broadcast_to(total, carry_ref.shape)
    cnt_ref[...] = jnp.broadcast_to(total, cnt_ref.shape)
    route_t = jnp.where(rowi == 0, e1, jnp.where(rowi == 1, e2, jnp.where(rowi == 2, gate1, jnp.where(
        rowi == 3, gate2, jnp.where(rowi == 4, rank1, jnp.where(rowi == 5, rank2, 0.0))))))
    route_ref[0] = route_t.T


def _out_proj(ycp, ctx, x, g1, sc2, sh2, ng, wcp, wvo, rw, rb, tm):
    B, S, D = x.shape
    tok = lambda n: pl.BlockSpec((1, tm, n), lambda b, i: (b, i, 0))
    per_b = pl.BlockSpec((1, 1, D), lambda b, i: (b, 0, 0))
    full = lambda a: pl.BlockSpec(a.shape, lambda b, i: (0,) * a.ndim)
    tri = jnp.triu(jnp.ones((tm, tm), BF16), 1)
    return pl.pallas_call(
        _out_kernel,
        grid=(B, S // tm),
        in_specs=[tok(C_CONV + C_POOL), tok(N_HEADS * D_KV_LAT), tok(D), per_b, per_b, per_b,
                  full(ng), full(wcp), full(wvo), full(rw), full(rb), full(tri)],
        out_specs=[tok(D), tok(D), tok(LANES), pl.BlockSpec((LANES, LANES), lambda b, i: (0, 0))],
        out_shape=[
            jax.ShapeDtypeStruct((B, S, D), F32),
            jax.ShapeDtypeStruct((B, S, D), F32),
            jax.ShapeDtypeStruct((B, S, LANES), F32),
            jax.ShapeDtypeStruct((LANES, LANES), F32),
        ],
        scratch_shapes=[pltpu.VMEM((LANES, LANES), F32)],
        compiler_params=_cparams(("arbitrary", "arbitrary")),
        name="out_proj_router",
    )(ycp, ctx, x, g1, sc2, sh2, ng, wcp, wvo, rw, rb, tri)


def _row_copy(src, i, dst, k, sem):
    return pltpu.make_async_copy(src.at[pl.ds(i, 1), :], dst.at[pl.ds(k, 1), :], sem)


def _dispatch_kernel(dst_ref, h_ref, xs_in_ref, xs_ref, sem):
    del xs_in_ref
    tm = h_ref.shape[0]

    def issue(r, c):
        _row_copy(h_ref, r, xs_ref, dst_ref[2 * r], sem).start()
        _row_copy(h_ref, r, xs_ref, dst_ref[2 * r + 1], sem).start()
        return c

    lax.fori_loop(0, tm, issue, 0, unroll=ROW_DMA_UNROLL)

    def drain(r, c):
        _row_copy(h_ref, r, xs_ref, dst_ref[2 * r], sem).wait()
        _row_copy(h_ref, r, xs_ref, dst_ref[2 * r + 1], sem).wait()
        return c

    lax.fori_loop(0, tm, drain, 0, unroll=ROW_DMA_UNROLL)


def _dispatch(dst_flat, h2, xs0, tm):
    T, D = h2.shape
    n_rows = xs0.shape[0]
    return pl.pallas_call(
        _dispatch_kernel,
        grid=(T // tm,),
        in_specs=[pl.BlockSpec((2 * tm,), lambda i: (i,), memory_space=pltpu.SMEM),
                  pl.BlockSpec((tm, D), lambda i: (i, 0)),
                  pl.BlockSpec(memory_space=pl.ANY)],
        out_specs=pl.BlockSpec(memory_space=pl.ANY),
        out_shape=jax.ShapeDtypeStruct((n_rows, D), F32),
        scratch_shapes=[pltpu.SemaphoreType.DMA(())],
        input_output_aliases={2: 0},
        compiler_params=_cparams(("arbitrary",), row_dma=True),
        name="moe_dispatch",
    )(dst_flat, h2, xs0)


def _ffn_kernel(be_ref, nu_ref, xs_ref, w1_ref, w3_ref, w2_ref, y_ref, w13_scr, w2_scr):
    i = pl.program_id(0)
    new_expert = (i == 0) | (be_ref[i] != be_ref[jnp.maximum(i - 1, 0)])

    @pl.when(new_expert)
    def _():
        w13_scr[:, 0:D_EXPERT] = w1_ref[0, 0].astype(BF16)
        w13_scr[:, D_EXPERT:2 * D_EXPERT] = w3_ref[0, 0].astype(BF16)
        w2_scr[...] = w2_ref[0, 0].astype(BF16)

    @pl.when(i < nu_ref[0])
    def _():
        xb = xs_ref[...].astype(BF16)
        h13 = jnp.dot(xb, w13_scr[...], preferred_element_type=F32)
        h1 = h13[:, 0:D_EXPERT]
        h3 = h13[:, D_EXPERT:2 * D_EXPERT]
        act = (h1 * jax.nn.sigmoid(h1)) * h3
        y_ref[...] = jnp.dot(act.astype(BF16), w2_scr[...], preferred_element_type=F32)

    @pl.when(i >= nu_ref[0])
    def _():
        y_ref[...] = jnp.zeros_like(y_ref)


def _expert_ffn(blk_exp, n_used, xs, w1, w3, w2, layer):
    P, D = xs.shape
    nblk = P // FFN_BLOCK
    grid_spec = pltpu.PrefetchScalarGridSpec(
        num_scalar_prefetch=2,
        grid=(nblk,),
        in_specs=[
            pl.BlockSpec((FFN_BLOCK, D), lambda i, be, nu: (jnp.maximum(jnp.minimum(i, nu[0] - 1), 0), 0)),
            pl.BlockSpec((1, 1, D, D_EXPERT), lambda i, be, nu: (layer, be[i], 0, 0)),
            pl.BlockSpec((1, 1, D, D_EXPERT), lambda i, be, nu: (layer, be[i], 0, 0)),
            pl.BlockSpec((1, 1, D_EXPERT, D), lambda i, be, nu: (layer, be[i], 0, 0)),
        ],
        out_specs=pl.BlockSpec((FFN_BLOCK, D), lambda i, be, nu: (i, 0)),
        scratch_shapes=[pltpu.VMEM((D, 2 * D_EXPERT), BF16), pltpu.VMEM((D_EXPERT, D), BF16)],
    )
    return pl.pallas_call(
        _ffn_kernel,
        grid_spec=grid_spec,
        out_shape=jax.ShapeDtypeStruct((P, D), F32),
        compiler_params=_cparams(("arbitrary",)),
        name="expert_ffn",
    )(blk_exp, n_used, xs, w1, w3, w2)


def _combine_kernel(dst_ref, x1_ref, route_ref, g2_ref, fg_ref, y_ref, o_ref, ya_ref, yb_ref, sem,
                    *, final_norm):
    tm = x1_ref.shape[1]

    def issue(r, c):
        _row_copy(y_ref, dst_ref[2 * r], ya_ref, r, sem.at[0]).start()
        _row_copy(y_ref, dst_ref[2 * r + 1], yb_ref, r, sem.at[1]).start()
        return c

    lax.fori_loop(0, tm, issue, 0, unroll=ROW_DMA_UNROLL)

    def drain(r, c):
        _row_copy(y_ref, dst_ref[2 * r], ya_ref, r, sem.at[0]).wait()
        _row_copy(y_ref, dst_ref[2 * r + 1], yb_ref, r, sem.at[1]).wait()
        return c

    lax.fori_loop(0, tm, drain, 0, unroll=ROW_DMA_UNROLL)

    route = route_ref[0]
    moe = ya_ref[...] * route[:, 2:3] + yb_ref[...] * route[:, 3:4]
    out = x1_ref[0] + g2_ref[0] * moe
    if final_norm:
        out = _rms(out) * fg_ref[...]
    o_ref[0] = out


def _combine(dst_flat, x1, route, g2, fg, y, tm, final_norm):
    B, S, D = x1.shape
    nt = S // tm
    tok = lambda n: pl.BlockSpec((1, tm, n), lambda b, i: (b, i, 0))
    return pl.pallas_call(
        functools.partial(_combine_kernel, final_norm=final_norm),
        grid=(B, nt),
        in_specs=[pl.BlockSpec((2 * tm,), lambda b, i: (b * nt + i,), memory_space=pltpu.SMEM),
                  tok(D), tok(LANES),
                  pl.BlockSpec((1, 1, D), lambda b, i: (b, 0, 0)),
                  pl.BlockSpec((1, D), lambda b, i: (0, 0)),
                  pl.BlockSpec(memory_space=pl.ANY)],
        out_specs=tok(D),
        out_shape=jax.ShapeDtypeStruct((B, S, D), F32),
        scratch_shapes=[pltpu.VMEM((tm, D), F32), pltpu.VMEM((tm, D), F32), pltpu.SemaphoreType.DMA((2,))],
        compiler_params=_cparams(("arbitrary", "arbitrary"), row_dma=True),
        name="moe_combine",
    )(dst_flat, x1, route, g2, fg, y)


def _pick_tile(n, pref):
    t = min(n, pref)
    while n % t:
        t //= 2
    return t


def kernel(x, c, mod_w, mod_b, norm1_g, w_in, conv_k, conv_b, conv_ln_g, conv_ln_b, q_norm_g, kv_norm_g, w_uq, w_uk, w_uv, pool_w, pool_scale, w_out, norm2_g, router_g_w, router_g_b, router_e_w, router_e_b, exp_w1, exp_w3, exp_w2, final_g):
    B, S, D = x.shape
    L = mod_w.shape[0]
    T = B * S
    tm = _pick_tile(S, 512)
    qb = _pick_tile(S, 256)
    tmc = _pick_tile(S, 512)
    n_rows = 2 * T + N_EXPERTS * FFN_BLOCK
    nblk = n_rows // FFN_BLOCK

    mod = _modulation(c, mod_w, mod_b)
    wq_all, wvo_all = _fold_weights(w_uq, w_uk, w_uv, w_out)
    row = lambda a: a.reshape(1, -1)

    slot_buf = jnp.zeros((n_rows, D), F32)
    for l in range(L):
        sh1, sc1, g1, sh2, sc2, g2 = [mod[l, :, k * D:(k + 1) * D].reshape(B, 1, D) for k in range(6)]
        w_in_p = jnp.concatenate(
            [w_in[l, :, :N_IN - C_POOL], jnp.zeros((D, N_IN_PAD - N_IN), F32), w_in[l, :, N_IN - C_POOL:]],
            axis=1).astype(BF16)
        cu, qabs, ckv, qi, kw = _in_proj(x, sc1, sh1, row(norm1_g[l]), w_in_p, row(q_norm_g[l]),
                                         row(kv_norm_g[l]), wq_all[l], tm)
        pw_bd = jax.scipy.linalg.block_diag(*[pool_w[l, g] for g in range(pool_w.shape[1])]).astype(BF16)
        ycp = _conv_pool(cu, conv_k[l], row(conv_b[l]), row(conv_ln_g[l]), row(conv_ln_b[l]),
                         pw_bd, row(pool_scale[l]))
        ctx = _attention(qi, kw, qabs, ckv, qb)
        wcp = jnp.concatenate([w_out[l, :C_CONV], w_out[l, D - C_POOL:]], axis=0).astype(BF16)
        n_r = N_GROUPS + N_EXPERTS
        rw = jnp.concatenate([router_g_w[l], router_e_w[l].reshape(D, N_EXPERTS),
                              jnp.zeros((D, LANES - n_r), F32)], axis=1)
        rb = jnp.concatenate([router_g_b[l], router_e_b[l].reshape(N_EXPERTS),
                              jnp.zeros((LANES - n_r,), F32)]).reshape(1, LANES)
        x1, h2, route, cnt = _out_proj(ycp, ctx, x, g1, sc2, sh2, row(norm2_g[l]), wcp, wvo_all[l],
                                       rw, rb, tm)
        counts = cnt[:N_EXPERTS, 0].astype(jnp.int32)
        pcounts = (counts + FFN_BLOCK - 1) // FFN_BLOCK * FFN_BLOCK
        pends = jnp.cumsum(pcounts)
        pstarts = pends - pcounts
        route2 = route.reshape(T, LANES)
        e_idx = route2[:, 0:2].astype(jnp.int32)
        e_start = jnp.sum(jnp.where(e_idx[..., None] == jnp.arange(N_EXPERTS, dtype=jnp.int32), pstarts, 0),
                          axis=-1)
        dst = (e_start + route2[:, 4:6].astype(jnp.int32)).reshape(2 * T)
        n_used = (pends[-1] // FFN_BLOCK).astype(jnp.int32).reshape(1)
        blk_start = jnp.arange(nblk, dtype=jnp.int32) * FFN_BLOCK
        blk_exp = jnp.minimum(jnp.sum((pends[None, :] <= blk_start[:, None]).astype(jnp.int32), axis=1),
                              N_EXPERTS - 1)
        xs = _dispatch(dst, h2.reshape(T, D), slot_buf, tmc)
        y = _expert_ffn(blk_exp, n_used, xs, exp_w1, exp_w3, exp_w2, l)
        slot_buf = y
        x = _combine(dst, x1, route, g2, row(final_g), y, tmc, final_norm=(l == L - 1))
    return x
```

```python
import functools

import jax
import jax.numpy as jnp
from jax import lax
from jax.experimental import pallas as pl
from jax.experimental.pallas import tpu as pltpu

F32 = jnp.float32
BF16 = jnp.bfloat16
HIGHEST = lax.Precision.HIGHEST

EPS = 1e-6
CHUNK = 64
CONV_WIDTH = 31
C_CONV = 256
C_POOL = 256
POOL_GROUP = 64
N_HEADS = 8
HEAD_DIM = 64
D_Q_LAT = 256
D_KV_LAT = 128
IDX_HEADS = 8
IDX_DIM = 64
TOPK_MAX = 256
N_GROUPS = 4
EXPERTS_PER_GROUP = 8
N_EXPERTS = 32
D_EXPERT = 512
N_IN = 1736
N_IN_PAD = 1792
KW_OFF = 1408
POOL_OFF = 1536
LANES = 128
HALO = 32
CONV_TILE = 64
FFN_BLOCK = 256
ROW_DMA_UNROLL = 8
ATTN_BATCH = 2
ATTN_BATCH_MAX_KEYS = 1024
LOG2E = 1.4426950408889634
VMEM_LIMIT = 56 * 1024 * 1024


def _cparams(sem, row_dma=False):
    return pltpu.CompilerParams(dimension_semantics=sem, vmem_limit_bytes=VMEM_LIMIT,
                                disable_bounds_checks=row_dma)


def _mod_kernel(c_ref, w_ref, b_ref, o_ref):
    c = c_ref[...]
    cond = c * jax.nn.sigmoid(c)
    o_ref[0] = jnp.dot(cond, w_ref[0], precision=HIGHEST, preferred_element_type=F32) + b_ref[0]


def _modulation(c, mod_w, mod_b):
    L, D, D6 = mod_w.shape
    B = c.shape[0]
    nj = D6 // D
    return pl.pallas_call(
        _mod_kernel,
        grid=(L, nj),
        in_specs=[
            pl.BlockSpec((B, D), lambda l, j: (0, 0)),
            pl.BlockSpec((1, D, D), lambda l, j: (l, 0, j)),
            pl.BlockSpec((1, 1, D), lambda l, j: (l, 0, j)),
        ],
        out_specs=pl.BlockSpec((1, B, D), lambda l, j: (l, 0, j)),
        out_shape=jax.ShapeDtypeStruct((L, B, D6), F32),
        compiler_params=_cparams(("arbitrary", "arbitrary")),
        name="modulation",
    )(c, mod_w, mod_b.reshape(L, 1, D6))


def _fold_kernel(uq_ref, ukt_ref, uv_ref, wo_ref, wq_ref, wvo_ref):
    wq_ref[0] = jnp.dot(uq_ref[0, 0], ukt_ref[0, 0], precision=HIGHEST,
                        preferred_element_type=F32).astype(BF16)
    wvo_ref[0] = jnp.dot(uv_ref[0, 0], wo_ref[0, 0], precision=HIGHEST,
                         preferred_element_type=F32).astype(BF16)


def _fold_weights(w_uq, w_uk, w_uv, w_out):
    L = w_uq.shape[0]
    D = w_out.shape[-1]
    uq = jnp.transpose(w_uq, (0, 2, 1, 3))
    ukt = jnp.transpose(w_uk, (0, 2, 3, 1))
    uv = jnp.transpose(w_uv, (0, 2, 1, 3))
    wo = w_out[:, C_CONV:C_CONV + N_HEADS * HEAD_DIM, :].reshape(L, N_HEADS, HEAD_DIM, D)
    return pl.pallas_call(
        _fold_kernel,
        grid=(L, N_HEADS),
        in_specs=[
            pl.BlockSpec((1, 1, D_Q_LAT, HEAD_DIM), lambda l, h: (l, h, 0, 0)),
            pl.BlockSpec((1, 1, HEAD_DIM, D_KV_LAT), lambda l, h: (l, h, 0, 0)),
            pl.BlockSpec((1, 1, D_KV_LAT, HEAD_DIM), lambda l, h: (l, h, 0, 0)),
            pl.BlockSpec((1, 1, HEAD_DIM, D), lambda l, h: (l, h, 0, 0)),
        ],
        out_specs=[
            pl.BlockSpec((1, D_Q_LAT, D_KV_LAT), lambda l, h: (l, 0, h)),
            pl.BlockSpec((1, D_KV_LAT, D), lambda l, h: (l, h, 0)),
        ],
        out_shape=[
            jax.ShapeDtypeStruct((L, D_Q_LAT, N_HEADS * D_KV_LAT), BF16),
            jax.ShapeDtypeStruct((L, N_HEADS * D_KV_LAT, D), BF16),
        ],
        compiler_params=_cparams(("arbitrary", "arbitrary")),
        name="fold_weights",
    )(uq, ukt, uv, wo)


def _rms(v):
    return v * lax.rsqrt(jnp.mean(v * v, axis=-1, keepdims=True) + EPS)


def _in_kernel(x_ref, sc_ref, sh_ref, g_ref, w_ref, qg_ref, kvg_ref, wq_ref,
               cu_ref, qabs_ref, ckv_ref, qi_ref, kw_ref):
    x = x_ref[0]
    h = _rms(x) * g_ref[...] * (1.0 + sc_ref[0]) + sh_ref[0]
    z = jnp.dot(h.astype(BF16), w_ref[...], preferred_element_type=F32)
    a_val = z[:, 0:C_CONV]
    a_gate = z[:, C_CONV:2 * C_CONV]
    cu_ref[0, :, 0:C_CONV] = a_val * jax.nn.sigmoid(a_gate)
    cu_ref[0, :, C_CONV:C_CONV + C_POOL] = z[:, POOL_OFF:POOL_OFF + C_POOL]
    o = 2 * C_CONV
    cq = _rms(z[:, o:o + D_Q_LAT]) * qg_ref[...]
    qabs = jnp.dot(cq.astype(BF16), wq_ref[...], preferred_element_type=F32)
    qabs_ref[0] = (qabs * (HEAD_DIM ** -0.5 * LOG2E)).astype(BF16)
    o += D_Q_LAT
    ckv_ref[0] = (_rms(z[:, o:o + D_KV_LAT]) * kvg_ref[...]).astype(BF16)
    o += D_KV_LAT
    qi_ref[0] = z[:, o:o + IDX_HEADS * IDX_DIM].astype(BF16)
    kw_ref[0] = z[:, KW_OFF:KW_OFF + LANES]


def _in_proj(x, sc, sh, g, w_in_p, qg, kvg, wq, tm):
    B, S, D = x.shape
    tok = lambda n: pl.BlockSpec((1, tm, n), lambda b, i: (b, i, 0))
    per_b = pl.BlockSpec((1, 1, D), lambda b, i: (b, 0, 0))
    full = lambda a: pl.BlockSpec(a.shape, lambda b, i: (0,) * a.ndim)
    nq = IDX_HEADS * IDX_DIM
    return pl.pallas_call(
        _in_kernel,
        grid=(B, S // tm),
        in_specs=[tok(D), per_b, per_b, full(g), full(w_in_p), full(qg), full(kvg), full(wq)],
        out_specs=[tok(C_CONV + C_POOL), tok(N_HEADS * D_KV_LAT), tok(D_KV_LAT), tok(nq), tok(LANES)],
        out_shape=[
            jax.ShapeDtypeStruct((B, S, C_CONV + C_POOL), F32),
            jax.ShapeDtypeStruct((B, S, N_HEADS * D_KV_LAT), BF16),
            jax.ShapeDtypeStruct((B, S, D_KV_LAT), BF16),
            jax.ShapeDtypeStruct((B, S, nq), BF16),
            jax.ShapeDtypeStruct((B, S, LANES), F32),
        ],
        compiler_params=_cparams(("arbitrary", "arbitrary")),
        name="in_proj",
    )(x, sc, sh, g, w_in_p, qg, kvg, wq)


def _cp_kernel(cu_ref, ck_ref, cb_ref, lg_ref, lb_ref, pw_ref, ps_ref, o_ref, pad_ref):
    S = cu_ref.shape[1]
    TT = CONV_TILE
    pad_ref[0:HALO, :] = jnp.zeros((HALO, C_CONV + C_POOL), F32)
    pad_ref[HALO:HALO + S, :] = cu_ref[0]
    lane = lax.broadcasted_iota(jnp.int32, (TT, C_POOL), 1)
    row = lax.broadcasted_iota(jnp.int32, (TT, C_POOL), 0)
    win_len = jnp.where(lane < POOL_GROUP, 2,
                        jnp.where(lane < 2 * POOL_GROUP, 4, jnp.where(lane < 3 * POOL_GROUP, 8, 16)))

    def tile(i, carry):
        t0 = pl.multiple_of(i * TT, TT)
        win = pad_ref[pl.ds(t0, TT + HALO), :]
        wc = win[:, 0:C_CONV]
        wp = win[:, C_CONV:C_CONV + C_POOL]
        acc = jnp.zeros((TT, C_CONV), F32)
        for s in range(8):
            ws = wc if s == 0 else pltpu.roll(wc, TT + HALO - s, 0)
            for j in range(CONV_WIDTH):
                o = HALO - (CONV_WIDTH - 1) + j
                if o % 8 == s:
                    acc = acc + ck_ref[j:j + 1, :] * ws[o - s:o - s + TT, :]
        v = acc + cb_ref[...]
        mu = jnp.mean(v, axis=-1, keepdims=True)
        vc = v - mu
        var = jnp.mean(vc * vc, axis=-1, keepdims=True)
        yn = vc * lax.rsqrt(var + EPS) * lg_ref[...] + lb_ref[...]
        o_ref[0, pl.ds(t0, TT), 0:C_CONV] = (yn * jax.nn.sigmoid(yn)).astype(BF16)
        lo = HALO - 15
        s2 = wp[lo + 1:, :] + wp[lo:-1, :]
        s4 = s2[2:, :] + s2[:-2, :]
        s8 = s4[4:, :] + s4[:-4, :]
        s16 = s8[8:, :] + s8[:-8, :]
        u = wp[HALO:, :]
        sw = jnp.where(lane < POOL_GROUP, s2[14:, :],
                       jnp.where(lane < 2 * POOL_GROUP, s4[12:, :],
                                 jnp.where(lane < 3 * POOL_GROUP, s8[8:, :], s16)))
        n_pos = jnp.minimum(t0 + row + 1, win_len).astype(F32)
        d = (sw / n_pos - u).astype(BF16)
        yp = jnp.dot(d, pw_ref[...], preferred_element_type=F32) * ps_ref[...]
        o_ref[0, pl.ds(t0, TT), C_CONV:C_CONV + C_POOL] = yp.astype(BF16)
        return carry

    lax.fori_loop(0, S // TT, tile, 0)


def _conv_pool(cu, ck, cb, lg, lb, pw_bd, ps):
    B, S, W = cu.shape
    full = lambda a: pl.BlockSpec(a.shape, lambda b: (0,) * a.ndim)
    return pl.pallas_call(
        _cp_kernel,
        grid=(B,),
        in_specs=[pl.BlockSpec((1, S, W), lambda b: (b, 0, 0)),
                  full(ck), full(cb), full(lg), full(lb), full(pw_bd), full(ps)],
        out_specs=pl.BlockSpec((1, S, W), lambda b: (b, 0, 0)),
        out_shape=jax.ShapeDtypeStruct((B, S, W), BF16),
        scratch_shapes=[pltpu.VMEM((S + HALO, W), F32)],
        compiler_params=_cparams(("arbitrary",)),
        name="conv_pool",
    )(cu, ck, cb, lg, lb, pw_bd, ps)


def _attn_kernel(qi_ref, kwq_ref, kwk_ref, qabs_ref, ckv_ref, o_ref, key_ref, aux_ref, bias_ref, *, topk, q0):
    G, QB = qi_ref.shape[0], qi_ref.shape[1]
    LK = ckv_ref.shape[1]
    R = G * QB
    dn_nt = (((1,), (1,)), ((), ()))

    def visible(rows):
        kidx = lax.broadcasted_iota(jnp.int32, (rows, LK), 1)
        qpos = q0 + lax.broadcasted_iota(jnp.int32, (rows, 1), 0) % QB
        return kidx, kidx < (qpos // CHUNK + 1) * CHUNK

    if LK > topk:
        _, allowed = visible(QB)
        for g in range(G):
            ki = kwk_ref[g, :, 0:IDX_DIM].astype(BF16)
            wi = kwq_ref[g, :, IDX_DIM:IDX_DIM + IDX_HEADS]
            qi = qi_ref[g]
            score = jnp.zeros((QB, LK), F32)
            for h in range(IDX_HEADS):
                rel = lax.dot_general(qi[:, h * IDX_DIM:(h + 1) * IDX_DIM], ki, dn_nt,
                                      preferred_element_type=F32)
                score = score + jnp.maximum(rel, 0.0) * wi[:, h:h + 1]
            score = score * ((IDX_HEADS * IDX_DIM) ** -0.5)
            score = jnp.where(score == 0.0, 0.0, score)
            score = jnp.where(allowed, score, -jnp.inf)
            bits = lax.bitcast_convert_type(score, jnp.int32)
            key_ref[g * QB:(g + 1) * QB, :] = bits ^ ((bits >> 31) & jnp.int32(0x7FFFFFFF))

        kf = jnp.float32(topk)

        H = R // 2

        def partial_counts(r0, cand):
            acc = jnp.zeros((H, LANES), F32)
            for c in range(LK // LANES):
                k = key_ref[r0:r0 + H, c * LANES:(c + 1) * LANES]
                acc = acc + jnp.where(k >= cand, 1.0, 0.0)
            return acc

        def settle(acc, cand, pre):
            return jnp.where(jnp.sum(acc, axis=-1, keepdims=True) >= kf, cand, pre)

        def bit(i):
            return jnp.where(i <= 31, jnp.int32(1) << jnp.maximum(31 - i, 0), 0)

        def bit_step(i, carry):
            pre_a, pre_b, acc_b = carry
            cand_a = pre_a + bit(i)
            acc_a = partial_counts(0, cand_a)
            pre_b = settle(acc_b, pre_b + bit(i), pre_b)
            acc_b = partial_counts(H, pre_b + bit(i + 1))
            pre_a = settle(acc_a, cand_a, pre_a)
            return pre_a, pre_b, acc_b

        lowest = jnp.full((H, 1), jnp.iinfo(jnp.int32).min, jnp.int32)
        pre_a, pre_b, _ = lax.fori_loop(0, 32, bit_step,
                                        (lowest, lowest, partial_counts(H, lowest + bit(0))))
        thr = jnp.concatenate([pre_a, pre_b], axis=0)
        reached = key_ref[...] >= thr
        bias_ref[...] = jnp.where(reached, 0.0, -jnp.inf)
        has_excess = jnp.max(jnp.sum(jnp.where(reached, 1.0, 0.0), axis=-1, keepdims=True)) > kf

        @pl.when(has_excess)
        def _():
            kidx, allowed_r = visible(R)
            key = key_ref[...]
            n_gt = jnp.sum(jnp.where(key > thr, 1.0, 0.0), axis=-1, keepdims=True)
            need = kf - n_gt
            aux_ref[...] = jnp.where(key == thr, kidx, jnp.int32(LK))
            nbits = (LK - 1).bit_length()

            def idx_step(i, p):
                cand = p | (jnp.int32(1) << (nbits - 1 - i))
                cnt = jnp.sum(jnp.where(aux_ref[...] < cand, 1.0, 0.0), axis=-1, keepdims=True)
                return jnp.where(cnt < need, cand, p)

            cut = lax.fori_loop(0, nbits, idx_step, jnp.zeros((R, 1), jnp.int32))
            sel = ((key_ref[...] > thr) | (aux_ref[...] <= cut)) & allowed_r
            bias_ref[...] = jnp.where(sel, 0.0, -jnp.inf)

    for g in range(G):
        if LK > topk:
            bias = bias_ref[g * QB:(g + 1) * QB, :]
        else:
            bias = jnp.where(visible(QB)[1], 0.0, -jnp.inf)
        ckv = ckv_ref[g]
        for h in range(N_HEADS):
            qa = qabs_ref[g, :, h * D_KV_LAT:(h + 1) * D_KV_LAT]
            logit = lax.dot_general(qa, ckv, dn_nt, preferred_element_type=F32) + bias
            m = jnp.max(logit, axis=-1, keepdims=True)
            p = jnp.exp2(logit - m)
            l = jnp.sum(p, axis=-1, keepdims=True)
            ctx = jnp.dot(p.astype(BF16), ckv, preferred_element_type=F32)
            o_ref[g, :, h * D_KV_LAT:(h + 1) * D_KV_LAT] = (ctx / l).astype(BF16)


def _attention(qi, kw, qabs, ckv, qb):
    B, S, _ = ckv.shape
    topk = min(TOPK_MAX, S // 4)
    buf = qabs
    for j in range(S // qb):
        lk = (j + 1) * qb
        G = ATTN_BATCH if (B % ATTN_BATCH == 0 and lk <= ATTN_BATCH_MAX_KEYS) else 1
        blk = lambda n, j=j: pl.BlockSpec((G, qb, n), lambda b: (b, j, 0))
        keys = lambda n, lk=lk: pl.BlockSpec((G, lk, n), lambda b: (b, 0, 0))
        buf = pl.pallas_call(
            functools.partial(_attn_kernel, topk=topk, q0=j * qb),
            grid=(B // G,),
            in_specs=[blk(IDX_HEADS * IDX_DIM), blk(LANES), keys(LANES), blk(N_HEADS * D_KV_LAT),
                      keys(D_KV_LAT)],
            out_specs=blk(N_HEADS * D_KV_LAT),
            out_shape=jax.ShapeDtypeStruct(buf.shape, BF16),
            scratch_shapes=[pltpu.VMEM((G * qb, lk), jnp.int32), pltpu.VMEM((G * qb, lk), jnp.int32),
                            pltpu.VMEM((G * qb, lk), F32)],
            input_output_aliases={3: 0},
            compiler_params=_cparams(("arbitrary",)),
            name=f"dsa_attention_q{j}",
        )(qi, kw, kw, buf, ckv)
    return buf


def _out_kernel(ycp_ref, ctx_ref, x_ref, g1_ref, sc_ref, sh_ref, ng_ref, wcp_ref, wvo_ref,
                rw_ref, rb_ref, tri_ref, x1_ref, h2_ref, route_ref, cnt_ref, carry_ref):
    first = (pl.program_id(0) == 0) & (pl.program_id(1) == 0)

    @pl.when(first)
    def _():
        carry_ref[...] = jnp.zeros_like(carry_ref)

    mix = jnp.dot(ycp_ref[0], wcp_ref[...], preferred_element_type=F32)
    mix = mix + jnp.dot(ctx_ref[0], wvo_ref[...], preferred_element_type=F32)
    x1 = x_ref[0] + g1_ref[0] * mix
    x1_ref[0] = x1
    h2 = _rms(x1) * ng_ref[...] * (1.0 + sc_ref[0]) + sh_ref[0]
    h2_ref[0] = h2
    rw = rw_ref[...]
    rw_hi = rw.astype(BF16)
    rw_lo = (rw - rw_hi.astype(F32)).astype(BF16)
    h_hi = h2.astype(BF16)
    h_lo = (h2 - h_hi.astype(F32)).astype(BF16)
    logits = (jnp.dot(h_hi, rw_hi, preferred_element_type=F32)
              + (jnp.dot(h_lo, rw_hi, preferred_element_type=F32)
                 + jnp.dot(h_hi, rw_lo, preferred_element_type=F32))) + rb_ref[...]

    lt = logits.T
    tm = lt.shape[1]
    rowi = lax.broadcasted_iota(jnp.int32, (LANES, tm), 0)
    rowf = rowi.astype(F32)
    ninf = -jnp.inf
    big = jnp.float32(LANES)
    glog = jnp.where(rowi < N_GROUPS, lt, ninf)
    gmax = jnp.max(glog, axis=0, keepdims=True)
    gidx = jnp.min(jnp.where(glog == gmax, rowf, big), axis=0, keepdims=True)
    g_p = 1.0 / jnp.sum(jnp.exp(glog - gmax), axis=0, keepdims=True)
    lo = N_GROUPS + EXPERTS_PER_GROUP * gidx
    elog = jnp.where((rowf >= lo) & (rowf < lo + EXPERTS_PER_GROUP), lt, ninf)
    m1 = jnp.max(elog, axis=0, keepdims=True)
    i1 = jnp.min(jnp.where(elog == m1, rowf, big), axis=0, keepdims=True)
    elog2 = jnp.where(rowf == i1, ninf, elog)
    m2 = jnp.max(elog2, axis=0, keepdims=True)
    i2 = jnp.min(jnp.where(elog2 == m2, rowf, big), axis=0, keepdims=True)
    r = jnp.exp(m2 - m1)
    gate1 = g_p / (1.0 + r)
    gate2 = g_p * r / (1.0 + r)
    e1 = i1 - N_GROUPS
    e2 = i2 - N_GROUPS
    oh1 = rowf == e1
    oh2 = rowf == e2
    oh1f = jnp.where(oh1, 1.0, 0.0)
    oh2f = jnp.where(oh2, 1.0, 0.0)
    pre1 = jnp.dot(oh1f.astype(BF16), tri_ref[...], preferred_element_type=F32)
    pre2 = jnp.dot(oh2f.astype(BF16), tri_ref[...], preferred_element_type=F32)
    carry = carry_ref[:, 0:1]
    cnt1 = jnp.sum(oh1f, axis=1, keepdims=True)
    cnt2 = jnp.sum(oh2f, axis=1, keepdims=True)
    rank1 = jnp.sum(jnp.where(oh1, carry + pre1, 0.0), axis=0, keepdims=True)
    rank2 = jnp.sum(jnp.where(oh2, carry + cnt1 + pre2, 0.0), axis=0, keepdims=True)
    total = carry + cnt1 + cnt2
    carry_ref[...] = jnp.broadcast_to(total, carry_ref.shape)
    cnt_ref[...] = jnp.broadcast_to(total, cnt_ref.shape)
    route_t = jnp.where(rowi == 0, e1, jnp.where(rowi == 1, e2, jnp.where(rowi == 2, gate1, jnp.where(
        rowi == 3, gate2, jnp.where(rowi == 4, rank1, jnp.where(rowi == 5, rank2, 0.0))))))
    route_ref[0] = route_t.T


def _out_proj(ycp, ctx, x, g1, sc2, sh2, ng, wcp, wvo, rw, rb, tm):
    B, S, D = x.shape
    tok = lambda n: pl.BlockSpec((1, tm, n), lambda b, i: (b, i, 0))
    per_b = pl.BlockSpec((1, 1, D), lambda b, i: (b, 0, 0))
    full = lambda a: pl.BlockSpec(a.shape, lambda b, i: (0,) * a.ndim)
    tri = jnp.triu(jnp.ones((tm, tm), BF16), 1)
    return pl.pallas_call(
        _out_kernel,
        grid=(B, S // tm),
        in_specs=[tok(C_CONV + C_POOL), tok(N_HEADS * D_KV_LAT), tok(D), per_b, per_b, per_b,
                  full(ng), full(wcp), full(wvo), full(rw), full(rb), full(tri)],
        out_specs=[tok(D), tok(D), tok(LANES), pl.BlockSpec((LANES, LANES), lambda b, i: (0, 0))],
        out_shape=[
            jax.ShapeDtypeStruct((B, S, D), F32),
            jax.ShapeDtypeStruct((B, S, D), F32),
            jax.ShapeDtypeStruct((B, S, LANES), F32),
            jax.ShapeDtypeStruct((LANES, LANES), F32),
        ],
        scratch_shapes=[pltpu.VMEM((LANES, LANES), F32)],
        compiler_params=_cparams(("arbitrary", "arbitrary")),
        name="out_proj_router",
    )(ycp, ctx, x, g1, sc2, sh2, ng, wcp, wvo, rw, rb, tri)


def _row_copy(src, i, dst, k, sem):
    return pltpu.make_async_copy(src.at[pl.ds(i, 1), :], dst.at[pl.ds(k, 1), :], sem)


DISPATCH_SLOTS = 3


def _dispatch_kernel(dst_ref, dst_prev_ref, h_ref, xs_in_ref, xs_ref, hbuf, load_sem, sem):
    del xs_in_ref
    tm = hbuf.shape[1]
    i = pl.program_id(0)
    n = pl.num_programs(0)
    slot = i % DISPATCH_SLOTS

    def load(t, s):
        return pltpu.make_async_copy(h_ref.at[pl.ds(pl.multiple_of(t * tm, tm), tm), :], hbuf.at[s],
                                     load_sem.at[s])

    @pl.when(i == 0)
    def _():
        load(0, 0).start()

    @pl.when(i + 1 < n)
    def _():
        load(i + 1, (i + 1) % DISPATCH_SLOTS).start()

    load(i, slot).wait()

    def copies(idx_ref, s, r):
        return (_row_copy(hbuf.at[s], r, xs_ref, idx_ref[2 * r], sem.at[s]),
                _row_copy(hbuf.at[s], r, xs_ref, idx_ref[2 * r + 1], sem.at[s]))

    def issue(r, c):
        for cp in copies(dst_ref, slot, r):
            cp.start()
        return c

    lax.fori_loop(0, tm, issue, 0, unroll=ROW_DMA_UNROLL)

    def drain(idx_ref, s):
        def body(r, c):
            for cp in copies(idx_ref, s, r):
                cp.wait()
            return c
        lax.fori_loop(0, tm, body, 0, unroll=ROW_DMA_UNROLL)

    @pl.when(i >= 1)
    def _():
        drain(dst_prev_ref, (i + DISPATCH_SLOTS - 1) % DISPATCH_SLOTS)

    @pl.when(i == n - 1)
    def _():
        drain(dst_ref, slot)


def _dispatch(dst_flat, h2, xs0, tm):
    T, D = h2.shape
    n_rows = xs0.shape[0]
    return pl.pallas_call(
        _dispatch_kernel,
        grid=(T // tm,),
        in_specs=[pl.BlockSpec((2 * tm,), lambda i: (i,), memory_space=pltpu.SMEM),
                  pl.BlockSpec((2 * tm,), lambda i: (jnp.maximum(i - 1, 0),), memory_space=pltpu.SMEM),
                  pl.BlockSpec(memory_space=pl.ANY),
                  pl.BlockSpec(memory_space=pl.ANY)],
        out_specs=pl.BlockSpec(memory_space=pl.ANY),
        out_shape=jax.ShapeDtypeStruct((n_rows, D), F32),
        scratch_shapes=[pltpu.VMEM((DISPATCH_SLOTS, tm, D), F32),
                        pltpu.SemaphoreType.DMA((DISPATCH_SLOTS,)),
                        pltpu.SemaphoreType.DMA((DISPATCH_SLOTS,))],
        input_output_aliases={3: 0},
        compiler_params=_cparams(("arbitrary",), row_dma=True),
        name="moe_dispatch",
    )(dst_flat, dst_flat, h2, xs0)


def _ffn_kernel(be_ref, nu_ref, xs_ref, w1_ref, w3_ref, w2_ref, y_ref, w13_scr, w2_scr):
    i = pl.program_id(0)
    new_expert = (i == 0) | (be_ref[i] != be_ref[jnp.maximum(i - 1, 0)])

    @pl.when(new_expert)
    def _():
        w13_scr[:, 0:D_EXPERT] = w1_ref[0, 0].astype(BF16)
        w13_scr[:, D_EXPERT:2 * D_EXPERT] = w3_ref[0, 0].astype(BF16)
        w2_scr[...] = w2_ref[0, 0].astype(BF16)

    @pl.when(i < nu_ref[0])
    def _():
        xb = xs_ref[...].astype(BF16)
        h13 = jnp.dot(xb, w13_scr[...], preferred_element_type=F32)
        h1 = h13[:, 0:D_EXPERT]
        h3 = h13[:, D_EXPERT:2 * D_EXPERT]
        act = (h1 * jax.nn.sigmoid(h1)) * h3
        y_ref[...] = jnp.dot(act.astype(BF16), w2_scr[...], preferred_element_type=F32)

    @pl.when(i >= nu_ref[0])
    def _():
        y_ref[...] = jnp.zeros_like(y_ref)


def _expert_ffn(blk_exp, n_used, xs, w1, w3, w2, layer):
    P, D = xs.shape
    nblk = P // FFN_BLOCK
    grid_spec = pltpu.PrefetchScalarGridSpec(
        num_scalar_prefetch=2,
        grid=(nblk,),
        in_specs=[
            pl.BlockSpec((FFN_BLOCK, D), lambda i, be, nu: (jnp.maximum(jnp.minimum(i, nu[0] - 1), 0), 0)),
            pl.BlockSpec((1, 1, D, D_EXPERT), lambda i, be, nu: (layer, be[i], 0, 0)),
            pl.BlockSpec((1, 1, D, D_EXPERT), lambda i, be, nu: (layer, be[i], 0, 0)),
            pl.BlockSpec((1, 1, D_EXPERT, D), lambda i, be, nu: (layer, be[i], 0, 0)),
        ],
        out_specs=pl.BlockSpec((FFN_BLOCK, D), lambda i, be, nu: (i, 0)),
        scratch_shapes=[pltpu.VMEM((D, 2 * D_EXPERT), BF16), pltpu.VMEM((D_EXPERT, D), BF16)],
    )
    return pl.pallas_call(
        _ffn_kernel,
        grid_spec=grid_spec,
        out_shape=jax.ShapeDtypeStruct((P, D), F32),
        compiler_params=_cparams(("arbitrary",)),
        name="expert_ffn",
    )(blk_exp, n_used, xs, w1, w3, w2)


def _combine_kernel(dst_ref, dst_next_ref, x1_ref, route_ref, g2_ref, fg_ref, y_ref, o_ref, ya_ref, yb_ref, sem,
                    *, final_norm):
    tm = x1_ref.shape[1]
    step = pl.program_id(0) * pl.num_programs(1) + pl.program_id(1)
    n_steps = pl.num_programs(0) * pl.num_programs(1)
    slot = step % 2

    def copies(idx_ref, s, r):
        return (_row_copy(y_ref, idx_ref[2 * r], ya_ref.at[s], r, sem.at[0, s]),
                _row_copy(y_ref, idx_ref[2 * r + 1], yb_ref.at[s], r, sem.at[1, s]))

    def gather(idx_ref, s):
        def issue(r, c):
            for cp in copies(idx_ref, s, r):
                cp.start()
            return c
        lax.fori_loop(0, tm, issue, 0, unroll=ROW_DMA_UNROLL)

    @pl.when(step == 0)
    def _():
        gather(dst_ref, 0)

    @pl.when(step + 1 < n_steps)
    def _():
        gather(dst_next_ref, 1 - slot)

    def drain(r, c):
        for cp in copies(dst_ref, slot, r):
            cp.wait()
        return c

    lax.fori_loop(0, tm, drain, 0, unroll=ROW_DMA_UNROLL)

    route = route_ref[0]
    moe = ya_ref[slot] * route[:, 2:3] + yb_ref[slot] * route[:, 3:4]
    out = x1_ref[0] + g2_ref[0] * moe
    if final_norm:
        out = _rms(out) * fg_ref[...]
    o_ref[0] = out


def _combine(dst_flat, x1, route, g2, fg, y, tm, final_norm):
    B, S, D = x1.shape
    nt = S // tm
    tok = lambda n: pl.BlockSpec((1, tm, n), lambda b, i: (b, i, 0))
    return pl.pallas_call(
        functools.partial(_combine_kernel, final_norm=final_norm),
        grid=(B, nt),
        in_specs=[pl.BlockSpec((2 * tm,), lambda b, i: (b * nt + i,), memory_space=pltpu.SMEM),
                  pl.BlockSpec((2 * tm,), lambda b, i: (jnp.minimum(b * nt + i + 1, B * nt - 1),),
                               memory_space=pltpu.SMEM),
                  tok(D), tok(LANES),
                  pl.BlockSpec((1, 1, D), lambda b, i: (b, 0, 0)),
                  pl.BlockSpec((1, D), lambda b, i: (0, 0)),
                  pl.BlockSpec(memory_space=pl.ANY)],
        out_specs=tok(D),
        out_shape=jax.ShapeDtypeStruct((B, S, D), F32),
        scratch_shapes=[pltpu.VMEM((2, tm, D), F32), pltpu.VMEM((2, tm, D), F32),
                        pltpu.SemaphoreType.DMA((2, 2))],
        compiler_params=_cparams(("arbitrary", "arbitrary"), row_dma=True),
        name="moe_combine",
    )(dst_flat, dst_flat, x1, route, g2, fg, y)


def _pick_tile(n, pref):
    t = min(n, pref)
    while n % t:
        t //= 2
    return t


def kernel(x, c, mod_w, mod_b, norm1_g, w_in, conv_k, conv_b, conv_ln_g, conv_ln_b, q_norm_g, kv_norm_g, w_uq, w_uk, w_uv, pool_w, pool_scale, w_out, norm2_g, router_g_w, router_g_b, router_e_w, router_e_b, exp_w1, exp_w3, exp_w2, final_g):
    B, S, D = x.shape
    L = mod_w.shape[0]
    T = B * S
    tm = _pick_tile(S, 512)
    qb = _pick_tile(S, 256)
    tmc = _pick_tile(S, 512)
    n_rows = 2 * T + N_EXPERTS * FFN_BLOCK
    nblk = n_rows // FFN_BLOCK

    mod = _modulation(c, mod_w, mod_b)
    wq_all, wvo_all = _fold_weights(w_uq, w_uk, w_uv, w_out)
    row = lambda a: a.reshape(1, -1)

    slot_buf = jnp.zeros((n_rows, D), F32)
    for l in range(L):
        sh1, sc1, g1, sh2, sc2, g2 = [mod[l, :, k * D:(k + 1) * D].reshape(B, 1, D) for k in range(6)]
        w_in_p = jnp.concatenate(
            [w_in[l, :, :N_IN - C_POOL], jnp.zeros((D, N_IN_PAD - N_IN), F32), w_in[l, :, N_IN - C_POOL:]],
            axis=1).astype(BF16)
        cu, qabs, ckv, qi, kw = _in_proj(x, sc1, sh1, row(norm1_g[l]), w_in_p, row(q_norm_g[l]),
                                         row(kv_norm_g[l]), wq_all[l], tm)
        pw_bd = jax.scipy.linalg.block_diag(*[pool_w[l, g] for g in range(pool_w.shape[1])]).astype(BF16)
        ycp = _conv_pool(cu, conv_k[l], row(conv_b[l]), row(conv_ln_g[l]), row(conv_ln_b[l]),
                         pw_bd, row(pool_scale[l]))
        ctx = _attention(qi, kw, qabs, ckv, qb)
        wcp = jnp.concatenate([w_out[l, :C_CONV], w_out[l, D - C_POOL:]], axis=0).astype(BF16)
        n_r = N_GROUPS + N_EXPERTS
        rw = jnp.concatenate([router_g_w[l], router_e_w[l].reshape(D, N_EXPERTS),
                              jnp.zeros((D, LANES - n_r), F32)], axis=1)
        rb = jnp.concatenate([router_g_b[l], router_e_b[l].reshape(N_EXPERTS),
                              jnp.zeros((LANES - n_r,), F32)]).reshape(1, LANES)
        x1, h2, route, cnt = _out_proj(ycp, ctx, x, g1, sc2, sh2, row(norm2_g[l]), wcp, wvo_all[l],
                                       rw, rb, tm)
        counts = cnt[:N_EXPERTS, 0].astype(jnp.int32)
        pcounts = (counts + FFN_BLOCK - 1) // FFN_BLOCK * FFN_BLOCK
        pends = jnp.cumsum(pcounts)
        pstarts = pends - pcounts
        route2 = route.reshape(T, LANES)
        e_idx = route2[:, 0:2].astype(jnp.int32)
        e_start = jnp.sum(jnp.where(e_idx[..., None] == jnp.arange(N_EXPERTS, dtype=jnp.int32), pstarts, 0),
                          axis=-1)
        dst = (e_start + route2[:, 4:6].astype(jnp.int32)).reshape(2 * T)
        n_used = (pends[-1] // FFN_BLOCK).astype(jnp.int32).reshape(1)
        blk_start = jnp.arange(nblk, dtype=jnp.int32) * FFN_BLOCK
        blk_exp = jnp.minimum(jnp.sum((pends[None, :] <= blk_start[:, None]).astype(jnp.int32), axis=1),
                              N_EXPERTS - 1)
        xs = _dispatch(dst, h2.reshape(T, D), slot_buf, tmc)
        y = _expert_ffn(blk_exp, n_used, xs, exp_w1, exp_w3, exp_w2, l)
        slot_buf = y
        x = _combine(dst, x1, route, g2, row(final_g), y, tmc, final_norm=(l == L - 1))
    return x
```

```python
import functools

import jax
import jax.numpy as jnp
from jax import lax
from jax.experimental import pallas as pl
from jax.experimental.pallas import tpu as pltpu

F32 = jnp.float32
BF16 = jnp.bfloat16
HIGHEST = lax.Precision.HIGHEST

EPS = 1e-6
CHUNK = 64
CONV_WIDTH = 31
C_CONV = 256
C_POOL = 256
POOL_GROUP = 64
N_HEADS = 8
HEAD_DIM = 64
D_Q_LAT = 256
D_KV_LAT = 128
IDX_HEADS = 8
IDX_DIM = 64
TOPK_MAX = 256
N_GROUPS = 4
EXPERTS_PER_GROUP = 8
N_EXPERTS = 32
D_EXPERT = 512
N_IN = 1736
N_IN_PAD = 1792
KW_OFF = 1408
POOL_OFF = 1536
LANES = 128
HALO = 32
CONV_TILE = 64
FFN_BLOCK = 256
ROW_DMA_UNROLL = 8
ROW_TILE = 8
ATTN_BATCH = 2
ATTN_BATCH_MAX_KEYS = 1024
LOG2E = 1.4426950408889634
VMEM_LIMIT = 56 * 1024 * 1024


def _cparams(sem, row_dma=False):
    return pltpu.CompilerParams(dimension_semantics=sem, vmem_limit_bytes=VMEM_LIMIT,
                                disable_bounds_checks=row_dma)


def _mod_kernel(c_ref, w_ref, b_ref, o_ref):
    c = c_ref[...]
    cond = c * jax.nn.sigmoid(c)
    o_ref[0] = jnp.dot(cond, w_ref[0], precision=HIGHEST, preferred_element_type=F32) + b_ref[0]


def _modulation(c, mod_w, mod_b):
    L, D, D6 = mod_w.shape
    B = c.shape[0]
    nj = D6 // D
    return pl.pallas_call(
        _mod_kernel,
        grid=(L, nj),
        in_specs=[
            pl.BlockSpec((B, D), lambda l, j: (0, 0)),
            pl.BlockSpec((1, D, D), lambda l, j: (l, 0, j)),
            pl.BlockSpec((1, 1, D), lambda l, j: (l, 0, j)),
        ],
        out_specs=pl.BlockSpec((1, B, D), lambda l, j: (l, 0, j)),
        out_shape=jax.ShapeDtypeStruct((L, B, D6), F32),
        compiler_params=_cparams(("arbitrary", "arbitrary")),
        name="modulation",
    )(c, mod_w, mod_b.reshape(L, 1, D6))


def _fold_kernel(uq_ref, ukt_ref, uv_ref, wo_ref, wq_ref, wvo_ref):
    wq_ref[0] = jnp.dot(uq_ref[0, 0], ukt_ref[0, 0], precision=HIGHEST,
                        preferred_element_type=F32).astype(BF16)
    wvo_ref[0] = jnp.dot(uv_ref[0, 0], wo_ref[0, 0], precision=HIGHEST,
                         preferred_element_type=F32).astype(BF16)


def _fold_weights(w_uq, w_uk, w_uv, w_out):
    L = w_uq.shape[0]
    D = w_out.shape[-1]
    uq = jnp.transpose(w_uq, (0, 2, 1, 3))
    ukt = jnp.transpose(w_uk, (0, 2, 3, 1))
    uv = jnp.transpose(w_uv, (0, 2, 1, 3))
    wo = w_out[:, C_CONV:C_CONV + N_HEADS * HEAD_DIM, :].reshape(L, N_HEADS, HEAD_DIM, D)
    return pl.pallas_call(
        _fold_kernel,
        grid=(L, N_HEADS),
        in_specs=[
            pl.BlockSpec((1, 1, D_Q_LAT, HEAD_DIM), lambda l, h: (l, h, 0, 0)),
            pl.BlockSpec((1, 1, HEAD_DIM, D_KV_LAT), lambda l, h: (l, h, 0, 0)),
            pl.BlockSpec((1, 1, D_KV_LAT, HEAD_DIM), lambda l, h: (l, h, 0, 0)),
            pl.BlockSpec((1, 1, HEAD_DIM, D), lambda l, h: (l, h, 0, 0)),
        ],
        out_specs=[
            pl.BlockSpec((1, D_Q_LAT, D_KV_LAT), lambda l, h: (l, 0, h)),
            pl.BlockSpec((1, D_KV_LAT, D), lambda l, h: (l, h, 0)),
        ],
        out_shape=[
            jax.ShapeDtypeStruct((L, D_Q_LAT, N_HEADS * D_KV_LAT), BF16),
            jax.ShapeDtypeStruct((L, N_HEADS * D_KV_LAT, D), BF16),
        ],
        compiler_params=_cparams(("arbitrary", "arbitrary")),
        name="fold_weights",
    )(uq, ukt, uv, wo)


def _rms(v):
    return v * lax.rsqrt(jnp.mean(v * v, axis=-1, keepdims=True) + EPS)


def _in_kernel(x_ref, sc_ref, sh_ref, g_ref, w_ref, qg_ref, kvg_ref, wq_ref,
               cu_ref, qabs_ref, ckv_ref, qi_ref, kw_ref):
    x = x_ref[0]
    h = _rms(x) * g_ref[...] * (1.0 + sc_ref[0]) + sh_ref[0]
    z = jnp.dot(h.astype(BF16), w_ref[...], preferred_element_type=F32)
    a_val = z[:, 0:C_CONV]
    a_gate = z[:, C_CONV:2 * C_CONV]
    cu_ref[0, :, 0:C_CONV] = a_val * jax.nn.sigmoid(a_gate)
    cu_ref[0, :, C_CONV:C_CONV + C_POOL] = z[:, POOL_OFF:POOL_OFF + C_POOL]
    o = 2 * C_CONV
    cq = _rms(z[:, o:o + D_Q_LAT]) * qg_ref[...]
    qabs = jnp.dot(cq.astype(BF16), wq_ref[...], preferred_element_type=F32)
    qabs_ref[0] = (qabs * (HEAD_DIM ** -0.5 * LOG2E)).astype(BF16)
    o += D_Q_LAT
    ckv_ref[0] = (_rms(z[:, o:o + D_KV_LAT]) * kvg_ref[...]).astype(BF16)
    o += D_KV_LAT
    qi_ref[0] = z[:, o:o + IDX_HEADS * IDX_DIM].astype(BF16)
    kw_ref[0] = z[:, KW_OFF:KW_OFF + LANES]


def _in_proj(x, sc, sh, g, w_in_p, qg, kvg, wq, tm):
    B, S, D = x.shape
    tok = lambda n: pl.BlockSpec((1, tm, n), lambda b, i: (b, i, 0))
    per_b = pl.BlockSpec((1, 1, D), lambda b, i: (b, 0, 0))
    full = lambda a: pl.BlockSpec(a.shape, lambda b, i: (0,) * a.ndim)
    nq = IDX_HEADS * IDX_DIM
    return pl.pallas_call(
        _in_kernel,
        grid=(B, S // tm),
        in_specs=[tok(D), per_b, per_b, full(g), full(w_in_p), full(qg), full(kvg), full(wq)],
        out_specs=[tok(C_CONV + C_POOL), tok(N_HEADS * D_KV_LAT), tok(D_KV_LAT), tok(nq), tok(LANES)],
        out_shape=[
            jax.ShapeDtypeStruct((B, S, C_CONV + C_POOL), F32),
            jax.ShapeDtypeStruct((B, S, N_HEADS * D_KV_LAT), BF16),
            jax.ShapeDtypeStruct((B, S, D_KV_LAT), BF16),
            jax.ShapeDtypeStruct((B, S, nq), BF16),
            jax.ShapeDtypeStruct((B, S, LANES), F32),
        ],
        compiler_params=_cparams(("arbitrary", "arbitrary")),
        name="in_proj",
    )(x, sc, sh, g, w_in_p, qg, kvg, wq)


def _cp_kernel(cu_ref, ck_ref, cb_ref, lg_ref, lb_ref, pw_ref, ps_ref, o_ref, pad_ref):
    S = cu_ref.shape[1]
    TT = CONV_TILE
    pad_ref[0:HALO, :] = jnp.zeros((HALO, C_CONV + C_POOL), F32)
    pad_ref[HALO:HALO + S, :] = cu_ref[0]
    lane = lax.broadcasted_iota(jnp.int32, (TT, C_POOL), 1)
    row = lax.broadcasted_iota(jnp.int32, (TT, C_POOL), 0)
    win_len = jnp.where(lane < POOL_GROUP, 2,
                        jnp.where(lane < 2 * POOL_GROUP, 4, jnp.where(lane < 3 * POOL_GROUP, 8, 16)))

    def tile(i, carry):
        t0 = pl.multiple_of(i * TT, TT)
        win = pad_ref[pl.ds(t0, TT + HALO), :]
        wc = win[:, 0:C_CONV]
        wp = win[:, C_CONV:C_CONV + C_POOL]
        acc = jnp.zeros((TT, C_CONV), F32)
        for s in range(8):
            ws = wc if s == 0 else pltpu.roll(wc, TT + HALO - s, 0)
            for j in range(CONV_WIDTH):
                o = HALO - (CONV_WIDTH - 1) + j
                if o % 8 == s:
                    acc = acc + ck_ref[j:j + 1, :] * ws[o - s:o - s + TT, :]
        v = acc + cb_ref[...]
        mu = jnp.mean(v, axis=-1, keepdims=True)
        vc = v - mu
        var = jnp.mean(vc * vc, axis=-1, keepdims=True)
        yn = vc * lax.rsqrt(var + EPS) * lg_ref[...] + lb_ref[...]
        o_ref[0, pl.ds(t0, TT), 0:C_CONV] = (yn * jax.nn.sigmoid(yn)).astype(BF16)
        lo = HALO - 15
        s2 = wp[lo + 1:, :] + wp[lo:-1, :]
        s4 = s2[2:, :] + s2[:-2, :]
        s8 = s4[4:, :] + s4[:-4, :]
        s16 = s8[8:, :] + s8[:-8, :]
        u = wp[HALO:, :]
        sw = jnp.where(lane < POOL_GROUP, s2[14:, :],
                       jnp.where(lane < 2 * POOL_GROUP, s4[12:, :],
                                 jnp.where(lane < 3 * POOL_GROUP, s8[8:, :], s16)))
        n_pos = jnp.minimum(t0 + row + 1, win_len).astype(F32)
        d = (sw / n_pos - u).astype(BF16)
        yp = jnp.dot(d, pw_ref[...], preferred_element_type=F32) * ps_ref[...]
        o_ref[0, pl.ds(t0, TT), C_CONV:C_CONV + C_POOL] = yp.astype(BF16)
        return carry

    lax.fori_loop(0, S // TT, tile, 0)


def _conv_pool(cu, ck, cb, lg, lb, pw_bd, ps):
    B, S, W = cu.shape
    full = lambda a: pl.BlockSpec(a.shape, lambda b: (0,) * a.ndim)
    return pl.pallas_call(
        _cp_kernel,
        grid=(B,),
        in_specs=[pl.BlockSpec((1, S, W), lambda b: (b, 0, 0)),
                  full(ck), full(cb), full(lg), full(lb), full(pw_bd), full(ps)],
        out_specs=pl.BlockSpec((1, S, W), lambda b: (b, 0, 0)),
        out_shape=jax.ShapeDtypeStruct((B, S, W), BF16),
        scratch_shapes=[pltpu.VMEM((S + HALO, W), F32)],
        compiler_params=_cparams(("arbitrary",)),
        name="conv_pool",
    )(cu, ck, cb, lg, lb, pw_bd, ps)


def _attn_kernel(qi_ref, kwq_ref, kwk_ref, qabs_ref, ckv_ref, o_ref, key_ref, aux_ref, bias_ref, *, topk, q0):
    G, QB = qi_ref.shape[0], qi_ref.shape[1]
    LK = ckv_ref.shape[1]
    R = G * QB
    dn_nt = (((1,), (1,)), ((), ()))

    def visible(rows):
        kidx = lax.broadcasted_iota(jnp.int32, (rows, LK), 1)
        qpos = q0 + lax.broadcasted_iota(jnp.int32, (rows, 1), 0) % QB
        return kidx, kidx < (qpos // CHUNK + 1) * CHUNK

    if LK > topk:
        _, allowed = visible(QB)
        for g in range(G):
            ki = kwk_ref[g, :, 0:IDX_DIM].astype(BF16)
            wi = kwq_ref[g, :, IDX_DIM:IDX_DIM + IDX_HEADS]
            qi = qi_ref[g]
            score = jnp.zeros((QB, LK), F32)
            for h in range(IDX_HEADS):
                rel = lax.dot_general(qi[:, h * IDX_DIM:(h + 1) * IDX_DIM], ki, dn_nt,
                                      preferred_element_type=F32)
                score = score + jnp.maximum(rel, 0.0) * wi[:, h:h + 1]
            score = score * ((IDX_HEADS * IDX_DIM) ** -0.5)
            score = jnp.where(score == 0.0, 0.0, score)
            score = jnp.where(allowed, score, -jnp.inf)
            bits = lax.bitcast_convert_type(score, jnp.int32)
            key_ref[g * QB:(g + 1) * QB, :] = bits ^ ((bits >> 31) & jnp.int32(0x7FFFFFFF))

        kf = jnp.float32(topk)

        H = R // 2

        def partial_counts(r0, cand):
            acc = jnp.zeros((H, LANES), F32)
            for c in range(LK // LANES):
                k = key_ref[r0:r0 + H, c * LANES:(c + 1) * LANES]
                acc = acc + jnp.where(k >= cand, 1.0, 0.0)
            return acc

        def settle(acc, cand, pre):
            return jnp.where(jnp.sum(acc, axis=-1, keepdims=True) >= kf, cand, pre)

        def bit(i):
            return jnp.where(i <= 31, jnp.int32(1) << jnp.maximum(31 - i, 0), 0)

        def bit_step(i, carry):
            pre_a, pre_b, acc_b = carry
            cand_a = pre_a + bit(i)
            acc_a = partial_counts(0, cand_a)
            pre_b = settle(acc_b, pre_b + bit(i), pre_b)
            acc_b = partial_counts(H, pre_b + bit(i + 1))
            pre_a = settle(acc_a, cand_a, pre_a)
            return pre_a, pre_b, acc_b

        lowest = jnp.full((H, 1), jnp.iinfo(jnp.int32).min, jnp.int32)
        pre_a, pre_b, _ = lax.fori_loop(0, 32, bit_step,
                                        (lowest, lowest, partial_counts(H, lowest + bit(0))))
        thr = jnp.concatenate([pre_a, pre_b], axis=0)
        reached = key_ref[...] >= thr
        bias_ref[...] = jnp.where(reached, 0.0, -jnp.inf)
        has_excess = jnp.max(jnp.sum(jnp.where(reached, 1.0, 0.0), axis=-1, keepdims=True)) > kf

        @pl.when(has_excess)
        def _():
            kidx, allowed_r = visible(R)
            key = key_ref[...]
            n_gt = jnp.sum(jnp.where(key > thr, 1.0, 0.0), axis=-1, keepdims=True)
            need = kf - n_gt
            aux_ref[...] = jnp.where(key == thr, kidx, jnp.int32(LK))
            nbits = (LK - 1).bit_length()

            def idx_step(i, p):
                cand = p | (jnp.int32(1) << (nbits - 1 - i))
                cnt = jnp.sum(jnp.where(aux_ref[...] < cand, 1.0, 0.0), axis=-1, keepdims=True)
                return jnp.where(cnt < need, cand, p)

            cut = lax.fori_loop(0, nbits, idx_step, jnp.zeros((R, 1), jnp.int32))
            sel = ((key_ref[...] > thr) | (aux_ref[...] <= cut)) & allowed_r
            bias_ref[...] = jnp.where(sel, 0.0, -jnp.inf)

    for g in range(G):
        if LK > topk:
            bias = bias_ref[g * QB:(g + 1) * QB, :]
        else:
            bias = jnp.where(visible(QB)[1], 0.0, -jnp.inf)
        ckv = ckv_ref[g]
        for h in range(N_HEADS):
            qa = qabs_ref[g, :, h * D_KV_LAT:(h + 1) * D_KV_LAT]
            logit = lax.dot_general(qa, ckv, dn_nt, preferred_element_type=F32) + bias
            m = jnp.max(logit, axis=-1, keepdims=True)
            p = jnp.exp2(logit - m)
            l = jnp.sum(p, axis=-1, keepdims=True)
            ctx = jnp.dot(p.astype(BF16), ckv, preferred_element_type=F32)
            o_ref[g, :, h * D_KV_LAT:(h + 1) * D_KV_LAT] = (ctx / l).astype(BF16)


def _attention(qi, kw, qabs, ckv, qb):
    B, S, _ = ckv.shape
    topk = min(TOPK_MAX, S // 4)
    buf = qabs
    for j in range(S // qb):
        lk = (j + 1) * qb
        G = ATTN_BATCH if (B % ATTN_BATCH == 0 and lk <= ATTN_BATCH_MAX_KEYS) else 1
        blk = lambda n, j=j: pl.BlockSpec((G, qb, n), lambda b: (b, j, 0))
        keys = lambda n, lk=lk: pl.BlockSpec((G, lk, n), lambda b: (b, 0, 0))
        buf = pl.pallas_call(
            functools.partial(_attn_kernel, topk=topk, q0=j * qb),
            grid=(B // G,),
            in_specs=[blk(IDX_HEADS * IDX_DIM), blk(LANES), keys(LANES), blk(N_HEADS * D_KV_LAT),
                      keys(D_KV_LAT)],
            out_specs=blk(N_HEADS * D_KV_LAT),
            out_shape=jax.ShapeDtypeStruct(buf.shape, BF16),
            scratch_shapes=[pltpu.VMEM((G * qb, lk), jnp.int32), pltpu.VMEM((G * qb, lk), jnp.int32),
                            pltpu.VMEM((G * qb, lk), F32)],
            input_output_aliases={3: 0},
            compiler_params=_cparams(("arbitrary",)),
            name=f"dsa_attention_q{j}",
        )(qi, kw, kw, buf, ckv)
    return buf


def _out_kernel(ycp_ref, ctx_ref, x_ref, g1_ref, sc_ref, sh_ref, ng_ref, wcp_ref, wvo_ref,
                rw_ref, rb_ref, tri_ref, x1_ref, h2_ref, route_ref, cnt_ref, carry_ref):
    first = (pl.program_id(0) == 0) & (pl.program_id(1) == 0)

    @pl.when(first)
    def _():
        carry_ref[...] = jnp.zeros_like(carry_ref)

    mix = jnp.dot(ycp_ref[0], wcp_ref[...], preferred_element_type=F32)
    mix = mix + jnp.dot(ctx_ref[0], wvo_ref[...], preferred_element_type=F32)
    x1 = x_ref[0] + g1_ref[0] * mix
    x1_ref[0] = x1
    h2 = _rms(x1) * ng_ref[...] * (1.0 + sc_ref[0]) + sh_ref[0]
    _rows_to_tiles(h2_ref.at[0], h2)
    rw = rw_ref[...]
    rw_hi = rw.astype(BF16)
    rw_lo = (rw - rw_hi.astype(F32)).astype(BF16)
    h_hi = h2.astype(BF16)
    h_lo = (h2 - h_hi.astype(F32)).astype(BF16)
    logits = (jnp.dot(h_hi, rw_hi, preferred_element_type=F32)
              + (jnp.dot(h_lo, rw_hi, preferred_element_type=F32)
                 + jnp.dot(h_hi, rw_lo, preferred_element_type=F32))) + rb_ref[...]

    lt = logits.T
    tm = lt.shape[1]
    rowi = lax.broadcasted_iota(jnp.int32, (LANES, tm), 0)
    rowf = rowi.astype(F32)
    ninf = -jnp.inf
    big = jnp.float32(LANES)
    glog = jnp.where(rowi < N_GROUPS, lt, ninf)
    gmax = jnp.max(glog, axis=0, keepdims=True)
    gidx = jnp.min(jnp.where(glog == gmax, rowf, big), axis=0, keepdims=True)
    g_p = 1.0 / jnp.sum(jnp.exp(glog - gmax), axis=0, keepdims=True)
    lo = N_GROUPS + EXPERTS_PER_GROUP * gidx
    elog = jnp.where((rowf >= lo) & (rowf < lo + EXPERTS_PER_GROUP), lt, ninf)
    m1 = jnp.max(elog, axis=0, keepdims=True)
    i1 = jnp.min(jnp.where(elog == m1, rowf, big), axis=0, keepdims=True)
    elog2 = jnp.where(rowf == i1, ninf, elog)
    m2 = jnp.max(elog2, axis=0, keepdims=True)
    i2 = jnp.min(jnp.where(elog2 == m2, rowf, big), axis=0, keepdims=True)
    r = jnp.exp(m2 - m1)
    gate1 = g_p / (1.0 + r)
    gate2 = g_p * r / (1.0 + r)
    e1 = i1 - N_GROUPS
    e2 = i2 - N_GROUPS
    oh1 = rowf == e1
    oh2 = rowf == e2
    oh1f = jnp.where(oh1, 1.0, 0.0)
    oh2f = jnp.where(oh2, 1.0, 0.0)
    pre1 = jnp.dot(oh1f.astype(BF16), tri_ref[...], preferred_element_type=F32)
    pre2 = jnp.dot(oh2f.astype(BF16), tri_ref[...], preferred_element_type=F32)
    carry = carry_ref[:, 0:1]
    cnt1 = jnp.sum(oh1f, axis=1, keepdims=True)
    cnt2 = jnp.sum(oh2f, axis=1, keepdims=True)
    rank1 = jnp.sum(jnp.where(oh1, carry + pre1, 0.0), axis=0, keepdims=True)
    rank2 = jnp.sum(jnp.where(oh2, carry + cnt1 + pre2, 0.0), axis=0, keepdims=True)
    total = carry + cnt1 + cnt2
    carry_ref[...] = jnp.broadcast_to(total, carry_ref.shape)
    cnt_ref[...] = jnp.broadcast_to(total, cnt_ref.shape)
    route_t = jnp.where(rowi == 0, e1, jnp.where(rowi == 1, e2, jnp.where(rowi == 2, gate1, jnp.where(
        rowi == 3, gate2, jnp.where(rowi == 4, rank1, jnp.where(rowi == 5, rank2, 0.0))))))
    route_ref[0] = route_t.T


def _out_proj(ycp, ctx, x, g1, sc2, sh2, ng, wcp, wvo, rw, rb, tm):
    B, S, D = x.shape
    tok = lambda n: pl.BlockSpec((1, tm, n), lambda b, i: (b, i, 0))
    per_b = pl.BlockSpec((1, 1, D), lambda b, i: (b, 0, 0))
    full = lambda a: pl.BlockSpec(a.shape, lambda b, i: (0,) * a.ndim)
    tri = jnp.triu(jnp.ones((tm, tm), BF16), 1)
    return pl.pallas_call(
        _out_kernel,
        grid=(B, S // tm),
        in_specs=[tok(C_CONV + C_POOL), tok(N_HEADS * D_KV_LAT), tok(D), per_b, per_b, per_b,
                  full(ng), full(wcp), full(wvo), full(rw), full(rb), full(tri)],
        out_specs=[tok(D), pl.BlockSpec((1, tm * ROW_TILE, LANES), lambda b, i: (b, i, 0)), tok(LANES),
                   pl.BlockSpec((LANES, LANES), lambda b, i: (0, 0))],
        out_shape=[
            jax.ShapeDtypeStruct((B, S, D), F32),
            jax.ShapeDtypeStruct((B, S * ROW_TILE, LANES), F32),
            jax.ShapeDtypeStruct((B, S, LANES), F32),
            jax.ShapeDtypeStruct((LANES, LANES), F32),
        ],
        scratch_shapes=[pltpu.VMEM((LANES, LANES), F32)],
        compiler_params=_cparams(("arbitrary", "arbitrary")),
        name="out_proj_router",
    )(ycp, ctx, x, g1, sc2, sh2, ng, wcp, wvo, rw, rb, tri)


def _row_copy(src, i, dst, k, sem):
    return pltpu.make_async_copy(src.at[pl.ds(pl.multiple_of(i * ROW_TILE, ROW_TILE), ROW_TILE), :],
                                 dst.at[pl.ds(pl.multiple_of(k * ROW_TILE, ROW_TILE), ROW_TILE), :], sem)


def _rows_to_tiles(ref, val):
    n = val.shape[0]
    for s in range(ROW_TILE):
        ref[pl.ds(s, n, stride=ROW_TILE), :] = val[:, s * LANES:(s + 1) * LANES]


def _tiles_to_rows(ref, n):
    return jnp.concatenate([ref[pl.ds(s, n, stride=ROW_TILE), :] for s in range(ROW_TILE)], axis=-1)


def _dispatch_kernel(dst_ref, h_ref, xs_in_ref, xs_ref, sem):
    del xs_in_ref
    tm = h_ref.shape[0] // ROW_TILE

    def issue(r, c):
        _row_copy(h_ref, r, xs_ref, dst_ref[2 * r], sem).start()
        _row_copy(h_ref, r, xs_ref, dst_ref[2 * r + 1], sem).start()
        return c

    lax.fori_loop(0, tm, issue, 0, unroll=ROW_DMA_UNROLL)

    def drain(r, c):
        _row_copy(h_ref, r, xs_ref, dst_ref[2 * r], sem).wait()
        _row_copy(h_ref, r, xs_ref, dst_ref[2 * r + 1], sem).wait()
        return c

    lax.fori_loop(0, tm, drain, 0, unroll=ROW_DMA_UNROLL)


def _dispatch(dst_flat, h2, xs0, tm):
    T = h2.shape[0] // ROW_TILE
    return pl.pallas_call(
        _dispatch_kernel,
        grid=(T // tm,),
        in_specs=[pl.BlockSpec((2 * tm,), lambda i: (i,), memory_space=pltpu.SMEM),
                  pl.BlockSpec((tm * ROW_TILE, LANES), lambda i: (i, 0)),
                  pl.BlockSpec(memory_space=pl.ANY)],
        out_specs=pl.BlockSpec(memory_space=pl.ANY),
        out_shape=jax.ShapeDtypeStruct(xs0.shape, F32),
        scratch_shapes=[pltpu.SemaphoreType.DMA(())],
        input_output_aliases={2: 0},
        compiler_params=_cparams(("arbitrary",), row_dma=True),
        name="moe_dispatch",
    )(dst_flat, h2, xs0)


def _ffn_kernel(be_ref, nu_ref, xs_ref, w1_ref, w3_ref, w2_ref, y_ref, w13_scr, w2_scr):
    i = pl.program_id(0)
    new_expert = (i == 0) | (be_ref[i] != be_ref[jnp.maximum(i - 1, 0)])

    @pl.when(new_expert)
    def _():
        w13_scr[:, 0:D_EXPERT] = w1_ref[0, 0].astype(BF16)
        w13_scr[:, D_EXPERT:2 * D_EXPERT] = w3_ref[0, 0].astype(BF16)
        w2_scr[...] = w2_ref[0, 0].astype(BF16)

    @pl.when(i < nu_ref[0])
    def _():
        xb = _tiles_to_rows(xs_ref, FFN_BLOCK).astype(BF16)
        h13 = jnp.dot(xb, w13_scr[...], preferred_element_type=F32)
        h1 = h13[:, 0:D_EXPERT]
        h3 = h13[:, D_EXPERT:2 * D_EXPERT]
        act = (h1 * jax.nn.sigmoid(h1)) * h3
        _rows_to_tiles(y_ref, jnp.dot(act.astype(BF16), w2_scr[...], preferred_element_type=F32))

    @pl.when(i >= nu_ref[0])
    def _():
        y_ref[...] = jnp.zeros_like(y_ref)


def _expert_ffn(blk_exp, n_used, xs, w1, w3, w2, layer):
    D = w1.shape[2]
    rows_blk = FFN_BLOCK * ROW_TILE
    nblk = xs.shape[0] // rows_blk
    grid_spec = pltpu.PrefetchScalarGridSpec(
        num_scalar_prefetch=2,
        grid=(nblk,),
        in_specs=[
            pl.BlockSpec((rows_blk, LANES), lambda i, be, nu: (jnp.maximum(jnp.minimum(i, nu[0] - 1), 0), 0)),
            pl.BlockSpec((1, 1, D, D_EXPERT), lambda i, be, nu: (layer, be[i], 0, 0)),
            pl.BlockSpec((1, 1, D, D_EXPERT), lambda i, be, nu: (layer, be[i], 0, 0)),
            pl.BlockSpec((1, 1, D_EXPERT, D), lambda i, be, nu: (layer, be[i], 0, 0)),
        ],
        out_specs=pl.BlockSpec((rows_blk, LANES), lambda i, be, nu: (i, 0)),
        scratch_shapes=[pltpu.VMEM((D, 2 * D_EXPERT), BF16), pltpu.VMEM((D_EXPERT, D), BF16)],
    )
    return pl.pallas_call(
        _ffn_kernel,
        grid_spec=grid_spec,
        out_shape=jax.ShapeDtypeStruct(xs.shape, F32),
        compiler_params=_cparams(("arbitrary",)),
        name="expert_ffn",
    )(blk_exp, n_used, xs, w1, w3, w2)


def _combine_kernel(dst_ref, x1_ref, route_ref, g2_ref, fg_ref, y_ref, o_ref, ya_ref, yb_ref, sem,
                    *, final_norm):
    tm = x1_ref.shape[1]

    def issue(r, c):
        _row_copy(y_ref, dst_ref[2 * r], ya_ref, r, sem.at[0]).start()
        _row_copy(y_ref, dst_ref[2 * r + 1], yb_ref, r, sem.at[1]).start()
        return c

    lax.fori_loop(0, tm, issue, 0, unroll=ROW_DMA_UNROLL)

    def drain(r, c):
        _row_copy(y_ref, dst_ref[2 * r], ya_ref, r, sem.at[0]).wait()
        _row_copy(y_ref, dst_ref[2 * r + 1], yb_ref, r, sem.at[1]).wait()
        return c

    lax.fori_loop(0, tm, drain, 0, unroll=ROW_DMA_UNROLL)

    route = route_ref[0]
    moe = _tiles_to_rows(ya_ref, tm) * route[:, 2:3] + _tiles_to_rows(yb_ref, tm) * route[:, 3:4]
    out = x1_ref[0] + g2_ref[0] * moe
    if final_norm:
        out = _rms(out) * fg_ref[...]
    o_ref[0] = out


def _combine(dst_flat, x1, route, g2, fg, y, tm, final_norm):
    B, S, D = x1.shape
    nt = S // tm
    tok = lambda n: pl.BlockSpec((1, tm, n), lambda b, i: (b, i, 0))
    return pl.pallas_call(
        functools.partial(_combine_kernel, final_norm=final_norm),
        grid=(B, nt),
        in_specs=[pl.BlockSpec((2 * tm,), lambda b, i: (b * nt + i,), memory_space=pltpu.SMEM),
                  tok(D), tok(LANES),
                  pl.BlockSpec((1, 1, D), lambda b, i: (b, 0, 0)),
                  pl.BlockSpec((1, D), lambda b, i: (0, 0)),
                  pl.BlockSpec(memory_space=pl.ANY)],
        out_specs=tok(D),
        out_shape=jax.ShapeDtypeStruct((B, S, D), F32),
        scratch_shapes=[pltpu.VMEM((tm * ROW_TILE, LANES), F32), pltpu.VMEM((tm * ROW_TILE, LANES), F32),
                        pltpu.SemaphoreType.DMA((2,))],
        compiler_params=_cparams(("arbitrary", "arbitrary"), row_dma=True),
        name="moe_combine",
    )(dst_flat, x1, route, g2, fg, y)


def _pick_tile(n, pref):
    t = min(n, pref)
    while n % t:
        t //= 2
    return t


def kernel(x, c, mod_w, mod_b, norm1_g, w_in, conv_k, conv_b, conv_ln_g, conv_ln_b, q_norm_g, kv_norm_g, w_uq, w_uk, w_uv, pool_w, pool_scale, w_out, norm2_g, router_g_w, router_g_b, router_e_w, router_e_b, exp_w1, exp_w3, exp_w2, final_g):
    B, S, D = x.shape
    L = mod_w.shape[0]
    T = B * S
    tm = _pick_tile(S, 512)
    qb = _pick_tile(S, 256)
    tmc = _pick_tile(S, 512)
    n_rows = 2 * T + N_EXPERTS * FFN_BLOCK
    nblk = n_rows // FFN_BLOCK

    mod = _modulation(c, mod_w, mod_b)
    wq_all, wvo_all = _fold_weights(w_uq, w_uk, w_uv, w_out)
    row = lambda a: a.reshape(1, -1)

    assert D == ROW_TILE * LANES
    slot_buf = jnp.zeros((n_rows * ROW_TILE, LANES), F32)
    for l in range(L):
        sh1, sc1, g1, sh2, sc2, g2 = [mod[l, :, k * D:(k + 1) * D].reshape(B, 1, D) for k in range(6)]
        w_in_p = jnp.concatenate(
            [w_in[l, :, :N_IN - C_POOL], jnp.zeros((D, N_IN_PAD - N_IN), F32), w_in[l, :, N_IN - C_POOL:]],
            axis=1).astype(BF16)
        cu, qabs, ckv, qi, kw = _in_proj(x, sc1, sh1, row(norm1_g[l]), w_in_p, row(q_norm_g[l]),
                                         row(kv_norm_g[l]), wq_all[l], tm)
        pw_bd = jax.scipy.linalg.block_diag(*[pool_w[l, g] for g in range(pool_w.shape[1])]).astype(BF16)
        ycp = _conv_pool(cu, conv_k[l], row(conv_b[l]), row(conv_ln_g[l]), row(conv_ln_b[l]),
                         pw_bd, row(pool_scale[l]))
        ctx = _attention(qi, kw, qabs, ckv, qb)
        wcp = jnp.concatenate([w_out[l, :C_CONV], w_out[l, D - C_POOL:]], axis=0).astype(BF16)
        n_r = N_GROUPS + N_EXPERTS
        rw = jnp.concatenate([router_g_w[l], router_e_w[l].reshape(D, N_EXPERTS),
                              jnp.zeros((D, LANES - n_r), F32)], axis=1)
        rb = jnp.concatenate([router_g_b[l], router_e_b[l].reshape(N_EXPERTS),
                              jnp.zeros((LANES - n_r,), F32)]).reshape(1, LANES)
        x1, h2, route, cnt = _out_proj(ycp, ctx, x, g1, sc2, sh2, row(norm2_g[l]), wcp, wvo_all[l],
                                       rw, rb, tm)
        counts = cnt[:N_EXPERTS, 0].astype(jnp.int32)
        pcounts = (counts + FFN_BLOCK - 1) // FFN_BLOCK * FFN_BLOCK
        pends = jnp.cumsum(pcounts)
        pstarts = pends - pcounts
        route2 = route.reshape(T, LANES)
        e_idx = route2[:, 0:2].astype(jnp.int32)
        e_start = jnp.sum(jnp.where(e_idx[..., None] == jnp.arange(N_EXPERTS, dtype=jnp.int32), pstarts, 0),
                          axis=-1)
        dst = (e_start + route2[:, 4:6].astype(jnp.int32)).reshape(2 * T)
        n_used = (pends[-1] // FFN_BLOCK).astype(jnp.int32).reshape(1)
        blk_start = jnp.arange(nblk, dtype=jnp.int32) * FFN_BLOCK
        blk_exp = jnp.minimum(jnp.sum((pends[None, :] <= blk_start[:, None]).astype(jnp.int32), axis=1),
                              N_EXPERTS - 1)
        xs = _dispatch(dst, h2.reshape(T * ROW_TILE, LANES), slot_buf, tmc)
        y = _expert_ffn(blk_exp, n_used, xs, exp_w1, exp_w3, exp_w2, l)
        slot_buf = y
        x = _combine(dst, x1, route, g2, row(final_g), y, tmc, final_norm=(l == L - 1))
    return x
```

```python
import functools

import jax
import jax.numpy as jnp
from jax import lax
from jax.experimental import pallas as pl
from jax.experimental.pallas import tpu as pltpu

F32 = jnp.float32
BF16 = jnp.bfloat16
HIGHEST = lax.Precision.HIGHEST

EPS = 1e-6
CHUNK = 64
CONV_WIDTH = 31
C_CONV = 256
C_POOL = 256
POOL_GROUP = 64
N_HEADS = 8
HEAD_DIM = 64
D_Q_LAT = 256
D_KV_LAT = 128
IDX_HEADS = 8
IDX_DIM = 64
TOPK_MAX = 256
N_GROUPS = 4
EXPERTS_PER_GROUP = 8
N_EXPERTS = 32
D_EXPERT = 512
N_IN = 1736
N_IN_PAD = 1792
KW_OFF = 1408
POOL_OFF = 1536
LANES = 128
HALO = 32
CONV_TILE = 64
FFN_BLOCK = 256
ROW_DMA_UNROLL = 8
ROW_TILE = 8
ATTN_BATCH = 2
ATTN_BATCH_MAX_KEYS = 1024
LOG2E = 1.4426950408889634
VMEM_LIMIT = 56 * 1024 * 1024


def _cparams(sem, row_dma=False):
    return pltpu.CompilerParams(dimension_semantics=sem, vmem_limit_bytes=VMEM_LIMIT,
                                disable_bounds_checks=row_dma)


def _mod_kernel(c_ref, w_ref, b_ref, o_ref):
    c = c_ref[...]
    cond = c * jax.nn.sigmoid(c)
    o_ref[0] = jnp.dot(cond, w_ref[0], precision=HIGHEST, preferred_element_type=F32) + b_ref[0]


def _modulation(c, mod_w, mod_b):
    L, D, D6 = mod_w.shape
    B = c.shape[0]
    nj = D6 // D
    return pl.pallas_call(
        _mod_kernel,
        grid=(L, nj),
        in_specs=[
            pl.BlockSpec((B, D), lambda l, j: (0, 0)),
            pl.BlockSpec((1, D, D), lambda l, j: (l, 0, j)),
            pl.BlockSpec((1, 1, D), lambda l, j: (l, 0, j)),
        ],
        out_specs=pl.BlockSpec((1, B, D), lambda l, j: (l, 0, j)),
        out_shape=jax.ShapeDtypeStruct((L, B, D6), F32),
        compiler_params=_cparams(("arbitrary", "arbitrary")),
        name="modulation",
    )(c, mod_w, mod_b.reshape(L, 1, D6))


def _fold_kernel(uq_ref, ukt_ref, uv_ref, wo_ref, wq_ref, wvo_ref):
    wq_ref[0] = jnp.dot(uq_ref[0, 0], ukt_ref[0, 0], precision=HIGHEST,
                        preferred_element_type=F32).astype(BF16)
    wvo_ref[0] = jnp.dot(uv_ref[0, 0], wo_ref[0, 0], precision=HIGHEST,
                         preferred_element_type=F32).astype(BF16)


def _fold_weights(w_uq, w_uk, w_uv, w_out):
    L = w_uq.shape[0]
    D = w_out.shape[-1]
    uq = jnp.transpose(w_uq, (0, 2, 1, 3))
    ukt = jnp.transpose(w_uk, (0, 2, 3, 1))
    uv = jnp.transpose(w_uv, (0, 2, 1, 3))
    wo = w_out[:, C_CONV:C_CONV + N_HEADS * HEAD_DIM, :].reshape(L, N_HEADS, HEAD_DIM, D)
    return pl.pallas_call(
        _fold_kernel,
        grid=(L, N_HEADS),
        in_specs=[
            pl.BlockSpec((1, 1, D_Q_LAT, HEAD_DIM), lambda l, h: (l, h, 0, 0)),
            pl.BlockSpec((1, 1, HEAD_DIM, D_KV_LAT), lambda l, h: (l, h, 0, 0)),
            pl.BlockSpec((1, 1, D_KV_LAT, HEAD_DIM), lambda l, h: (l, h, 0, 0)),
            pl.BlockSpec((1, 1, HEAD_DIM, D), lambda l, h: (l, h, 0, 0)),
        ],
        out_specs=[
            pl.BlockSpec((1, D_Q_LAT, D_KV_LAT), lambda l, h: (l, 0, h)),
            pl.BlockSpec((1, D_KV_LAT, D), lambda l, h: (l, h, 0)),
        ],
        out_shape=[
            jax.ShapeDtypeStruct((L, D_Q_LAT, N_HEADS * D_KV_LAT), BF16),
            jax.ShapeDtypeStruct((L, N_HEADS * D_KV_LAT, D), BF16),
        ],
        compiler_params=_cparams(("arbitrary", "arbitrary")),
        name="fold_weights",
    )(uq, ukt, uv, wo)


def _rms(v):
    return v * lax.rsqrt(jnp.mean(v * v, axis=-1, keepdims=True) + EPS)


def _in_kernel(x_ref, sc_ref, sh_ref, g_ref, w_ref, qg_ref, kvg_ref, wq_ref,
               cu_ref, qabs_ref, ckv_ref, qi_ref, kw_ref):
    x = x_ref[0]
    h = _rms(x) * g_ref[...] * (1.0 + sc_ref[0]) + sh_ref[0]
    z = jnp.dot(h.astype(BF16), w_ref[...], preferred_element_type=F32)
    a_val = z[:, 0:C_CONV]
    a_gate = z[:, C_CONV:2 * C_CONV]
    cu_ref[0, :, 0:C_CONV] = a_val * jax.nn.sigmoid(a_gate)
    cu_ref[0, :, C_CONV:C_CONV + C_POOL] = z[:, POOL_OFF:POOL_OFF + C_POOL]
    o = 2 * C_CONV
    cq = _rms(z[:, o:o + D_Q_LAT]) * qg_ref[...]
    qabs = jnp.dot(cq.astype(BF16), wq_ref[...], preferred_element_type=F32)
    qabs_ref[0] = (qabs * (HEAD_DIM ** -0.5 * LOG2E)).astype(BF16)
    o += D_Q_LAT
    ckv_ref[0] = (_rms(z[:, o:o + D_KV_LAT]) * kvg_ref[...]).astype(BF16)
    o += D_KV_LAT
    qi_ref[0] = z[:, o:o + IDX_HEADS * IDX_DIM].astype(BF16)
    kw_ref[0] = z[:, KW_OFF:KW_OFF + LANES]


def _in_proj(x, sc, sh, g, w_in_p, qg, kvg, wq, tm):
    B, S, D = x.shape
    tok = lambda n: pl.BlockSpec((1, tm, n), lambda b, i: (b, i, 0))
    per_b = pl.BlockSpec((1, 1, D), lambda b, i: (b, 0, 0))
    full = lambda a: pl.BlockSpec(a.shape, lambda b, i: (0,) * a.ndim)
    nq = IDX_HEADS * IDX_DIM
    return pl.pallas_call(
        _in_kernel,
        grid=(B, S // tm),
        in_specs=[tok(D), per_b, per_b, full(g), full(w_in_p), full(qg), full(kvg), full(wq)],
        out_specs=[tok(C_CONV + C_POOL), tok(N_HEADS * D_KV_LAT), tok(D_KV_LAT), tok(nq), tok(LANES)],
        out_shape=[
            jax.ShapeDtypeStruct((B, S, C_CONV + C_POOL), F32),
            jax.ShapeDtypeStruct((B, S, N_HEADS * D_KV_LAT), BF16),
            jax.ShapeDtypeStruct((B, S, D_KV_LAT), BF16),
            jax.ShapeDtypeStruct((B, S, nq), BF16),
            jax.ShapeDtypeStruct((B, S, LANES), F32),
        ],
        compiler_params=_cparams(("arbitrary", "arbitrary")),
        name="in_proj",
    )(x, sc, sh, g, w_in_p, qg, kvg, wq)


def _cp_kernel(cu_ref, ck_ref, cb_ref, lg_ref, lb_ref, pw_ref, ps_ref, o_ref, pad_ref):
    S = cu_ref.shape[1]
    TT = CONV_TILE
    pad_ref[0:HALO, :] = jnp.zeros((HALO, C_CONV + C_POOL), F32)
    pad_ref[HALO:HALO + S, :] = cu_ref[0]
    lane = lax.broadcasted_iota(jnp.int32, (TT, C_POOL), 1)
    row = lax.broadcasted_iota(jnp.int32, (TT, C_POOL), 0)
    win_len = jnp.where(lane < POOL_GROUP, 2,
                        jnp.where(lane < 2 * POOL_GROUP, 4, jnp.where(lane < 3 * POOL_GROUP, 8, 16)))

    def tile(i, carry):
        t0 = pl.multiple_of(i * TT, TT)
        win = pad_ref[pl.ds(t0, TT + HALO), :]
        wc = win[:, 0:C_CONV]
        wp = win[:, C_CONV:C_CONV + C_POOL]
        acc = jnp.zeros((TT, C_CONV), F32)
        for s in range(8):
            ws = wc if s == 0 else pltpu.roll(wc, TT + HALO - s, 0)
            for j in range(CONV_WIDTH):
                o = HALO - (CONV_WIDTH - 1) + j
                if o % 8 == s:
                    acc = acc + ck_ref[j:j + 1, :] * ws[o - s:o - s + TT, :]
        v = acc + cb_ref[...]
        mu = jnp.mean(v, axis=-1, keepdims=True)
        vc = v - mu
        var = jnp.mean(vc * vc, axis=-1, keepdims=True)
        yn = vc * lax.rsqrt(var + EPS) * lg_ref[...] + lb_ref[...]
        o_ref[0, pl.ds(t0, TT), 0:C_CONV] = (yn * jax.nn.sigmoid(yn)).astype(BF16)
        lo = HALO - 15
        s2 = wp[lo + 1:, :] + wp[lo:-1, :]
        s4 = s2[2:, :] + s2[:-2, :]
        s8 = s4[4:, :] + s4[:-4, :]
        s16 = s8[8:, :] + s8[:-8, :]
        u = wp[HALO:, :]
        sw = jnp.where(lane < POOL_GROUP, s2[14:, :],
                       jnp.where(lane < 2 * POOL_GROUP, s4[12:, :],
                                 jnp.where(lane < 3 * POOL_GROUP, s8[8:, :], s16)))
        n_pos = jnp.minimum(t0 + row + 1, win_len).astype(F32)
        d = (sw / n_pos - u).astype(BF16)
        yp = jnp.dot(d, pw_ref[...], preferred_element_type=F32) * ps_ref[...]
        o_ref[0, pl.ds(t0, TT), C_CONV:C_CONV + C_POOL] = yp.astype(BF16)
        return carry

    lax.fori_loop(0, S // TT, tile, 0)


def _conv_pool(cu, ck, cb, lg, lb, pw_bd, ps):
    B, S, W = cu.shape
    full = lambda a: pl.BlockSpec(a.shape, lambda b: (0,) * a.ndim)
    return pl.pallas_call(
        _cp_kernel,
        grid=(B,),
        in_specs=[pl.BlockSpec((1, S, W), lambda b: (b, 0, 0)),
                  full(ck), full(cb), full(lg), full(lb), full(pw_bd), full(ps)],
        out_specs=pl.BlockSpec((1, S, W), lambda b: (b, 0, 0)),
        out_shape=jax.ShapeDtypeStruct((B, S, W), BF16),
        scratch_shapes=[pltpu.VMEM((S + HALO, W), F32)],
        compiler_params=_cparams(("arbitrary",)),
        name="conv_pool",
    )(cu, ck, cb, lg, lb, pw_bd, ps)


def _attn_kernel(qi_ref, kwq_ref, kwk_ref, qabs_ref, ckv_ref, o_ref, key_ref, aux_ref, bias_ref, *, topk, q0):
    G, QB = qi_ref.shape[0], qi_ref.shape[1]
    LK = ckv_ref.shape[1]
    R = G * QB
    dn_nt = (((1,), (1,)), ((), ()))

    def visible(rows):
        kidx = lax.broadcasted_iota(jnp.int32, (rows, LK), 1)
        qpos = q0 + lax.broadcasted_iota(jnp.int32, (rows, 1), 0) % QB
        return kidx, kidx < (qpos // CHUNK + 1) * CHUNK

    if LK > topk:
        _, allowed = visible(QB)
        for g in range(G):
            ki = kwk_ref[g, :, 0:IDX_DIM].astype(BF16)
            wi = kwq_ref[g, :, IDX_DIM:IDX_DIM + IDX_HEADS]
            qi = qi_ref[g]
            score = jnp.zeros((QB, LK), F32)
            for h in range(IDX_HEADS):
                rel = lax.dot_general(qi[:, h * IDX_DIM:(h + 1) * IDX_DIM], ki, dn_nt,
                                      preferred_element_type=F32)
                score = score + jnp.maximum(rel, 0.0) * wi[:, h:h + 1]
            score = score * ((IDX_HEADS * IDX_DIM) ** -0.5)
            score = jnp.where(score == 0.0, 0.0, score)
            score = jnp.where(allowed, score, -jnp.inf)
            bits = lax.bitcast_convert_type(score, jnp.int32)
            key_ref[g * QB:(g + 1) * QB, :] = bits ^ ((bits >> 31) & jnp.int32(0x7FFFFFFF))

        kf = jnp.float32(topk)

        H = R // 2

        def partial_counts(r0, cand):
            acc = jnp.zeros((H, LANES), F32)
            for c in range(LK // LANES):
                k = key_ref[r0:r0 + H, c * LANES:(c + 1) * LANES]
                acc = acc + jnp.where(k >= cand, 1.0, 0.0)
            return acc

        def settle(acc, cand, pre):
            return jnp.where(jnp.sum(acc, axis=-1, keepdims=True) >= kf, cand, pre)

        def bit(i):
            return jnp.where(i <= 31, jnp.int32(1) << jnp.maximum(31 - i, 0), 0)

        def bit_step(i, carry):
            pre_a, pre_b, acc_b = carry
            cand_a = pre_a + bit(i)
            acc_a = partial_counts(0, cand_a)
            pre_b = settle(acc_b, pre_b + bit(i), pre_b)
            acc_b = partial_counts(H, pre_b + bit(i + 1))
            pre_a = settle(acc_a, cand_a, pre_a)
            return pre_a, pre_b, acc_b

        lowest = jnp.full((H, 1), jnp.iinfo(jnp.int32).min, jnp.int32)
        pre_a, pre_b, _ = lax.fori_loop(0, 32, bit_step,
                                        (lowest, lowest, partial_counts(H, lowest + bit(0))))
        thr = jnp.concatenate([pre_a, pre_b], axis=0)
        reached = key_ref[...] >= thr
        bias_ref[...] = jnp.where(reached, 0.0, -jnp.inf)
        has_excess = jnp.max(jnp.sum(jnp.where(reached, 1.0, 0.0), axis=-1, keepdims=True)) > kf

        @pl.when(has_excess)
        def _():
            kidx, allowed_r = visible(R)
            key = key_ref[...]
            n_gt = jnp.sum(jnp.where(key > thr, 1.0, 0.0), axis=-1, keepdims=True)
            need = kf - n_gt
            aux_ref[...] = jnp.where(key == thr, kidx, jnp.int32(LK))
            nbits = (LK - 1).bit_length()

            def idx_step(i, p):
                cand = p | (jnp.int32(1) << (nbits - 1 - i))
                cnt = jnp.sum(jnp.where(aux_ref[...] < cand, 1.0, 0.0), axis=-1, keepdims=True)
                return jnp.where(cnt < need, cand, p)

            cut = lax.fori_loop(0, nbits, idx_step, jnp.zeros((R, 1), jnp.int32))
            sel = ((key_ref[...] > thr) | (aux_ref[...] <= cut)) & allowed_r
            bias_ref[...] = jnp.where(sel, 0.0, -jnp.inf)

    for g in range(G):
        if LK > topk:
            bias = bias_ref[g * QB:(g + 1) * QB, :]
        else:
            bias = jnp.where(visible(QB)[1], 0.0, -jnp.inf)
        ckv = ckv_ref[g]
        for h in range(N_HEADS):
            qa = qabs_ref[g, :, h * D_KV_LAT:(h + 1) * D_KV_LAT]
            logit = lax.dot_general(qa, ckv, dn_nt, preferred_element_type=F32) + bias
            m = jnp.max(logit, axis=-1, keepdims=True)
            p = jnp.exp2(logit - m)
            l = jnp.sum(p, axis=-1, keepdims=True)
            ctx = jnp.dot(p.astype(BF16), ckv, preferred_element_type=F32)
            o_ref[g, :, h * D_KV_LAT:(h + 1) * D_KV_LAT] = (ctx / l).astype(BF16)


def _attention(qi, kw, qabs, ckv, qb):
    B, S, _ = ckv.shape
    topk = min(TOPK_MAX, S // 4)
    buf = qabs
    for j in range(S // qb):
        lk = (j + 1) * qb
        G = ATTN_BATCH if (B % ATTN_BATCH == 0 and lk <= ATTN_BATCH_MAX_KEYS) else 1
        blk = lambda n, j=j: pl.BlockSpec((G, qb, n), lambda b: (b, j, 0))
        keys = lambda n, lk=lk: pl.BlockSpec((G, lk, n), lambda b: (b, 0, 0))
        buf = pl.pallas_call(
            functools.partial(_attn_kernel, topk=topk, q0=j * qb),
            grid=(B // G,),
            in_specs=[blk(IDX_HEADS * IDX_DIM), blk(LANES), keys(LANES), blk(N_HEADS * D_KV_LAT),
                      keys(D_KV_LAT)],
            out_specs=blk(N_HEADS * D_KV_LAT),
            out_shape=jax.ShapeDtypeStruct(buf.shape, BF16),
            scratch_shapes=[pltpu.VMEM((G * qb, lk), jnp.int32), pltpu.VMEM((G * qb, lk), jnp.int32),
                            pltpu.VMEM((G * qb, lk), F32)],
            input_output_aliases={3: 0},
            compiler_params=_cparams(("arbitrary",)),
            name=f"dsa_attention_q{j}",
        )(qi, kw, kw, buf, ckv)
    return buf


def _out_kernel(ycp_ref, ctx_ref, x_ref, g1_ref, sc_ref, sh_ref, ng_ref, wcp_ref, wvo_ref,
                rw_ref, rb_ref, tri_ref, x1_ref, h2_ref, route_ref, cnt_ref, carry_ref):
    first = (pl.program_id(0) == 0) & (pl.program_id(1) == 0)

    @pl.when(first)
    def _():
        carry_ref[...] = jnp.zeros_like(carry_ref)

    mix = jnp.dot(ycp_ref[0], wcp_ref[...], preferred_element_type=F32)
    mix = mix + jnp.dot(ctx_ref[0], wvo_ref[...], preferred_element_type=F32)
    x1 = x_ref[0] + g1_ref[0] * mix
    x1_ref[0] = x1
    h2 = _rms(x1) * ng_ref[...] * (1.0 + sc_ref[0]) + sh_ref[0]
    _rows_to_tiles(h2_ref.at[0], h2)
    rw = rw_ref[...]
    rw_hi = rw.astype(BF16)
    rw_lo = (rw - rw_hi.astype(F32)).astype(BF16)
    h_hi = h2.astype(BF16)
    h_lo = (h2 - h_hi.astype(F32)).astype(BF16)
    logits = (jnp.dot(h_hi, rw_hi, preferred_element_type=F32)
              + (jnp.dot(h_lo, rw_hi, preferred_element_type=F32)
                 + jnp.dot(h_hi, rw_lo, preferred_element_type=F32))) + rb_ref[...]

    lt = logits.T
    tm = lt.shape[1]
    rowi = lax.broadcasted_iota(jnp.int32, (LANES, tm), 0)
    rowf = rowi.astype(F32)
    ninf = -jnp.inf
    big = jnp.float32(LANES)
    glog = jnp.where(rowi < N_GROUPS, lt, ninf)
    gmax = jnp.max(glog, axis=0, keepdims=True)
    gidx = jnp.min(jnp.where(glog == gmax, rowf, big), axis=0, keepdims=True)
    g_p = 1.0 / jnp.sum(jnp.exp(glog - gmax), axis=0, keepdims=True)
    lo = N_GROUPS + EXPERTS_PER_GROUP * gidx
    elog = jnp.where((rowf >= lo) & (rowf < lo + EXPERTS_PER_GROUP), lt, ninf)
    m1 = jnp.max(elog, axis=0, keepdims=True)
    i1 = jnp.min(jnp.where(elog == m1, rowf, big), axis=0, keepdims=True)
    elog2 = jnp.where(rowf == i1, ninf, elog)
    m2 = jnp.max(elog2, axis=0, keepdims=True)
    i2 = jnp.min(jnp.where(elog2 == m2, rowf, big), axis=0, keepdims=True)
    r = jnp.exp(m2 - m1)
    gate1 = g_p / (1.0 + r)
    gate2 = g_p * r / (1.0 + r)
    e1 = i1 - N_GROUPS
    e2 = i2 - N_GROUPS
    oh1 = rowf == e1
    oh2 = rowf == e2
    oh1f = jnp.where(oh1, 1.0, 0.0)
    oh2f = jnp.where(oh2, 1.0, 0.0)
    pre1 = jnp.dot(oh1f.astype(BF16), tri_ref[...], preferred_element_type=F32)
    pre2 = jnp.dot(oh2f.astype(BF16), tri_ref[...], preferred_element_type=F32)
    carry = carry_ref[:, 0:1]
    cnt1 = jnp.sum(oh1f, axis=1, keepdims=True)
    cnt2 = jnp.sum(oh2f, axis=1, keepdims=True)
    rank1 = jnp.sum(jnp.where(oh1, carry + pre1, 0.0), axis=0, keepdims=True)
    rank2 = jnp.sum(jnp.where(oh2, carry + cnt1 + pre2, 0.0), axis=0, keepdims=True)
    total = carry + cnt1 + cnt2
    carry_ref[...] = jnp.broadcast_to(total, carry_ref.shape)
    cnt_ref[...] = jnp.broadcast_to(total, cnt_ref.shape)
    route_t = jnp.where(rowi == 0, e1, jnp.where(rowi == 1, e2, jnp.where(rowi == 2, gate1, jnp.where(
        rowi == 3, gate2, jnp.where(rowi == 4, rank1, jnp.where(rowi == 5, rank2, 0.0))))))
    route_ref[0] = route_t.T


def _out_proj(ycp, ctx, x, g1, sc2, sh2, ng, wcp, wvo, rw, rb, tm):
    B, S, D = x.shape
    tok = lambda n: pl.BlockSpec((1, tm, n), lambda b, i: (b, i, 0))
    per_b = pl.BlockSpec((1, 1, D), lambda b, i: (b, 0, 0))
    full = lambda a: pl.BlockSpec(a.shape, lambda b, i: (0,) * a.ndim)
    tri = jnp.triu(jnp.ones((tm, tm), BF16), 1)
    return pl.pallas_call(
        _out_kernel,
        grid=(B, S // tm),
        in_specs=[tok(C_CONV + C_POOL), tok(N_HEADS * D_KV_LAT), tok(D), per_b, per_b, per_b,
                  full(ng), full(wcp), full(wvo), full(rw), full(rb), full(tri)],
        out_specs=[tok(D), pl.BlockSpec((1, tm * ROW_TILE, LANES), lambda b, i: (b, i, 0)), tok(LANES),
                   pl.BlockSpec((LANES, LANES), lambda b, i: (0, 0))],
        out_shape=[
            jax.ShapeDtypeStruct((B, S, D), F32),
            jax.ShapeDtypeStruct((B, S * ROW_TILE, LANES), F32),
            jax.ShapeDtypeStruct((B, S, LANES), F32),
            jax.ShapeDtypeStruct((LANES, LANES), F32),
        ],
        scratch_shapes=[pltpu.VMEM((LANES, LANES), F32)],
        compiler_params=_cparams(("arbitrary", "arbitrary")),
        name="out_proj_router",
    )(ycp, ctx, x, g1, sc2, sh2, ng, wcp, wvo, rw, rb, tri)


def _row_copy(src, i, dst, k, sem):
    return pltpu.make_async_copy(src.at[pl.ds(pl.multiple_of(i * ROW_TILE, ROW_TILE), ROW_TILE), :],
                                 dst.at[pl.ds(pl.multiple_of(k * ROW_TILE, ROW_TILE), ROW_TILE), :], sem)


def _rows_to_tiles(ref, val):
    n = val.shape[0]
    for s in range(ROW_TILE):
        ref[pl.ds(s, n, stride=ROW_TILE), :] = val[:, s * LANES:(s + 1) * LANES]


def _tiles_to_rows(ref, n):
    return jnp.concatenate([ref[pl.ds(s, n, stride=ROW_TILE), :] for s in range(ROW_TILE)], axis=-1)


def _dispatch_kernel(dst_ref, h_ref, xs_in_ref, xs_ref, sem):
    del xs_in_ref
    tm = h_ref.shape[0] // ROW_TILE

    def issue(r, c):
        _row_copy(h_ref, r, xs_ref, dst_ref[2 * r], sem).start(priority=0)
        _row_copy(h_ref, r, xs_ref, dst_ref[2 * r + 1], sem).start(priority=1)
        return c

    lax.fori_loop(0, tm, issue, 0, unroll=ROW_DMA_UNROLL)

    def drain(r, c):
        _row_copy(h_ref, r, xs_ref, dst_ref[2 * r], sem).wait()
        _row_copy(h_ref, r, xs_ref, dst_ref[2 * r + 1], sem).wait()
        return c

    lax.fori_loop(0, tm, drain, 0, unroll=ROW_DMA_UNROLL)


def _dispatch(dst_flat, h2, xs0, tm):
    T = h2.shape[0] // ROW_TILE
    return pl.pallas_call(
        _dispatch_kernel,
        grid=(T // tm,),
        in_specs=[pl.BlockSpec((2 * tm,), lambda i: (i,), memory_space=pltpu.SMEM),
                  pl.BlockSpec((tm * ROW_TILE, LANES), lambda i: (i, 0)),
                  pl.BlockSpec(memory_space=pl.ANY)],
        out_specs=pl.BlockSpec(memory_space=pl.ANY),
        out_shape=jax.ShapeDtypeStruct(xs0.shape, F32),
        scratch_shapes=[pltpu.SemaphoreType.DMA(())],
        input_output_aliases={2: 0},
        compiler_params=_cparams(("arbitrary",), row_dma=True),
        name="moe_dispatch",
    )(dst_flat, h2, xs0)


def _ffn_kernel(be_ref, nu_ref, xs_ref, w1_ref, w3_ref, w2_ref, y_ref, w13_scr, w2_scr):
    i = pl.program_id(0)
    new_expert = (i == 0) | (be_ref[i] != be_ref[jnp.maximum(i - 1, 0)])

    @pl.when(new_expert)
    def _():
        w13_scr[:, 0:D_EXPERT] = w1_ref[0, 0].astype(BF16)
        w13_scr[:, D_EXPERT:2 * D_EXPERT] = w3_ref[0, 0].astype(BF16)
        w2_scr[...] = w2_ref[0, 0].astype(BF16)

    @pl.when(i < nu_ref[0])
    def _():
        xb = _tiles_to_rows(xs_ref, FFN_BLOCK).astype(BF16)
        h13 = jnp.dot(xb, w13_scr[...], preferred_element_type=F32)
        h1 = h13[:, 0:D_EXPERT]
        h3 = h13[:, D_EXPERT:2 * D_EXPERT]
        act = (h1 * jax.nn.sigmoid(h1)) * h3
        _rows_to_tiles(y_ref, jnp.dot(act.astype(BF16), w2_scr[...], preferred_element_type=F32))

    @pl.when(i >= nu_ref[0])
    def _():
        y_ref[...] = jnp.zeros_like(y_ref)


def _expert_ffn(blk_exp, n_used, xs, w1, w3, w2, layer):
    D = w1.shape[2]
    rows_blk = FFN_BLOCK * ROW_TILE
    nblk = xs.shape[0] // rows_blk
    grid_spec = pltpu.PrefetchScalarGridSpec(
        num_scalar_prefetch=2,
        grid=(nblk,),
        in_specs=[
            pl.BlockSpec((rows_blk, LANES), lambda i, be, nu: (jnp.maximum(jnp.minimum(i, nu[0] - 1), 0), 0)),
            pl.BlockSpec((1, 1, D, D_EXPERT), lambda i, be, nu: (layer, be[i], 0, 0)),
            pl.BlockSpec((1, 1, D, D_EXPERT), lambda i, be, nu: (layer, be[i], 0, 0)),
            pl.BlockSpec((1, 1, D_EXPERT, D), lambda i, be, nu: (layer, be[i], 0, 0)),
        ],
        out_specs=pl.BlockSpec((rows_blk, LANES), lambda i, be, nu: (i, 0)),
        scratch_shapes=[pltpu.VMEM((D, 2 * D_EXPERT), BF16), pltpu.VMEM((D_EXPERT, D), BF16)],
    )
    return pl.pallas_call(
        _ffn_kernel,
        grid_spec=grid_spec,
        out_shape=jax.ShapeDtypeStruct(xs.shape, F32),
        compiler_params=_cparams(("arbitrary",)),
        name="expert_ffn",
    )(blk_exp, n_used, xs, w1, w3, w2)


def _combine_kernel(dst_ref, x1_ref, route_ref, g2_ref, fg_ref, y_ref, o_ref, ya_ref, yb_ref, sem,
                    *, final_norm):
    tm = x1_ref.shape[1]

    def issue(r, c):
        _row_copy(y_ref, dst_ref[2 * r], ya_ref, r, sem.at[0]).start(priority=0)
        _row_copy(y_ref, dst_ref[2 * r + 1], yb_ref, r, sem.at[1]).start(priority=1)
        return c

    lax.fori_loop(0, tm, issue, 0, unroll=ROW_DMA_UNROLL)

    def drain(r, c):
        _row_copy(y_ref, dst_ref[2 * r], ya_ref, r, sem.at[0]).wait()
        _row_copy(y_ref, dst_ref[2 * r + 1], yb_ref, r, sem.at[1]).wait()
        return c

    lax.fori_loop(0, tm, drain, 0, unroll=ROW_DMA_UNROLL)

    route = route_ref[0]
    moe = _tiles_to_rows(ya_ref, tm) * route[:, 2:3] + _tiles_to_rows(yb_ref, tm) * route[:, 3:4]
    out = x1_ref[0] + g2_ref[0] * moe
    if final_norm:
        out = _rms(out) * fg_ref[...]
    o_ref[0] = out


def _combine(dst_flat, x1, route, g2, fg, y, tm, final_norm):
    B, S, D = x1.shape
    nt = S // tm
    tok = lambda n: pl.BlockSpec((1, tm, n), lambda b, i: (b, i, 0))
    return pl.pallas_call(
        functools.partial(_combine_kernel, final_norm=final_norm),
        grid=(B, nt),
        in_specs=[pl.BlockSpec((2 * tm,), lambda b, i: (b * nt + i,), memory_space=pltpu.SMEM),
                  tok(D), tok(LANES),
                  pl.BlockSpec((1, 1, D), lambda b, i: (b, 0, 0)),
                  pl.BlockSpec((1, D), lambda b, i: (0, 0)),
                  pl.BlockSpec(memory_space=pl.ANY)],
        out_specs=tok(D),
        out_shape=jax.ShapeDtypeStruct((B, S, D), F32),
        scratch_shapes=[pltpu.VMEM((tm * ROW_TILE, LANES), F32), pltpu.VMEM((tm * ROW_TILE, LANES), F32),
                        pltpu.SemaphoreType.DMA((2,))],
        compiler_params=_cparams(("arbitrary", "arbitrary"), row_dma=True),
        name="moe_combine",
    )(dst_flat, x1, route, g2, fg, y)


def _pick_tile(n, pref):
    t = min(n, pref)
    while n % t:
        t //= 2
    return t


def kernel(x, c, mod_w, mod_b, norm1_g, w_in, conv_k, conv_b, conv_ln_g, conv_ln_b, q_norm_g, kv_norm_g, w_uq, w_uk, w_uv, pool_w, pool_scale, w_out, norm2_g, router_g_w, router_g_b, router_e_w, router_e_b, exp_w1, exp_w3, exp_w2, final_g):
    B, S, D = x.shape
    L = mod_w.shape[0]
    T = B * S
    tm = _pick_tile(S, 512)
    qb = _pick_tile(S, 256)
    tmc = _pick_tile(S, 512)
    n_rows = 2 * T + N_EXPERTS * FFN_BLOCK
    nblk = n_rows // FFN_BLOCK

    mod = _modulation(c, mod_w, mod_b)
    wq_all, wvo_all = _fold_weights(w_uq, w_uk, w_uv, w_out)
    row = lambda a: a.reshape(1, -1)

    assert D == ROW_TILE * LANES
    slot_buf = jnp.zeros((n_rows * ROW_TILE, LANES), F32)
    for l in range(L):
        sh1, sc1, g1, sh2, sc2, g2 = [mod[l, :, k * D:(k + 1) * D].reshape(B, 1, D) for k in range(6)]
        w_in_p = jnp.concatenate(
            [w_in[l, :, :N_IN - C_POOL], jnp.zeros((D, N_IN_PAD - N_IN), F32), w_in[l, :, N_IN - C_POOL:]],
            axis=1).astype(BF16)
        cu, qabs, ckv, qi, kw = _in_proj(x, sc1, sh1, row(norm1_g[l]), w_in_p, row(q_norm_g[l]),
                                         row(kv_norm_g[l]), wq_all[l], tm)
        pw_bd = jax.scipy.linalg.block_diag(*[pool_w[l, g] for g in range(pool_w.shape[1])]).astype(BF16)
        ycp = _conv_pool(cu, conv_k[l], row(conv_b[l]), row(conv_ln_g[l]), row(conv_ln_b[l]),
                         pw_bd, row(pool_scale[l]))
        ctx = _attention(qi, kw, qabs, ckv, qb)
        wcp = jnp.concatenate([w_out[l, :C_CONV], w_out[l, D - C_POOL:]], axis=0).astype(BF16)
        n_r = N_GROUPS + N_EXPERTS
        rw = jnp.concatenate([router_g_w[l], router_e_w[l].reshape(D, N_EXPERTS),
                              jnp.zeros((D, LANES - n_r), F32)], axis=1)
        rb = jnp.concatenate([router_g_b[l], router_e_b[l].reshape(N_EXPERTS),
                              jnp.zeros((LANES - n_r,), F32)]).reshape(1, LANES)
        x1, h2, route, cnt = _out_proj(ycp, ctx, x, g1, sc2, sh2, row(norm2_g[l]), wcp, wvo_all[l],
                                       rw, rb, tm)
        counts = cnt[:N_EXPERTS, 0].astype(jnp.int32)
        pcounts = (counts + FFN_BLOCK - 1) // FFN_BLOCK * FFN_BLOCK
        pends = jnp.cumsum(pcounts)
        pstarts = pends - pcounts
        route2 = route.reshape(T, LANES)
        e_idx = route2[:, 0:2].astype(jnp.int32)
        e_start = jnp.sum(jnp.where(e_idx[..., None] == jnp.arange(N_EXPERTS, dtype=jnp.int32), pstarts, 0),
                          axis=-1)
        dst = (e_start + route2[:, 4:6].astype(jnp.int32)).reshape(2 * T)
        n_used = (pends[-1] // FFN_BLOCK).astype(jnp.int32).reshape(1)
        blk_start = jnp.arange(nblk, dtype=jnp.int32) * FFN_BLOCK
        blk_exp = jnp.minimum(jnp.sum((pends[None, :] <= blk_start[:, None]).astype(jnp.int32), axis=1),
                              N_EXPERTS - 1)
        xs = _dispatch(dst, h2.reshape(T * ROW_TILE, LANES), slot_buf, tmc)
        y = _expert_ffn(blk_exp, n_used, xs, exp_w1, exp_w3, exp_w2, l)
        slot_buf = y
        x = _combine(dst, x1, route, g2, row(final_g), y, tmc, final_norm=(l == L - 1))
    return x
```

```python
import functools

import jax
import jax.numpy as jnp
from jax import lax
from jax.experimental import pallas as pl
from jax.experimental.pallas import tpu as pltpu

F32 = jnp.float32
BF16 = jnp.bfloat16
HIGHEST = lax.Precision.HIGHEST

EPS = 1e-6
CHUNK = 64
CONV_WIDTH = 31
C_CONV = 256
C_POOL = 256
POOL_GROUP = 64
N_HEADS = 8
HEAD_DIM = 64
D_Q_LAT = 256
D_KV_LAT = 128
IDX_HEADS = 8
IDX_DIM = 64
TOPK_MAX = 256
N_GROUPS = 4
EXPERTS_PER_GROUP = 8
N_EXPERTS = 32
D_EXPERT = 512
N_IN = 1736
N_IN_PAD = 1792
KW_OFF = 1408
POOL_OFF = 1536
LANES = 128
HALO = 32
CONV_TILE = 64
FFN_BLOCK = 256
ROW_DMA_UNROLL = 8
ROW_TILE = 8
ATTN_BATCH = 2
ATTN_BATCH_MAX_KEYS = 1024
LOG2E = 1.4426950408889634
VMEM_LIMIT = 56 * 1024 * 1024


def _cparams(sem, row_dma=False):
    return pltpu.CompilerParams(dimension_semantics=sem, vmem_limit_bytes=VMEM_LIMIT,
                                disable_bounds_checks=row_dma)


def _mod_kernel(c_ref, w_ref, b_ref, o_ref):
    c = c_ref[...]
    cond = c * jax.nn.sigmoid(c)
    o_ref[0] = jnp.dot(cond, w_ref[0], precision=HIGHEST, preferred_element_type=F32) + b_ref[0]


def _modulation(c, mod_w, mod_b):
    L, D, D6 = mod_w.shape
    B = c.shape[0]
    nj = D6 // D
    return pl.pallas_call(
        _mod_kernel,
        grid=(L, nj),
        in_specs=[
            pl.BlockSpec((B, D), lambda l, j: (0, 0)),
            pl.BlockSpec((1, D, D), lambda l, j: (l, 0, j)),
            pl.BlockSpec((1, 1, D), lambda l, j: (l, 0, j)),
        ],
        out_specs=pl.BlockSpec((1, B, D), lambda l, j: (l, 0, j)),
        out_shape=jax.ShapeDtypeStruct((L, B, D6), F32),
        compiler_params=_cparams(("arbitrary", "arbitrary")),
        name="modulation",
    )(c, mod_w, mod_b.reshape(L, 1, D6))


def _fold_kernel(uq_ref, ukt_ref, uv_ref, wo_ref, wq_ref, wvo_ref):
    wq_ref[0] = jnp.dot(uq_ref[0, 0], ukt_ref[0, 0], precision=HIGHEST,
                        preferred_element_type=F32).astype(BF16)
    wvo_ref[0] = jnp.dot(uv_ref[0, 0], wo_ref[0, 0], precision=HIGHEST,
                         preferred_element_type=F32).astype(BF16)


def _fold_weights(w_uq, w_uk, w_uv, w_out):
    L = w_uq.shape[0]
    D = w_out.shape[-1]
    uq = jnp.transpose(w_uq, (0, 2, 1, 3))
    ukt = jnp.transpose(w_uk, (0, 2, 3, 1))
    uv = jnp.transpose(w_uv, (0, 2, 1, 3))
    wo = w_out[:, C_CONV:C_CONV + N_HEADS * HEAD_DIM, :].reshape(L, N_HEADS, HEAD_DIM, D)
    return pl.pallas_call(
        _fold_kernel,
        grid=(L, N_HEADS),
        in_specs=[
            pl.BlockSpec((1, 1, D_Q_LAT, HEAD_DIM), lambda l, h: (l, h, 0, 0)),
            pl.BlockSpec((1, 1, HEAD_DIM, D_KV_LAT), lambda l, h: (l, h, 0, 0)),
            pl.BlockSpec((1, 1, D_KV_LAT, HEAD_DIM), lambda l, h: (l, h, 0, 0)),
            pl.BlockSpec((1, 1, HEAD_DIM, D), lambda l, h: (l, h, 0, 0)),
        ],
        out_specs=[
            pl.BlockSpec((1, D_Q_LAT, D_KV_LAT), lambda l, h: (l, 0, h)),
            pl.BlockSpec((1, D_KV_LAT, D), lambda l, h: (l, h, 0)),
        ],
        out_shape=[
            jax.ShapeDtypeStruct((L, D_Q_LAT, N_HEADS * D_KV_LAT), BF16),
            jax.ShapeDtypeStruct((L, N_HEADS * D_KV_LAT, D), BF16),
        ],
        compiler_params=_cparams(("arbitrary", "arbitrary")),
        name="fold_weights",
    )(uq, ukt, uv, wo)


def _rms(v):
    return v * lax.rsqrt(jnp.mean(v * v, axis=-1, keepdims=True) + EPS)


def _in_kernel(x_ref, sc_ref, sh_ref, g_ref, w_ref, qg_ref, kvg_ref, wq_ref,
               cu_ref, qabs_ref, ckv_ref, qi_ref, kw_ref):
    x = x_ref[0]
    h = _rms(x) * g_ref[...] * (1.0 + sc_ref[0]) + sh_ref[0]
    z = jnp.dot(h.astype(BF16), w_ref[...], preferred_element_type=F32)
    a_val = z[:, 0:C_CONV]
    a_gate = z[:, C_CONV:2 * C_CONV]
    cu_ref[0, :, 0:C_CONV] = a_val * jax.nn.sigmoid(a_gate)
    cu_ref[0, :, C_CONV:C_CONV + C_POOL] = z[:, POOL_OFF:POOL_OFF + C_POOL]
    o = 2 * C_CONV
    cq = _rms(z[:, o:o + D_Q_LAT]) * qg_ref[...]
    qabs = jnp.dot(cq.astype(BF16), wq_ref[...], preferred_element_type=F32)
    qabs_ref[0] = (qabs * (HEAD_DIM ** -0.5 * LOG2E)).astype(BF16)
    o += D_Q_LAT
    ckv_ref[0] = (_rms(z[:, o:o + D_KV_LAT]) * kvg_ref[...]).astype(BF16)
    o += D_KV_LAT
    qi_ref[0] = z[:, o:o + IDX_HEADS * IDX_DIM].astype(BF16)
    kw_ref[0] = z[:, KW_OFF:KW_OFF + LANES]


def _in_proj(x, sc, sh, g, w_in_p, qg, kvg, wq, tm):
    B, S, D = x.shape
    tok = lambda n: pl.BlockSpec((1, tm, n), lambda b, i: (b, i, 0))
    per_b = pl.BlockSpec((1, 1, D), lambda b, i: (b, 0, 0))
    full = lambda a: pl.BlockSpec(a.shape, lambda b, i: (0,) * a.ndim)
    nq = IDX_HEADS * IDX_DIM
    return pl.pallas_call(
        _in_kernel,
        grid=(B, S // tm),
        in_specs=[tok(D), per_b, per_b, full(g), full(w_in_p), full(qg), full(kvg), full(wq)],
        out_specs=[tok(C_CONV + C_POOL), tok(N_HEADS * D_KV_LAT), tok(D_KV_LAT), tok(nq), tok(LANES)],
        out_shape=[
            jax.ShapeDtypeStruct((B, S, C_CONV + C_POOL), F32),
            jax.ShapeDtypeStruct((B, S, N_HEADS * D_KV_LAT), BF16),
            jax.ShapeDtypeStruct((B, S, D_KV_LAT), BF16),
            jax.ShapeDtypeStruct((B, S, nq), BF16),
            jax.ShapeDtypeStruct((B, S, LANES), F32),
        ],
        compiler_params=_cparams(("arbitrary", "arbitrary")),
        name="in_proj",
    )(x, sc, sh, g, w_in_p, qg, kvg, wq)


def _cp_kernel(cu_ref, ck_ref, cb_ref, lg_ref, lb_ref, pw_ref, ps_ref, o_ref, pad_ref):
    S = cu_ref.shape[1]
    TT = CONV_TILE
    pad_ref[0:HALO, :] = jnp.zeros((HALO, C_CONV + C_POOL), F32)
    pad_ref[HALO:HALO + S, :] = cu_ref[0]
    lane = lax.broadcasted_iota(jnp.int32, (TT, C_POOL), 1)
    row = lax.broadcasted_iota(jnp.int32, (TT, C_POOL), 0)
    win_len = jnp.where(lane < POOL_GROUP, 2,
                        jnp.where(lane < 2 * POOL_GROUP, 4, jnp.where(lane < 3 * POOL_GROUP, 8, 16)))

    def tile(i, carry):
        t0 = pl.multiple_of(i * TT, TT)
        win = pad_ref[pl.ds(t0, TT + HALO), :]
        wc = win[:, 0:C_CONV]
        wp = win[:, C_CONV:C_CONV + C_POOL]
        acc = jnp.zeros((TT, C_CONV), F32)
        for s in range(8):
            ws = wc if s == 0 else pltpu.roll(wc, TT + HALO - s, 0)
            for j in range(CONV_WIDTH):
                o = HALO - (CONV_WIDTH - 1) + j
                if o % 8 == s:
                    acc = acc + ck_ref[j:j + 1, :] * ws[o - s:o - s + TT, :]
        v = acc + cb_ref[...]
        mu = jnp.mean(v, axis=-1, keepdims=True)
        vc = v - mu
        var = jnp.mean(vc * vc, axis=-1, keepdims=True)
        yn = vc * lax.rsqrt(var + EPS) * lg_ref[...] + lb_ref[...]
        o_ref[0, pl.ds(t0, TT), 0:C_CONV] = (yn * jax.nn.sigmoid(yn)).astype(BF16)
        lo = HALO - 15
        s2 = wp[lo + 1:, :] + wp[lo:-1, :]
        s4 = s2[2:, :] + s2[:-2, :]
        s8 = s4[4:, :] + s4[:-4, :]
        s16 = s8[8:, :] + s8[:-8, :]
        u = wp[HALO:, :]
        sw = jnp.where(lane < POOL_GROUP, s2[14:, :],
                       jnp.where(lane < 2 * POOL_GROUP, s4[12:, :],
                                 jnp.where(lane < 3 * POOL_GROUP, s8[8:, :], s16)))
        n_pos = jnp.minimum(t0 + row + 1, win_len).astype(F32)
        d = (sw / n_pos - u).astype(BF16)
        yp = jnp.dot(d, pw_ref[...], preferred_element_type=F32) * ps_ref[...]
        o_ref[0, pl.ds(t0, TT), C_CONV:C_CONV + C_POOL] = yp.astype(BF16)
        return carry

    lax.fori_loop(0, S // TT, tile, 0)


def _conv_pool(cu, ck, cb, lg, lb, pw_bd, ps):
    B, S, W = cu.shape
    full = lambda a: pl.BlockSpec(a.shape, lambda b: (0,) * a.ndim)
    return pl.pallas_call(
        _cp_kernel,
        grid=(B,),
        in_specs=[pl.BlockSpec((1, S, W), lambda b: (b, 0, 0)),
                  full(ck), full(cb), full(lg), full(lb), full(pw_bd), full(ps)],
        out_specs=pl.BlockSpec((1, S, W), lambda b: (b, 0, 0)),
        out_shape=jax.ShapeDtypeStruct((B, S, W), BF16),
        scratch_shapes=[pltpu.VMEM((S + HALO, W), F32)],
        compiler_params=_cparams(("arbitrary",)),
        name="conv_pool",
    )(cu, ck, cb, lg, lb, pw_bd, ps)


def _attn_kernel(qi_ref, kwq_ref, kwk_ref, qabs_ref, ckv_ref, o_ref, key_ref, aux_ref, bias_ref, *, topk, q0):
    G, QB = qi_ref.shape[0], qi_ref.shape[1]
    LK = ckv_ref.shape[1]
    R = G * QB
    dn_nt = (((1,), (1,)), ((), ()))

    def visible(rows):
        kidx = lax.broadcasted_iota(jnp.int32, (rows, LK), 1)
        qpos = q0 + lax.broadcasted_iota(jnp.int32, (rows, 1), 0) % QB
        return kidx, kidx < (qpos // CHUNK + 1) * CHUNK

    if LK > topk:
        _, allowed = visible(QB)
        for g in range(G):
            ki = kwk_ref[g, :, 0:IDX_DIM].astype(BF16)
            wi = kwq_ref[g, :, IDX_DIM:IDX_DIM + IDX_HEADS]
            qi = qi_ref[g]
            score = jnp.zeros((QB, LK), F32)
            for h in range(IDX_HEADS):
                rel = lax.dot_general(qi[:, h * IDX_DIM:(h + 1) * IDX_DIM], ki, dn_nt,
                                      preferred_element_type=F32)
                score = score + jnp.maximum(rel, 0.0) * wi[:, h:h + 1]
            score = score * ((IDX_HEADS * IDX_DIM) ** -0.5)
            score = jnp.where(score == 0.0, 0.0, score)
            score = jnp.where(allowed, score, -jnp.inf)
            bits = lax.bitcast_convert_type(score, jnp.int32)
            key_ref[g * QB:(g + 1) * QB, :] = bits ^ ((bits >> 31) & jnp.int32(0x7FFFFFFF))

        kf = jnp.float32(topk)

        H = R // 2

        def partial_counts(r0, cand):
            acc = jnp.zeros((H, LANES), F32)
            for c in range(LK // LANES):
                k = key_ref[r0:r0 + H, c * LANES:(c + 1) * LANES]
                acc = acc + jnp.where(k >= cand, 1.0, 0.0)
            return acc

        def settle(acc, cand, pre):
            return jnp.where(jnp.sum(acc, axis=-1, keepdims=True) >= kf, cand, pre)

        def bit(i):
            return jnp.where(i <= 31, jnp.int32(1) << jnp.maximum(31 - i, 0), 0)

        def bit_step(i, carry):
            pre_a, pre_b, acc_b = carry
            cand_a = pre_a + bit(i)
            acc_a = partial_counts(0, cand_a)
            pre_b = settle(acc_b, pre_b + bit(i), pre_b)
            acc_b = partial_counts(H, pre_b + bit(i + 1))
            pre_a = settle(acc_a, cand_a, pre_a)
            return pre_a, pre_b, acc_b

        lowest = jnp.full((H, 1), jnp.iinfo(jnp.int32).min, jnp.int32)
        pre_a, pre_b, _ = lax.fori_loop(0, 32, bit_step,
                                        (lowest, lowest, partial_counts(H, lowest + bit(0))))
        thr = jnp.concatenate([pre_a, pre_b], axis=0)
        reached = key_ref[...] >= thr
        bias_ref[...] = jnp.where(reached, 0.0, -jnp.inf)
        has_excess = jnp.max(jnp.sum(jnp.where(reached, 1.0, 0.0), axis=-1, keepdims=True)) > kf

        @pl.when(has_excess)
        def _():
            kidx, allowed_r = visible(R)
            key = key_ref[...]
            n_gt = jnp.sum(jnp.where(key > thr, 1.0, 0.0), axis=-1, keepdims=True)
            need = kf - n_gt
            aux_ref[...] = jnp.where(key == thr, kidx, jnp.int32(LK))
            nbits = (LK - 1).bit_length()

            def idx_step(i, p):
                cand = p | (jnp.int32(1) << (nbits - 1 - i))
                cnt = jnp.sum(jnp.where(aux_ref[...] < cand, 1.0, 0.0), axis=-1, keepdims=True)
                return jnp.where(cnt < need, cand, p)

            cut = lax.fori_loop(0, nbits, idx_step, jnp.zeros((R, 1), jnp.int32))
            sel = ((key_ref[...] > thr) | (aux_ref[...] <= cut)) & allowed_r
            bias_ref[...] = jnp.where(sel, 0.0, -jnp.inf)

    for g in range(G):
        if LK > topk:
            bias = bias_ref[g * QB:(g + 1) * QB, :]
        else:
            bias = jnp.where(visible(QB)[1], 0.0, -jnp.inf)
        ckv = ckv_ref[g]
        for h in range(N_HEADS):
            qa = qabs_ref[g, :, h * D_KV_LAT:(h + 1) * D_KV_LAT]
            logit = lax.dot_general(qa, ckv, dn_nt, preferred_element_type=F32) + bias
            m = jnp.max(logit, axis=-1, keepdims=True)
            p = jnp.exp2(logit - m)
            l = jnp.sum(p, axis=-1, keepdims=True)
            ctx = jnp.dot(p.astype(BF16), ckv, preferred_element_type=F32)
            o_ref[g, :, h * D_KV_LAT:(h + 1) * D_KV_LAT] = (ctx / l).astype(BF16)


def _attention(qi, kw, qabs, ckv, qb):
    B, S, _ = ckv.shape
    topk = min(TOPK_MAX, S // 4)
    buf = qabs
    for j in range(S // qb):
        lk = (j + 1) * qb
        G = ATTN_BATCH if (B % ATTN_BATCH == 0 and lk <= ATTN_BATCH_MAX_KEYS) else 1
        blk = lambda n, j=j: pl.BlockSpec((G, qb, n), lambda b: (b, j, 0))
        keys = lambda n, lk=lk: pl.BlockSpec((G, lk, n), lambda b: (b, 0, 0))
        buf = pl.pallas_call(
            functools.partial(_attn_kernel, topk=topk, q0=j * qb),
            grid=(B // G,),
            in_specs=[blk(IDX_HEADS * IDX_DIM), blk(LANES), keys(LANES), blk(N_HEADS * D_KV_LAT),
                      keys(D_KV_LAT)],
            out_specs=blk(N_HEADS * D_KV_LAT),
            out_shape=jax.ShapeDtypeStruct(buf.shape, BF16),
            scratch_shapes=[pltpu.VMEM((G * qb, lk), jnp.int32), pltpu.VMEM((G * qb, lk), jnp.int32),
                            pltpu.VMEM((G * qb, lk), F32)],
            input_output_aliases={3: 0},
            compiler_params=_cparams(("arbitrary",)),
            name=f"dsa_attention_q{j}",
        )(qi, kw, kw, buf, ckv)
    return buf


def _out_kernel(ycp_ref, ctx_ref, x_ref, g1_ref, sc_ref, sh_ref, ng_ref, wcp_ref, wvo_ref,
                rw_ref, rb_ref, tri_ref, x1_ref, h2_ref, route_ref, cnt_ref, carry_ref):
    first = (pl.program_id(0) == 0) & (pl.program_id(1) == 0)

    @pl.when(first)
    def _():
        carry_ref[...] = jnp.zeros_like(carry_ref)

    mix = jnp.dot(ycp_ref[0], wcp_ref[...], preferred_element_type=F32)
    mix = mix + jnp.dot(ctx_ref[0], wvo_ref[...], preferred_element_type=F32)
    x1 = x_ref[0] + g1_ref[0] * mix
    x1_ref[0] = x1
    h2 = _rms(x1) * ng_ref[...] * (1.0 + sc_ref[0]) + sh_ref[0]
    _rows_to_tiles(h2_ref.at[0], h2)
    rw = rw_ref[...]
    rw_hi = rw.astype(BF16)
    rw_lo = (rw - rw_hi.astype(F32)).astype(BF16)
    h_hi = h2.astype(BF16)
    h_lo = (h2 - h_hi.astype(F32)).astype(BF16)
    logits = (jnp.dot(h_hi, rw_hi, preferred_element_type=F32)
              + (jnp.dot(h_lo, rw_hi, preferred_element_type=F32)
                 + jnp.dot(h_hi, rw_lo, preferred_element_type=F32))) + rb_ref[...]

    lt = logits.T
    tm = lt.shape[1]
    rowi = lax.broadcasted_iota(jnp.int32, (LANES, tm), 0)
    rowf = rowi.astype(F32)
    ninf = -jnp.inf
    big = jnp.float32(LANES)
    glog = jnp.where(rowi < N_GROUPS, lt, ninf)
    gmax = jnp.max(glog, axis=0, keepdims=True)
    gidx = jnp.min(jnp.where(glog == gmax, rowf, big), axis=0, keepdims=True)
    g_p = 1.0 / jnp.sum(jnp.exp(glog - gmax), axis=0, keepdims=True)
    lo = N_GROUPS + EXPERTS_PER_GROUP * gidx
    elog = jnp.where((rowf >= lo) & (rowf < lo + EXPERTS_PER_GROUP), lt, ninf)
    m1 = jnp.max(elog, axis=0, keepdims=True)
    i1 = jnp.min(jnp.where(elog == m1, rowf, big), axis=0, keepdims=True)
    elog2 = jnp.where(rowf == i1, ninf, elog)
    m2 = jnp.max(elog2, axis=0, keepdims=True)
    i2 = jnp.min(jnp.where(elog2 == m2, rowf, big), axis=0, keepdims=True)
    r = jnp.exp(m2 - m1)
    gate1 = g_p / (1.0 + r)
    gate2 = g_p * r / (1.0 + r)
    e1 = i1 - N_GROUPS
    e2 = i2 - N_GROUPS
    oh1 = rowf == e1
    oh2 = rowf == e2
    oh1f = jnp.where(oh1, 1.0, 0.0)
    oh2f = jnp.where(oh2, 1.0, 0.0)
    pre1 = jnp.dot(oh1f.astype(BF16), tri_ref[...], preferred_element_type=F32)
    pre2 = jnp.dot(oh2f.astype(BF16), tri_ref[...], preferred_element_type=F32)
    carry = carry_ref[:, 0:1]
    cnt1 = jnp.sum(oh1f, axis=1, keepdims=True)
    cnt2 = jnp.sum(oh2f, axis=1, keepdims=True)
    rank1 = jnp.sum(jnp.where(oh1, carry + pre1, 0.0), axis=0, keepdims=True)
    rank2 = jnp.sum(jnp.where(oh2, carry + cnt1 + pre2, 0.0), axis=0, keepdims=True)
    total = carry + cnt1 + cnt2
    carry_ref[...] = jnp.broadcast_to(total, carry_ref.shape)
    cnt_ref[...] = jnp.broadcast_to(total, cnt_ref.shape)
    route_t = jnp.where(rowi == 0, e1, jnp.where(rowi == 1, e2, jnp.where(rowi == 2, gate1, jnp.where(
        rowi == 3, gate2, jnp.where(rowi == 4, rank1, jnp.where(rowi == 5, rank2, 0.0))))))
    route_ref[0] = route_t.T


def _out_proj(ycp, ctx, x, g1, sc2, sh2, ng, wcp, wvo, rw, rb, tm):
    B, S, D = x.shape
    tok = lambda n: pl.BlockSpec((1, tm, n), lambda b, i: (b, i, 0))
    per_b = pl.BlockSpec((1, 1, D), lambda b, i: (b, 0, 0))
    full = lambda a: pl.BlockSpec(a.shape, lambda b, i: (0,) * a.ndim)
    tri = jnp.triu(jnp.ones((tm, tm), BF16), 1)
    return pl.pallas_call(
        _out_kernel,
        grid=(B, S // tm),
        in_specs=[tok(C_CONV + C_POOL), tok(N_HEADS * D_KV_LAT), tok(D), per_b, per_b, per_b,
                  full(ng), full(wcp), full(wvo), full(rw), full(rb), full(tri)],
        out_specs=[tok(D), pl.BlockSpec((1, tm * ROW_TILE, LANES), lambda b, i: (b, i, 0)), tok(LANES),
                   pl.BlockSpec((LANES, LANES), lambda b, i: (0, 0))],
        out_shape=[
            jax.ShapeDtypeStruct((B, S, D), F32),
            jax.ShapeDtypeStruct((B, S * ROW_TILE, LANES), F32),
            jax.ShapeDtypeStruct((B, S, LANES), F32),
            jax.ShapeDtypeStruct((LANES, LANES), F32),
        ],
        scratch_shapes=[pltpu.VMEM((LANES, LANES), F32)],
        compiler_params=_cparams(("arbitrary", "arbitrary")),
        name="out_proj_router",
    )(ycp, ctx, x, g1, sc2, sh2, ng, wcp, wvo, rw, rb, tri)


def _row_copy(src, i, dst, k, sem):
    return pltpu.make_async_copy(src.at[pl.ds(pl.multiple_of(i * ROW_TILE, ROW_TILE), ROW_TILE), :],
                                 dst.at[pl.ds(pl.multiple_of(k * ROW_TILE, ROW_TILE), ROW_TILE), :], sem)


def _rows_to_tiles(ref, val):
    n = val.shape[0]
    for s in range(ROW_TILE):
        ref[pl.ds(s, n, stride=ROW_TILE), :] = val[:, s * LANES:(s + 1) * LANES]


def _tiles_to_rows(ref, n):
    return jnp.concatenate([ref[pl.ds(s, n, stride=ROW_TILE), :] for s in range(ROW_TILE)], axis=-1)


DISPATCH_SLOTS = 3


def _dispatch_kernel(dst_ref, dst_prev_ref, h_ref, xs_in_ref, xs_ref, hbuf, load_sem, sem):
    del xs_in_ref
    rows = hbuf.shape[1]
    tm = rows // ROW_TILE
    i = pl.program_id(0)
    n = pl.num_programs(0)
    slot = i % DISPATCH_SLOTS

    def load(t, s):
        return pltpu.make_async_copy(h_ref.at[pl.ds(pl.multiple_of(t * rows, rows), rows), :], hbuf.at[s],
                                     load_sem.at[s])

    @pl.when(i == 0)
    def _():
        load(0, 0).start()

    @pl.when(i + 1 < n)
    def _():
        load(i + 1, (i + 1) % DISPATCH_SLOTS).start()

    load(i, slot).wait()

    def copies(idx_ref, s, r):
        return (_row_copy(hbuf.at[s], r, xs_ref, idx_ref[2 * r], sem.at[s]),
                _row_copy(hbuf.at[s], r, xs_ref, idx_ref[2 * r + 1], sem.at[s]))

    def issue(r, c):
        for prio, cp in enumerate(copies(dst_ref, slot, r)):
            cp.start(priority=prio)
        return c

    lax.fori_loop(0, tm, issue, 0, unroll=ROW_DMA_UNROLL)

    def drain(idx_ref, s):
        def body(r, c):
            for cp in copies(idx_ref, s, r):
                cp.wait()
            return c
        lax.fori_loop(0, tm, body, 0, unroll=ROW_DMA_UNROLL)

    @pl.when(i >= 1)
    def _():
        drain(dst_prev_ref, (i + DISPATCH_SLOTS - 1) % DISPATCH_SLOTS)

    @pl.when(i == n - 1)
    def _():
        drain(dst_ref, slot)


def _dispatch(dst_flat, h2, xs0, tm):
    T = h2.shape[0] // ROW_TILE
    return pl.pallas_call(
        _dispatch_kernel,
        grid=(T // tm,),
        in_specs=[pl.BlockSpec((2 * tm,), lambda i: (i,), memory_space=pltpu.SMEM),
                  pl.BlockSpec((2 * tm,), lambda i: (jnp.maximum(i - 1, 0),), memory_space=pltpu.SMEM),
                  pl.BlockSpec(memory_space=pl.ANY),
                  pl.BlockSpec(memory_space=pl.ANY)],
        out_specs=pl.BlockSpec(memory_space=pl.ANY),
        out_shape=jax.ShapeDtypeStruct(xs0.shape, F32),
        scratch_shapes=[pltpu.VMEM((DISPATCH_SLOTS, tm * ROW_TILE, LANES), F32),
                        pltpu.SemaphoreType.DMA((DISPATCH_SLOTS,)),
                        pltpu.SemaphoreType.DMA((DISPATCH_SLOTS,))],
        input_output_aliases={3: 0},
        compiler_params=_cparams(("arbitrary",), row_dma=True),
        name="moe_dispatch",
    )(dst_flat, dst_flat, h2, xs0)


def _ffn_kernel(be_ref, nu_ref, xs_ref, w1_ref, w3_ref, w2_ref, y_ref, w13_scr, w2_scr):
    i = pl.program_id(0)
    new_expert = (i == 0) | (be_ref[i] != be_ref[jnp.maximum(i - 1, 0)])

    @pl.when(new_expert)
    def _():
        w13_scr[:, 0:D_EXPERT] = w1_ref[0, 0].astype(BF16)
        w13_scr[:, D_EXPERT:2 * D_EXPERT] = w3_ref[0, 0].astype(BF16)
        w2_scr[...] = w2_ref[0, 0].astype(BF16)

    @pl.when(i < nu_ref[0])
    def _():
        xb = _tiles_to_rows(xs_ref, FFN_BLOCK).astype(BF16)
        h13 = jnp.dot(xb, w13_scr[...], preferred_element_type=F32)
        h1 = h13[:, 0:D_EXPERT]
        h3 = h13[:, D_EXPERT:2 * D_EXPERT]
        act = (h1 * jax.nn.sigmoid(h1)) * h3
        _rows_to_tiles(y_ref, jnp.dot(act.astype(BF16), w2_scr[...], preferred_element_type=F32))

    @pl.when(i >= nu_ref[0])
    def _():
        y_ref[...] = jnp.zeros_like(y_ref)


def _expert_ffn(blk_exp, n_used, xs, w1, w3, w2, layer):
    D = w1.shape[2]
    rows_blk = FFN_BLOCK * ROW_TILE
    nblk = xs.shape[0] // rows_blk
    grid_spec = pltpu.PrefetchScalarGridSpec(
        num_scalar_prefetch=2,
        grid=(nblk,),
        in_specs=[
            pl.BlockSpec((rows_blk, LANES), lambda i, be, nu: (jnp.maximum(jnp.minimum(i, nu[0] - 1), 0), 0)),
            pl.BlockSpec((1, 1, D, D_EXPERT), lambda i, be, nu: (layer, be[i], 0, 0)),
            pl.BlockSpec((1, 1, D, D_EXPERT), lambda i, be, nu: (layer, be[i], 0, 0)),
            pl.BlockSpec((1, 1, D_EXPERT, D), lambda i, be, nu: (layer, be[i], 0, 0)),
        ],
        out_specs=pl.BlockSpec((rows_blk, LANES), lambda i, be, nu: (i, 0)),
        scratch_shapes=[pltpu.VMEM((D, 2 * D_EXPERT), BF16), pltpu.VMEM((D_EXPERT, D), BF16)],
    )
    return pl.pallas_call(
        _ffn_kernel,
        grid_spec=grid_spec,
        out_shape=jax.ShapeDtypeStruct(xs.shape, F32),
        compiler_params=_cparams(("arbitrary",)),
        name="expert_ffn",
    )(blk_exp, n_used, xs, w1, w3, w2)


def _combine_kernel(dst_ref, x1_ref, route_ref, g2_ref, fg_ref, y_ref, o_ref, ya_ref, yb_ref, sem,
                    *, final_norm):
    tm = x1_ref.shape[1]

    def issue(r, c):
        _row_copy(y_ref, dst_ref[2 * r], ya_ref, r, sem.at[0]).start(priority=0)
        _row_copy(y_ref, dst_ref[2 * r + 1], yb_ref, r, sem.at[1]).start(priority=1)
        return c

    lax.fori_loop(0, tm, issue, 0, unroll=ROW_DMA_UNROLL)

    def drain(r, c):
        _row_copy(y_ref, dst_ref[2 * r], ya_ref, r, sem.at[0]).wait()
        _row_copy(y_ref, dst_ref[2 * r + 1], yb_ref, r, sem.at[1]).wait()
        return c

    lax.fori_loop(0, tm, drain, 0, unroll=ROW_DMA_UNROLL)

    route = route_ref[0]
    moe = _tiles_to_rows(ya_ref, tm) * route[:, 2:3] + _tiles_to_rows(yb_ref, tm) * route[:, 3:4]
    out = x1_ref[0] + g2_ref[0] * moe
    if final_norm:
        out = _rms(out) * fg_ref[...]
    o_ref[0] = out


def _combine(dst_flat, x1, route, g2, fg, y, tm, final_norm):
    B, S, D = x1.shape
    nt = S // tm
    tok = lambda n: pl.BlockSpec((1, tm, n), lambda b, i: (b, i, 0))
    return pl.pallas_call(
        functools.partial(_combine_kernel, final_norm=final_norm),
        grid=(B, nt),
        in_specs=[pl.BlockSpec((2 * tm,), lambda b, i: (b * nt + i,), memory_space=pltpu.SMEM),
                  tok(D), tok(LANES),
                  pl.BlockSpec((1, 1, D), lambda b, i: (b, 0, 0)),
                  pl.BlockSpec((1, D), lambda b, i: (0, 0)),
                  pl.BlockSpec(memory_space=pl.ANY)],
        out_specs=tok(D),
        out_shape=jax.ShapeDtypeStruct((B, S, D), F32),
        scratch_shapes=[pltpu.VMEM((tm * ROW_TILE, LANES), F32), pltpu.VMEM((tm * ROW_TILE, LANES), F32),
                        pltpu.SemaphoreType.DMA((2,))],
        compiler_params=_cparams(("arbitrary", "arbitrary"), row_dma=True),
        name="moe_combine",
    )(dst_flat, x1, route, g2, fg, y)


def _pick_tile(n, pref):
    t = min(n, pref)
    while n % t:
        t //= 2
    return t


def kernel(x, c, mod_w, mod_b, norm1_g, w_in, conv_k, conv_b, conv_ln_g, conv_ln_b, q_norm_g, kv_norm_g, w_uq, w_uk, w_uv, pool_w, pool_scale, w_out, norm2_g, router_g_w, router_g_b, router_e_w, router_e_b, exp_w1, exp_w3, exp_w2, final_g):
    B, S, D = x.shape
    L = mod_w.shape[0]
    T = B * S
    tm = _pick_tile(S, 512)
    qb = _pick_tile(S, 256)
    tmc = _pick_tile(S, 512)
    n_rows = 2 * T + N_EXPERTS * FFN_BLOCK
    nblk = n_rows // FFN_BLOCK

    mod = _modulation(c, mod_w, mod_b)
    wq_all, wvo_all = _fold_weights(w_uq, w_uk, w_uv, w_out)
    row = lambda a: a.reshape(1, -1)

    assert D == ROW_TILE * LANES
    slot_buf = jnp.zeros((n_rows * ROW_TILE, LANES), F32)
    for l in range(L):
        sh1, sc1, g1, sh2, sc2, g2 = [mod[l, :, k * D:(k + 1) * D].reshape(B, 1, D) for k in range(6)]
        w_in_p = jnp.concatenate(
            [w_in[l, :, :N_IN - C_POOL], jnp.zeros((D, N_IN_PAD - N_IN), F32), w_in[l, :, N_IN - C_POOL:]],
            axis=1).astype(BF16)
        cu, qabs, ckv, qi, kw = _in_proj(x, sc1, sh1, row(norm1_g[l]), w_in_p, row(q_norm_g[l]),
                                         row(kv_norm_g[l]), wq_all[l], tm)
        pw_bd = jax.scipy.linalg.block_diag(*[pool_w[l, g] for g in range(pool_w.shape[1])]).astype(BF16)
        ycp = _conv_pool(cu, conv_k[l], row(conv_b[l]), row(conv_ln_g[l]), row(conv_ln_b[l]),
                         pw_bd, row(pool_scale[l]))
        ctx = _attention(qi, kw, qabs, ckv, qb)
        wcp = jnp.concatenate([w_out[l, :C_CONV], w_out[l, D - C_POOL:]], axis=0).astype(BF16)
        n_r = N_GROUPS + N_EXPERTS
        rw = jnp.concatenate([router_g_w[l], router_e_w[l].reshape(D, N_EXPERTS),
                              jnp.zeros((D, LANES - n_r), F32)], axis=1)
        rb = jnp.concatenate([router_g_b[l], router_e_b[l].reshape(N_EXPERTS),
                              jnp.zeros((LANES - n_r,), F32)]).reshape(1, LANES)
        x1, h2, route, cnt = _out_proj(ycp, ctx, x, g1, sc2, sh2, row(norm2_g[l]), wcp, wvo_all[l],
                                       rw, rb, tm)
        counts = cnt[:N_EXPERTS, 0].astype(jnp.int32)
        pcounts = (counts + FFN_BLOCK - 1) // FFN_BLOCK * FFN_BLOCK
        pends = jnp.cumsum(pcounts)
        pstarts = pends - pcounts
        route2 = route.reshape(T, LANES)
        e_idx = route2[:, 0:2].astype(jnp.int32)
        e_start = jnp.sum(jnp.where(e_idx[..., None] == jnp.arange(N_EXPERTS, dtype=jnp.int32), pstarts, 0),
                          axis=-1)
        dst = (e_start + route2[:, 4:6].astype(jnp.int32)).reshape(2 * T)
        n_used = (pends[-1] // FFN_BLOCK).astype(jnp.int32).reshape(1)
        blk_start = jnp.arange(nblk, dtype=jnp.int32) * FFN_BLOCK
        blk_exp = jnp.minimum(jnp.sum((pends[None, :] <= blk_start[:, None]).astype(jnp.int32), axis=1),
                              N_EXPERTS - 1)
        xs = _dispatch(dst, h2.reshape(T * ROW_TILE, LANES), slot_buf, tmc)
        y = _expert_ffn(blk_exp, n_used, xs, exp_w1, exp_w3, exp_w2, l)
        slot_buf = y
        x = _combine(dst, x1, route, g2, row(final_g), y, tmc, final_norm=(l == L - 1))
    return x
```

```python
import functools

import jax
import jax.numpy as jnp
from jax import lax
from jax.experimental import pallas as pl
from jax.experimental.pallas import tpu as pltpu

F32 = jnp.float32
BF16 = jnp.bfloat16
HIGHEST = lax.Precision.HIGHEST

EPS = 1e-6
CHUNK = 64
CONV_WIDTH = 31
C_CONV = 256
C_POOL = 256
POOL_GROUP = 64
N_HEADS = 8
HEAD_DIM = 64
D_Q_LAT = 256
D_KV_LAT = 128
IDX_HEADS = 8
IDX_DIM = 64
TOPK_MAX = 256
N_GROUPS = 4
EXPERTS_PER_GROUP = 8
N_EXPERTS = 32
D_EXPERT = 512
N_IN = 1736
N_IN_PAD = 1792
KW_OFF = 1408
POOL_OFF = 1536
LANES = 128
HALO = 32
CONV_TILE = 64
FFN_BLOCK = 256
ROW_DMA_UNROLL = 8
ROW_TILE = 8
ATTN_BATCH = 2
ATTN_BATCH_MAX_KEYS = 1024
LOG2E = 1.4426950408889634
VMEM_LIMIT = 56 * 1024 * 1024


def _cparams(sem, row_dma=False):
    return pltpu.CompilerParams(dimension_semantics=sem, vmem_limit_bytes=VMEM_LIMIT,
                                disable_bounds_checks=row_dma)


def _mod_kernel(c_ref, w_ref, b_ref, o_ref):
    c = c_ref[...]
    cond = c * jax.nn.sigmoid(c)
    o_ref[0] = jnp.dot(cond, w_ref[0], precision=HIGHEST, preferred_element_type=F32) + b_ref[0]


def _modulation(c, mod_w, mod_b):
    L, D, D6 = mod_w.shape
    B = c.shape[0]
    nj = D6 // D
    return pl.pallas_call(
        _mod_kernel,
        grid=(L, nj),
        in_specs=[
            pl.BlockSpec((B, D), lambda l, j: (0, 0)),
            pl.BlockSpec((1, D, D), lambda l, j: (l, 0, j)),
            pl.BlockSpec((1, 1, D), lambda l, j: (l, 0, j)),
        ],
        out_specs=pl.BlockSpec((1, B, D), lambda l, j: (l, 0, j)),
        out_shape=jax.ShapeDtypeStruct((L, B, D6), F32),
        compiler_params=_cparams(("arbitrary", "arbitrary")),
        name="modulation",
    )(c, mod_w, mod_b.reshape(L, 1, D6))


def _fold_kernel(uq_ref, ukt_ref, uv_ref, wo_ref, wq_ref, wvo_ref):
    wq_ref[0] = jnp.dot(uq_ref[0, 0], ukt_ref[0, 0], precision=HIGHEST,
                        preferred_element_type=F32).astype(BF16)
    wvo_ref[0] = jnp.dot(uv_ref[0, 0], wo_ref[0, 0], precision=HIGHEST,
                         preferred_element_type=F32).astype(BF16)


def _fold_weights(w_uq, w_uk, w_uv, w_out):
    L = w_uq.shape[0]
    D = w_out.shape[-1]
    uq = jnp.transpose(w_uq, (0, 2, 1, 3))
    ukt = jnp.transpose(w_uk, (0, 2, 3, 1))
    uv = jnp.transpose(w_uv, (0, 2, 1, 3))
    wo = w_out[:, C_CONV:C_CONV + N_HEADS * HEAD_DIM, :].reshape(L, N_HEADS, HEAD_DIM, D)
    return pl.pallas_call(
        _fold_kernel,
        grid=(L, N_HEADS),
        in_specs=[
            pl.BlockSpec((1, 1, D_Q_LAT, HEAD_DIM), lambda l, h: (l, h, 0, 0)),
            pl.BlockSpec((1, 1, HEAD_DIM, D_KV_LAT), lambda l, h: (l, h, 0, 0)),
            pl.BlockSpec((1, 1, D_KV_LAT, HEAD_DIM), lambda l, h: (l, h, 0, 0)),
            pl.BlockSpec((1, 1, HEAD_DIM, D), lambda l, h: (l, h, 0, 0)),
        ],
        out_specs=[
            pl.BlockSpec((1, D_Q_LAT, D_KV_LAT), lambda l, h: (l, 0, h)),
            pl.BlockSpec((1, D_KV_LAT, D), lambda l, h: (l, h, 0)),
        ],
        out_shape=[
            jax.ShapeDtypeStruct((L, D_Q_LAT, N_HEADS * D_KV_LAT), BF16),
            jax.ShapeDtypeStruct((L, N_HEADS * D_KV_LAT, D), BF16),
        ],
        compiler_params=_cparams(("arbitrary", "arbitrary")),
        name="fold_weights",
    )(uq, ukt, uv, wo)


def _rms(v):
    return v * lax.rsqrt(jnp.mean(v * v, axis=-1, keepdims=True) + EPS)


def _in_kernel(x_ref, sc_ref, sh_ref, g_ref, w_ref, qg_ref, kvg_ref, wq_ref,
               cu_ref, qabs_ref, ckv_ref, qi_ref, kw_ref):
    x = x_ref[0]
    h = _rms(x) * g_ref[...] * (1.0 + sc_ref[0]) + sh_ref[0]
    z = jnp.dot(h.astype(BF16), w_ref[...], preferred_element_type=F32)
    a_val = z[:, 0:C_CONV]
    a_gate = z[:, C_CONV:2 * C_CONV]
    cu_ref[0, :, 0:C_CONV] = a_val * jax.nn.sigmoid(a_gate)
    cu_ref[0, :, C_CONV:C_CONV + C_POOL] = z[:, POOL_OFF:POOL_OFF + C_POOL]
    o = 2 * C_CONV
    cq = _rms(z[:, o:o + D_Q_LAT]) * qg_ref[...]
    qabs = jnp.dot(cq.astype(BF16), wq_ref[...], preferred_element_type=F32)
    qabs_ref[0] = (qabs * (HEAD_DIM ** -0.5 * LOG2E)).astype(BF16)
    o += D_Q_LAT
    ckv_ref[0] = (_rms(z[:, o:o + D_KV_LAT]) * kvg_ref[...]).astype(BF16)
    o += D_KV_LAT
    qi_ref[0] = z[:, o:o + IDX_HEADS * IDX_DIM].astype(BF16)
    kw_ref[0] = z[:, KW_OFF:KW_OFF + LANES]


def _in_proj(x, sc, sh, g, w_in_p, qg, kvg, wq, tm):
    B, S, D = x.shape
    tok = lambda n: pl.BlockSpec((1, tm, n), lambda b, i: (b, i, 0))
    per_b = pl.BlockSpec((1, 1, D), lambda b, i: (b, 0, 0))
    full = lambda a: pl.BlockSpec(a.shape, lambda b, i: (0,) * a.ndim)
    nq = IDX_HEADS * IDX_DIM
    return pl.pallas_call(
        _in_kernel,
        grid=(B, S // tm),
        in_specs=[tok(D), per_b, per_b, full(g), full(w_in_p), full(qg), full(kvg), full(wq)],
        out_specs=[tok(C_CONV + C_POOL), tok(N_HEADS * D_KV_LAT), tok(D_KV_LAT), tok(nq), tok(LANES)],
        out_shape=[
            jax.ShapeDtypeStruct((B, S, C_CONV + C_POOL), F32),
            jax.ShapeDtypeStruct((B, S, N_HEADS * D_KV_LAT), BF16),
            jax.ShapeDtypeStruct((B, S, D_KV_LAT), BF16),
            jax.ShapeDtypeStruct((B, S, nq), BF16),
            jax.ShapeDtypeStruct((B, S, LANES), F32),
        ],
        compiler_params=_cparams(("arbitrary", "arbitrary")),
        name="in_proj",
    )(x, sc, sh, g, w_in_p, qg, kvg, wq)


def _cp_kernel(cu_ref, ck_ref, cb_ref, lg_ref, lb_ref, pw_ref, ps_ref, o_ref, pad_ref):
    S = cu_ref.shape[1]
    TT = CONV_TILE
    pad_ref[0:HALO, :] = jnp.zeros((HALO, C_CONV + C_POOL), F32)
    pad_ref[HALO:HALO + S, :] = cu_ref[0]
    lane = lax.broadcasted_iota(jnp.int32, (TT, C_POOL), 1)
    row = lax.broadcasted_iota(jnp.int32, (TT, C_POOL), 0)
    win_len = jnp.where(lane < POOL_GROUP, 2,
                        jnp.where(lane < 2 * POOL_GROUP, 4, jnp.where(lane < 3 * POOL_GROUP, 8, 16)))

    def tile(i, carry):
        t0 = pl.multiple_of(i * TT, TT)
        win = pad_ref[pl.ds(t0, TT + HALO), :]
        wc = win[:, 0:C_CONV]
        wp = win[:, C_CONV:C_CONV + C_POOL]
        acc = jnp.zeros((TT, C_CONV), F32)
        for s in range(8):
            ws = wc if s == 0 else pltpu.roll(wc, TT + HALO - s, 0)
            for j in range(CONV_WIDTH):
                o = HALO - (CONV_WIDTH - 1) + j
                if o % 8 == s:
                    acc = acc + ck_ref[j:j + 1, :] * ws[o - s:o - s + TT, :]
        v = acc + cb_ref[...]
        mu = jnp.mean(v, axis=-1, keepdims=True)
        vc = v - mu
        var = jnp.mean(vc * vc, axis=-1, keepdims=True)
        yn = vc * lax.rsqrt(var + EPS) * lg_ref[...] + lb_ref[...]
        o_ref[0, pl.ds(t0, TT), 0:C_CONV] = (yn * jax.nn.sigmoid(yn)).astype(BF16)
        lo = HALO - 15
        s2 = wp[lo + 1:, :] + wp[lo:-1, :]
        s4 = s2[2:, :] + s2[:-2, :]
        s8 = s4[4:, :] + s4[:-4, :]
        s16 = s8[8:, :] + s8[:-8, :]
        u = wp[HALO:, :]
        sw = jnp.where(lane < POOL_GROUP, s2[14:, :],
                       jnp.where(lane < 2 * POOL_GROUP, s4[12:, :],
                                 jnp.where(lane < 3 * POOL_GROUP, s8[8:, :], s16)))
        n_pos = jnp.minimum(t0 + row + 1, win_len).astype(F32)
        d = (sw / n_pos - u).astype(BF16)
        yp = jnp.dot(d, pw_ref[...], preferred_element_type=F32) * ps_ref[...]
        o_ref[0, pl.ds(t0, TT), C_CONV:C_CONV + C_POOL] = yp.astype(BF16)
        return carry

    lax.fori_loop(0, S // TT, tile, 0)


def _conv_pool(cu, ck, cb, lg, lb, pw_bd, ps):
    B, S, W = cu.shape
    full = lambda a: pl.BlockSpec(a.shape, lambda b: (0,) * a.ndim)
    return pl.pallas_call(
        _cp_kernel,
        grid=(B,),
        in_specs=[pl.BlockSpec((1, S, W), lambda b: (b, 0, 0)),
                  full(ck), full(cb), full(lg), full(lb), full(pw_bd), full(ps)],
        out_specs=pl.BlockSpec((1, S, W), lambda b: (b, 0, 0)),
        out_shape=jax.ShapeDtypeStruct((B, S, W), BF16),
        scratch_shapes=[pltpu.VMEM((S + HALO, W), F32)],
        compiler_params=_cparams(("arbitrary",)),
        name="conv_pool",
    )(cu, ck, cb, lg, lb, pw_bd, ps)


def _attn_kernel(qi_ref, kwq_ref, kwk_ref, qabs_ref, ckv_ref, o_ref, key_ref, aux_ref, bias_ref, *, topk, q0):
    G, QB = qi_ref.shape[0], qi_ref.shape[1]
    LK = ckv_ref.shape[1]
    R = G * QB
    dn_nt = (((1,), (1,)), ((), ()))

    def visible(rows):
        kidx = lax.broadcasted_iota(jnp.int32, (rows, LK), 1)
        qpos = q0 + lax.broadcasted_iota(jnp.int32, (rows, 1), 0) % QB
        return kidx, kidx < (qpos // CHUNK + 1) * CHUNK

    if LK > topk:
        _, allowed = visible(QB)
        for g in range(G):
            ki = kwk_ref[g, :, 0:IDX_DIM].astype(BF16)
            wi = kwq_ref[g, :, IDX_DIM:IDX_DIM + IDX_HEADS]
            qi = qi_ref[g]
            score = jnp.zeros((QB, LK), F32)
            for h in range(IDX_HEADS):
                rel = lax.dot_general(qi[:, h * IDX_DIM:(h + 1) * IDX_DIM], ki, dn_nt,
                                      preferred_element_type=F32)
                score = score + jnp.maximum(rel, 0.0) * wi[:, h:h + 1]
            score = score * ((IDX_HEADS * IDX_DIM) ** -0.5)
            score = jnp.where(score == 0.0, 0.0, score)
            score = jnp.where(allowed, score, -jnp.inf)
            bits = lax.bitcast_convert_type(score, jnp.int32)
            key_ref[g * QB:(g + 1) * QB, :] = bits ^ ((bits >> 31) & jnp.int32(0x7FFFFFFF))

        kf = jnp.float32(topk)

        H = R // 2

        def partial_counts(r0, cand):
            acc = jnp.zeros((H, LANES), F32)
            for c in range(LK // LANES):
                k = key_ref[r0:r0 + H, c * LANES:(c + 1) * LANES]
                acc = acc + jnp.where(k >= cand, 1.0, 0.0)
            return acc

        def settle(acc, cand, pre):
            return jnp.where(jnp.sum(acc, axis=-1, keepdims=True) >= kf, cand, pre)

        def bit(i):
            return jnp.where(i <= 31, jnp.int32(1) << jnp.maximum(31 - i, 0), 0)

        def bit_step(i, carry):
            pre_a, pre_b, acc_b = carry
            cand_a = pre_a + bit(i)
            acc_a = partial_counts(0, cand_a)
            pre_b = settle(acc_b, pre_b + bit(i), pre_b)
            acc_b = partial_counts(H, pre_b + bit(i + 1))
            pre_a = settle(acc_a, cand_a, pre_a)
            return pre_a, pre_b, acc_b

        lowest = jnp.full((H, 1), jnp.iinfo(jnp.int32).min, jnp.int32)
        pre_a, pre_b, _ = lax.fori_loop(0, 32, bit_step,
                                        (lowest, lowest, partial_counts(H, lowest + bit(0))))
        thr = jnp.concatenate([pre_a, pre_b], axis=0)
        reached = key_ref[...] >= thr
        bias_ref[...] = jnp.where(reached, 0.0, -jnp.inf)
        has_excess = jnp.max(jnp.sum(jnp.where(reached, 1.0, 0.0), axis=-1, keepdims=True)) > kf

        @pl.when(has_excess)
        def _():
            kidx, allowed_r = visible(R)
            key = key_ref[...]
            n_gt = jnp.sum(jnp.where(key > thr, 1.0, 0.0), axis=-1, keepdims=True)
            need = kf - n_gt
            aux_ref[...] = jnp.where(key == thr, kidx, jnp.int32(LK))
            nbits = (LK - 1).bit_length()

            def idx_step(i, p):
                cand = p | (jnp.int32(1) << (nbits - 1 - i))
                cnt = jnp.sum(jnp.where(aux_ref[...] < cand, 1.0, 0.0), axis=-1, keepdims=True)
                return jnp.where(cnt < need, cand, p)

            cut = lax.fori_loop(0, nbits, idx_step, jnp.zeros((R, 1), jnp.int32))
            sel = ((key_ref[...] > thr) | (aux_ref[...] <= cut)) & allowed_r
            bias_ref[...] = jnp.where(sel, 0.0, -jnp.inf)

    for g in range(G):
        if LK > topk:
            bias = bias_ref[g * QB:(g + 1) * QB, :]
        else:
            bias = jnp.where(visible(QB)[1], 0.0, -jnp.inf)
        ckv = ckv_ref[g]
        for h in range(N_HEADS):
            qa = qabs_ref[g, :, h * D_KV_LAT:(h + 1) * D_KV_LAT]
            logit = lax.dot_general(qa, ckv, dn_nt, preferred_element_type=F32) + bias
            m = jnp.max(logit, axis=-1, keepdims=True)
            p = jnp.exp2(logit - m)
            l = jnp.sum(p, axis=-1, keepdims=True)
            ctx = jnp.dot(p.astype(BF16), ckv, preferred_element_type=F32)
            o_ref[g, :, h * D_KV_LAT:(h + 1) * D_KV_LAT] = (ctx / l).astype(BF16)


def _attention(qi, kw, qabs, ckv, qb):
    B, S, _ = ckv.shape
    topk = min(TOPK_MAX, S // 4)
    buf = qabs
    for j in range(S // qb):
        lk = (j + 1) * qb
        G = ATTN_BATCH if (B % ATTN_BATCH == 0 and lk <= ATTN_BATCH_MAX_KEYS) else 1
        blk = lambda n, j=j: pl.BlockSpec((G, qb, n), lambda b: (b, j, 0))
        keys = lambda n, lk=lk: pl.BlockSpec((G, lk, n), lambda b: (b, 0, 0))
        buf = pl.pallas_call(
            functools.partial(_attn_kernel, topk=topk, q0=j * qb),
            grid=(B // G,),
            in_specs=[blk(IDX_HEADS * IDX_DIM), blk(LANES), keys(LANES), blk(N_HEADS * D_KV_LAT),
                      keys(D_KV_LAT)],
            out_specs=blk(N_HEADS * D_KV_LAT),
            out_shape=jax.ShapeDtypeStruct(buf.shape, BF16),
            scratch_shapes=[pltpu.VMEM((G * qb, lk), jnp.int32), pltpu.VMEM((G * qb, lk), jnp.int32),
                            pltpu.VMEM((G * qb, lk), F32)],
            input_output_aliases={3: 0},
            compiler_params=_cparams(("arbitrary",)),
            name=f"dsa_attention_q{j}",
        )(qi, kw, kw, buf, ckv)
    return buf


def _out_kernel(ycp_ref, ctx_ref, x_ref, g1_ref, sc_ref, sh_ref, ng_ref, wcp_ref, wvo_ref,
                rw_ref, rb_ref, tri_ref, x1_ref, h2_ref, route_ref, cnt_ref, carry_ref):
    first = (pl.program_id(0) == 0) & (pl.program_id(1) == 0)

    @pl.when(first)
    def _():
        carry_ref[...] = jnp.zeros_like(carry_ref)

    mix = jnp.dot(ycp_ref[0], wcp_ref[...], preferred_element_type=F32)
    mix = mix + jnp.dot(ctx_ref[0], wvo_ref[...], preferred_element_type=F32)
    x1 = x_ref[0] + g1_ref[0] * mix
    x1_ref[0] = x1
    h2 = _rms(x1) * ng_ref[...] * (1.0 + sc_ref[0]) + sh_ref[0]
    _rows_to_tiles(h2_ref.at[0], h2)
    rw = rw_ref[...]
    rw_hi = rw.astype(BF16)
    rw_lo = (rw - rw_hi.astype(F32)).astype(BF16)
    h_hi = h2.astype(BF16)
    h_lo = (h2 - h_hi.astype(F32)).astype(BF16)
    logits = (jnp.dot(h_hi, rw_hi, preferred_element_type=F32)
              + (jnp.dot(h_lo, rw_hi, preferred_element_type=F32)
                 + jnp.dot(h_hi, rw_lo, preferred_element_type=F32))) + rb_ref[...]

    lt = logits.T
    tm = lt.shape[1]
    rowi = lax.broadcasted_iota(jnp.int32, (LANES, tm), 0)
    rowf = rowi.astype(F32)
    ninf = -jnp.inf
    big = jnp.float32(LANES)
    glog = jnp.where(rowi < N_GROUPS, lt, ninf)
    gmax = jnp.max(glog, axis=0, keepdims=True)
    gidx = jnp.min(jnp.where(glog == gmax, rowf, big), axis=0, keepdims=True)
    g_p = 1.0 / jnp.sum(jnp.exp(glog - gmax), axis=0, keepdims=True)
    lo = N_GROUPS + EXPERTS_PER_GROUP * gidx
    elog = jnp.where((rowf >= lo) & (rowf < lo + EXPERTS_PER_GROUP), lt, ninf)
    m1 = jnp.max(elog, axis=0, keepdims=True)
    i1 = jnp.min(jnp.where(elog == m1, rowf, big), axis=0, keepdims=True)
    elog2 = jnp.where(rowf == i1, ninf, elog)
    m2 = jnp.max(elog2, axis=0, keepdims=True)
    i2 = jnp.min(jnp.where(elog2 == m2, rowf, big), axis=0, keepdims=True)
    r = jnp.exp(m2 - m1)
    gate1 = g_p / (1.0 + r)
    gate2 = g_p * r / (1.0 + r)
    e1 = i1 - N_GROUPS
    e2 = i2 - N_GROUPS
    oh1 = rowf == e1
    oh2 = rowf == e2
    oh1f = jnp.where(oh1, 1.0, 0.0)
    oh2f = jnp.where(oh2, 1.0, 0.0)
    pre1 = jnp.dot(oh1f.astype(BF16), tri_ref[...], preferred_element_type=F32)
    pre2 = jnp.dot(oh2f.astype(BF16), tri_ref[...], preferred_element_type=F32)
    carry = carry_ref[:, 0:1]
    cnt1 = jnp.sum(oh1f, axis=1, keepdims=True)
    cnt2 = jnp.sum(oh2f, axis=1, keepdims=True)
    rank1 = jnp.sum(jnp.where(oh1, carry + pre1, 0.0), axis=0, keepdims=True)
    rank2 = jnp.sum(jnp.where(oh2, carry + cnt1 + pre2, 0.0), axis=0, keepdims=True)
    total = carry + cnt1 + cnt2
    carry_ref[...] = jnp.broadcast_to(total, carry_ref.shape)
    cnt_ref[...] = jnp.broadcast_to(total, cnt_ref.shape)
    route_t = jnp.where(rowi == 0, e1, jnp.where(rowi == 1, e2, jnp.where(rowi == 2, gate1, jnp.where(
        rowi == 3, gate2, jnp.where(rowi == 4, rank1, jnp.where(rowi == 5, rank2, 0.0))))))
    route_ref[0] = route_t.T


def _out_proj(ycp, ctx, x, g1, sc2, sh2, ng, wcp, wvo, rw, rb, tm):
    B, S, D = x.shape
    tok = lambda n: pl.BlockSpec((1, tm, n), lambda b, i: (b, i, 0))
    per_b = pl.BlockSpec((1, 1, D), lambda b, i: (b, 0, 0))
    full = lambda a: pl.BlockSpec(a.shape, lambda b, i: (0,) * a.ndim)
    tri = jnp.triu(jnp.ones((tm, tm), BF16), 1)
    return pl.pallas_call(
        _out_kernel,
        grid=(B, S // tm),
        in_specs=[tok(C_CONV + C_POOL), tok(N_HEADS * D_KV_LAT), tok(D), per_b, per_b, per_b,
                  full(ng), full(wcp), full(wvo), full(rw), full(rb), full(tri)],
        out_specs=[tok(D), pl.BlockSpec((1, tm * ROW_TILE, LANES), lambda b, i: (b, i, 0)), tok(LANES),
                   pl.BlockSpec((LANES, LANES), lambda b, i: (0, 0))],
        out_shape=[
            jax.ShapeDtypeStruct((B, S, D), F32),
            jax.ShapeDtypeStruct((B, S * ROW_TILE, LANES), F32),
            jax.ShapeDtypeStruct((B, S, LANES), F32),
            jax.ShapeDtypeStruct((LANES, LANES), F32),
        ],
        scratch_shapes=[pltpu.VMEM((LANES, LANES), F32)],
        compiler_params=_cparams(("arbitrary", "arbitrary")),
        name="out_proj_router",
    )(ycp, ctx, x, g1, sc2, sh2, ng, wcp, wvo, rw, rb, tri)


def _row_copy(src, i, dst, k, sem):
    return pltpu.make_async_copy(src.at[pl.ds(pl.multiple_of(i * ROW_TILE, ROW_TILE), ROW_TILE), :],
                                 dst.at[pl.ds(pl.multiple_of(k * ROW_TILE, ROW_TILE), ROW_TILE), :], sem)


def _rows_to_tiles(ref, val):
    n = val.shape[0]
    for s in range(ROW_TILE):
        ref[pl.ds(s, n, stride=ROW_TILE), :] = val[:, s * LANES:(s + 1) * LANES]


def _tiles_to_rows(ref, n):
    return jnp.concatenate([ref[pl.ds(s, n, stride=ROW_TILE), :] for s in range(ROW_TILE)], axis=-1)


DISPATCH_SLOTS = 3


def _dispatch_kernel(dst_ref, dst_prev_ref, h_ref, xs_in_ref, xs_ref, hbuf, load_sem, sem):
    del xs_in_ref
    rows = hbuf.shape[1]
    tm = rows // ROW_TILE
    i = pl.program_id(0)
    n = pl.num_programs(0)
    slot = i % DISPATCH_SLOTS

    def load(t, s):
        return pltpu.make_async_copy(h_ref.at[pl.ds(pl.multiple_of(t * rows, rows), rows), :], hbuf.at[s],
                                     load_sem.at[s])

    @pl.when(i == 0)
    def _():
        load(0, 0).start()

    @pl.when(i + 1 < n)
    def _():
        load(i + 1, (i + 1) % DISPATCH_SLOTS).start()

    load(i, slot).wait()

    def copies(idx_ref, s, r):
        return (_row_copy(hbuf.at[s], r, xs_ref, idx_ref[2 * r], sem.at[s]),
                _row_copy(hbuf.at[s], r, xs_ref, idx_ref[2 * r + 1], sem.at[s]))

    def issue(r, c):
        for prio, cp in enumerate(copies(dst_ref, slot, r)):
            cp.start(priority=prio)
        return c

    lax.fori_loop(0, tm, issue, 0, unroll=ROW_DMA_UNROLL)

    def drain(idx_ref, s):
        def body(r, c):
            for cp in copies(idx_ref, s, r):
                cp.wait()
            return c
        lax.fori_loop(0, tm, body, 0, unroll=ROW_DMA_UNROLL)

    @pl.when(i >= 1)
    def _():
        drain(dst_prev_ref, (i + DISPATCH_SLOTS - 1) % DISPATCH_SLOTS)

    @pl.when(i == n - 1)
    def _():
        drain(dst_ref, slot)


def _dispatch(dst_flat, h2, xs0, tm):
    T = h2.shape[0] // ROW_TILE
    return pl.pallas_call(
        _dispatch_kernel,
        grid=(T // tm,),
        in_specs=[pl.BlockSpec((2 * tm,), lambda i: (i,), memory_space=pltpu.SMEM),
                  pl.BlockSpec((2 * tm,), lambda i: (jnp.maximum(i - 1, 0),), memory_space=pltpu.SMEM),
                  pl.BlockSpec(memory_space=pl.ANY),
                  pl.BlockSpec(memory_space=pl.ANY)],
        out_specs=pl.BlockSpec(memory_space=pl.ANY),
        out_shape=jax.ShapeDtypeStruct(xs0.shape, F32),
        scratch_shapes=[pltpu.VMEM((DISPATCH_SLOTS, tm * ROW_TILE, LANES), F32),
                        pltpu.SemaphoreType.DMA((DISPATCH_SLOTS,)),
                        pltpu.SemaphoreType.DMA((DISPATCH_SLOTS,))],
        input_output_aliases={3: 0},
        compiler_params=_cparams(("arbitrary",), row_dma=True),
        name="moe_dispatch",
    )(dst_flat, dst_flat, h2, xs0)


def _ffn_kernel(be_ref, nu_ref, xs_ref, w1_ref, w3_ref, w2_ref, y_ref, w13_scr, w2_scr):
    i = pl.program_id(0)
    new_expert = (i == 0) | (be_ref[i] != be_ref[jnp.maximum(i - 1, 0)])

    @pl.when(new_expert)
    def _():
        w13_scr[:, 0:D_EXPERT] = w1_ref[0, 0].astype(BF16)
        w13_scr[:, D_EXPERT:2 * D_EXPERT] = w3_ref[0, 0].astype(BF16)
        w2_scr[...] = w2_ref[0, 0].astype(BF16)

    @pl.when(i < nu_ref[0])
    def _():
        xb = _tiles_to_rows(xs_ref, FFN_BLOCK).astype(BF16)
        h13 = jnp.dot(xb, w13_scr[...], preferred_element_type=F32)
        h1 = h13[:, 0:D_EXPERT]
        h3 = h13[:, D_EXPERT:2 * D_EXPERT]
        act = (h1 * jax.nn.sigmoid(h1)) * h3
        _rows_to_tiles(y_ref, jnp.dot(act.astype(BF16), w2_scr[...], preferred_element_type=F32))

    @pl.when(i >= nu_ref[0])
    def _():
        y_ref[...] = jnp.zeros_like(y_ref)


def _expert_ffn(blk_exp, n_used, xs, w1, w3, w2, layer):
    D = w1.shape[2]
    rows_blk = FFN_BLOCK * ROW_TILE
    nblk = xs.shape[0] // rows_blk
    grid_spec = pltpu.PrefetchScalarGridSpec(
        num_scalar_prefetch=2,
        grid=(nblk,),
        in_specs=[
            pl.BlockSpec((rows_blk, LANES), lambda i, be, nu: (jnp.maximum(jnp.minimum(i, nu[0] - 1), 0), 0)),
            pl.BlockSpec((1, 1, D, D_EXPERT), lambda i, be, nu: (layer, be[i], 0, 0)),
            pl.BlockSpec((1, 1, D, D_EXPERT), lambda i, be, nu: (layer, be[i], 0, 0)),
            pl.BlockSpec((1, 1, D_EXPERT, D), lambda i, be, nu: (layer, be[i], 0, 0)),
        ],
        out_specs=pl.BlockSpec((rows_blk, LANES), lambda i, be, nu: (i, 0)),
        scratch_shapes=[pltpu.VMEM((D, 2 * D_EXPERT), BF16), pltpu.VMEM((D_EXPERT, D), BF16)],
    )
    return pl.pallas_call(
        _ffn_kernel,
        grid_spec=grid_spec,
        out_shape=jax.ShapeDtypeStruct(xs.shape, F32),
        compiler_params=_cparams(("arbitrary",)),
        name="expert_ffn",
    )(blk_exp, n_used, xs, w1, w3, w2)


def _combine_kernel(dst_ref, dst_next_ref, x1_ref, route_ref, g2_ref, fg_ref, y_ref, o_ref, ya_ref, yb_ref, sem,
                    *, final_norm):
    tm = x1_ref.shape[1]
    step = pl.program_id(0) * pl.num_programs(1) + pl.program_id(1)
    n_steps = pl.num_programs(0) * pl.num_programs(1)
    slot = step % 2

    def copies(idx_ref, s, r):
        return (_row_copy(y_ref, idx_ref[2 * r], ya_ref.at[s], r, sem.at[0, s]),
                _row_copy(y_ref, idx_ref[2 * r + 1], yb_ref.at[s], r, sem.at[1, s]))

    def gather(idx_ref, s):
        def issue(r, c):
            for prio, cp in enumerate(copies(idx_ref, s, r)):
                cp.start(priority=prio)
            return c
        lax.fori_loop(0, tm, issue, 0, unroll=ROW_DMA_UNROLL)

    @pl.when(step == 0)
    def _():
        gather(dst_ref, 0)

    @pl.when(step + 1 < n_steps)
    def _():
        gather(dst_next_ref, 1 - slot)

    def drain(r, c):
        for cp in copies(dst_ref, slot, r):
            cp.wait()
        return c

    lax.fori_loop(0, tm, drain, 0, unroll=ROW_DMA_UNROLL)

    route = route_ref[0]
    moe = (_tiles_to_rows(ya_ref.at[slot], tm) * route[:, 2:3]
           + _tiles_to_rows(yb_ref.at[slot], tm) * route[:, 3:4])
    out = x1_ref[0] + g2_ref[0] * moe
    if final_norm:
        out = _rms(out) * fg_ref[...]
    o_ref[0] = out


def _combine(dst_flat, x1, route, g2, fg, y, tm, final_norm):
    B, S, D = x1.shape
    nt = S // tm
    tok = lambda n: pl.BlockSpec((1, tm, n), lambda b, i: (b, i, 0))
    return pl.pallas_call(
        functools.partial(_combine_kernel, final_norm=final_norm),
        grid=(B, nt),
        in_specs=[pl.BlockSpec((2 * tm,), lambda b, i: (b * nt + i,), memory_space=pltpu.SMEM),
                  pl.BlockSpec((2 * tm,), lambda b, i: (jnp.minimum(b * nt + i + 1, B * nt - 1),),
                               memory_space=pltpu.SMEM),
                  tok(D), tok(LANES),
                  pl.BlockSpec((1, 1, D), lambda b, i: (b, 0, 0)),
                  pl.BlockSpec((1, D), lambda b, i: (0, 0)),
                  pl.BlockSpec(memory_space=pl.ANY)],
        out_specs=tok(D),
        out_shape=jax.ShapeDtypeStruct((B, S, D), F32),
        scratch_shapes=[pltpu.VMEM((2, tm * ROW_TILE, LANES), F32), pltpu.VMEM((2, tm * ROW_TILE, LANES), F32),
                        pltpu.SemaphoreType.DMA((2, 2))],
        compiler_params=_cparams(("arbitrary", "arbitrary"), row_dma=True),
        name="moe_combine",
    )(dst_flat, dst_flat, x1, route, g2, fg, y)


def _pick_tile(n, pref):
    t = min(n, pref)
    while n % t:
        t //= 2
    return t


def kernel(x, c, mod_w, mod_b, norm1_g, w_in, conv_k, conv_b, conv_ln_g, conv_ln_b, q_norm_g, kv_norm_g, w_uq, w_uk, w_uv, pool_w, pool_scale, w_out, norm2_g, router_g_w, router_g_b, router_e_w, router_e_b, exp_w1, exp_w3, exp_w2, final_g):
    B, S, D = x.shape
    L = mod_w.shape[0]
    T = B * S
    tm = _pick_tile(S, 512)
    qb = _pick_tile(S, 256)
    tmc = _pick_tile(S, 512)
    n_rows = 2 * T + N_EXPERTS * FFN_BLOCK
    nblk = n_rows // FFN_BLOCK

    mod = _modulation(c, mod_w, mod_b)
    wq_all, wvo_all = _fold_weights(w_uq, w_uk, w_uv, w_out)
    row = lambda a: a.reshape(1, -1)

    assert D == ROW_TILE * LANES
    slot_buf = jnp.zeros((n_rows * ROW_TILE, LANES), F32)
    for l in range(L):
        sh1, sc1, g1, sh2, sc2, g2 = [mod[l, :, k * D:(k + 1) * D].reshape(B, 1, D) for k in range(6)]
        w_in_p = jnp.concatenate(
            [w_in[l, :, :N_IN - C_POOL], jnp.zeros((D, N_IN_PAD - N_IN), F32), w_in[l, :, N_IN - C_POOL:]],
            axis=1).astype(BF16)
        cu, qabs, ckv, qi, kw = _in_proj(x, sc1, sh1, row(norm1_g[l]), w_in_p, row(q_norm_g[l]),
                                         row(kv_norm_g[l]), wq_all[l], tm)
        pw_bd = jax.scipy.linalg.block_diag(*[pool_w[l, g] for g in range(pool_w.shape[1])]).astype(BF16)
        ycp = _conv_pool(cu, conv_k[l], row(conv_b[l]), row(conv_ln_g[l]), row(conv_ln_b[l]),
                         pw_bd, row(pool_scale[l]))
        ctx = _attention(qi, kw, qabs, ckv, qb)
        wcp = jnp.concatenate([w_out[l, :C_CONV], w_out[l, D - C_POOL:]], axis=0).astype(BF16)
        n_r = N_GROUPS + N_EXPERTS
        rw = jnp.concatenate([router_g_w[l], router_e_w[l].reshape(D, N_EXPERTS),
                              jnp.zeros((D, LANES - n_r), F32)], axis=1)
        rb = jnp.concatenate([router_g_b[l], router_e_b[l].reshape(N_EXPERTS),
                              jnp.zeros((LANES - n_r,), F32)]).reshape(1, LANES)
        x1, h2, route, cnt = _out_proj(ycp, ctx, x, g1, sc2, sh2, row(norm2_g[l]), wcp, wvo_all[l],
                                       rw, rb, tm)
        counts = cnt[:N_EXPERTS, 0].astype(jnp.int32)
        pcounts = (counts + FFN_BLOCK - 1) // FFN_BLOCK * FFN_BLOCK
        pends = jnp.cumsum(pcounts)
        pstarts = pends - pcounts
        route2 = route.reshape(T, LANES)
        e_idx = route2[:, 0:2].astype(jnp.int32)
        e_start = jnp.sum(jnp.where(e_idx[..., None] == jnp.arange(N_EXPERTS, dtype=jnp.int32), pstarts, 0),
                          axis=-1)
        dst = (e_start + route2[:, 4:6].astype(jnp.int32)).reshape(2 * T)
        n_used = (pends[-1] // FFN_BLOCK).astype(jnp.int32).reshape(1)
        blk_start = jnp.arange(nblk, dtype=jnp.int32) * FFN_BLOCK
        blk_exp = jnp.minimum(jnp.sum((pends[None, :] <= blk_start[:, None]).astype(jnp.int32), axis=1),
                              N_EXPERTS - 1)
        xs = _dispatch(dst, h2.reshape(T * ROW_TILE, LANES), slot_buf, tmc)
        y = _expert_ffn(blk_exp, n_used, xs, exp_w1, exp_w3, exp_w2, l)
        slot_buf = y
        x = _combine(dst, x1, route, g2, row(final_g), y, tmc, final_norm=(l == L - 1))
    return x
```

```python
import functools

import jax
import jax.numpy as jnp
from jax import lax
from jax.experimental import pallas as pl
from jax.experimental.pallas import tpu as pltpu

F32 = jnp.float32
BF16 = jnp.bfloat16
HIGHEST = lax.Precision.HIGHEST

EPS = 1e-6
CHUNK = 64
CONV_WIDTH = 31
C_CONV = 256
C_POOL = 256
POOL_GROUP = 64
N_HEADS = 8
HEAD_DIM = 64
D_Q_LAT = 256
D_KV_LAT = 128
IDX_HEADS = 8
IDX_DIM = 64
TOPK_MAX = 256
N_GROUPS = 4
EXPERTS_PER_GROUP = 8
N_EXPERTS = 32
D_EXPERT = 512
N_IN = 1736
N_IN_PAD = 1792
KW_OFF = 1408
POOL_OFF = 1536
LANES = 128
HALO = 32
CONV_TILE = 64
FFN_BLOCK = 256
ROW_DMA_UNROLL = 8
ROW_TILE = 8
ATTN_BATCH = 2
ATTN_BATCH_MAX_KEYS = 1024
LOG2E = 1.4426950408889634
VMEM_LIMIT = 56 * 1024 * 1024


def _cparams(sem, row_dma=False):
    return pltpu.CompilerParams(dimension_semantics=sem, vmem_limit_bytes=VMEM_LIMIT,
                                disable_bounds_checks=row_dma)


def _mod_kernel(c_ref, w_ref, b_ref, o_ref):
    c = c_ref[...]
    cond = c * jax.nn.sigmoid(c)
    o_ref[0] = jnp.dot(cond, w_ref[0], precision=HIGHEST, preferred_element_type=F32) + b_ref[0]


def _modulation(c, mod_w, mod_b):
    L, D, D6 = mod_w.shape
    B = c.shape[0]
    nj = D6 // D
    return pl.pallas_call(
        _mod_kernel,
        grid=(L, nj),
        in_specs=[
            pl.BlockSpec((B, D), lambda l, j: (0, 0)),
            pl.BlockSpec((1, D, D), lambda l, j: (l, 0, j)),
            pl.BlockSpec((1, 1, D), lambda l, j: (l, 0, j)),
        ],
        out_specs=pl.BlockSpec((1, B, D), lambda l, j: (l, 0, j)),
        out_shape=jax.ShapeDtypeStruct((L, B, D6), F32),
        compiler_params=_cparams(("arbitrary", "arbitrary")),
        name="modulation",
    )(c, mod_w, mod_b.reshape(L, 1, D6))


def _fold_kernel(uq_ref, ukt_ref, uv_ref, wo_ref, wq_ref, wvo_ref):
    wq_ref[0] = jnp.dot(uq_ref[0, 0], ukt_ref[0, 0], precision=HIGHEST,
                        preferred_element_type=F32).astype(BF16)
    wvo_ref[0] = jnp.dot(uv_ref[0, 0], wo_ref[0, 0], precision=HIGHEST,
                         preferred_element_type=F32).astype(BF16)


def _fold_weights(w_uq, w_uk, w_uv, w_out):
    L = w_uq.shape[0]
    D = w_out.shape[-1]
    uq = jnp.transpose(w_uq, (0, 2, 1, 3))
    ukt = jnp.transpose(w_uk, (0, 2, 3, 1))
    uv = jnp.transpose(w_uv, (0, 2, 1, 3))
    wo = w_out[:, C_CONV:C_CONV + N_HEADS * HEAD_DIM, :].reshape(L, N_HEADS, HEAD_DIM, D)
    return pl.pallas_call(
        _fold_kernel,
        grid=(L, N_HEADS),
        in_specs=[
            pl.BlockSpec((1, 1, D_Q_LAT, HEAD_DIM), lambda l, h: (l, h, 0, 0)),
            pl.BlockSpec((1, 1, HEAD_DIM, D_KV_LAT), lambda l, h: (l, h, 0, 0)),
            pl.BlockSpec((1, 1, D_KV_LAT, HEAD_DIM), lambda l, h: (l, h, 0, 0)),
            pl.BlockSpec((1, 1, HEAD_DIM, D), lambda l, h: (l, h, 0, 0)),
        ],
        out_specs=[
            pl.BlockSpec((1, D_Q_LAT, D_KV_LAT), lambda l, h: (l, 0, h)),
            pl.BlockSpec((1, D_KV_LAT, D), lambda l, h: (l, h, 0)),
        ],
        out_shape=[
            jax.ShapeDtypeStruct((L, D_Q_LAT, N_HEADS * D_KV_LAT), BF16),
            jax.ShapeDtypeStruct((L, N_HEADS * D_KV_LAT, D), BF16),
        ],
        compiler_params=_cparams(("arbitrary", "arbitrary")),
        name="fold_weights",
    )(uq, ukt, uv, wo)


def _rms(v):
    return v * lax.rsqrt(jnp.mean(v * v, axis=-1, keepdims=True) + EPS)


def _in_kernel(x_ref, sc_ref, sh_ref, g_ref, w_ref, qg_ref, kvg_ref, wq_ref,
               cu_ref, qabs_ref, ckv_ref, qi_ref, kw_ref):
    x = x_ref[0]
    h = _rms(x) * g_ref[...] * (1.0 + sc_ref[0]) + sh_ref[0]
    z = jnp.dot(h.astype(BF16), w_ref[...], preferred_element_type=F32)
    a_val = z[:, 0:C_CONV]
    a_gate = z[:, C_CONV:2 * C_CONV]
    cu_ref[0, :, 0:C_CONV] = a_val * jax.nn.sigmoid(a_gate)
    cu_ref[0, :, C_CONV:C_CONV + C_POOL] = z[:, POOL_OFF:POOL_OFF + C_POOL]
    o = 2 * C_CONV
    cq = _rms(z[:, o:o + D_Q_LAT]) * qg_ref[...]
    qabs = jnp.dot(cq.astype(BF16), wq_ref[...], preferred_element_type=F32)
    qabs_ref[0] = (qabs * (HEAD_DIM ** -0.5 * LOG2E)).astype(BF16)
    o += D_Q_LAT
    ckv_ref[0] = (_rms(z[:, o:o + D_KV_LAT]) * kvg_ref[...]).astype(BF16)
    o += D_KV_LAT
    qi_ref[0] = z[:, o:o + IDX_HEADS * IDX_DIM].astype(BF16)
    kw_ref[0] = z[:, KW_OFF:KW_OFF + LANES]


def _in_proj(x, sc, sh, g, w_in_p, qg, kvg, wq, tm):
    B, S, D = x.shape
    tok = lambda n: pl.BlockSpec((1, tm, n), lambda b, i: (b, i, 0))
    per_b = pl.BlockSpec((1, 1, D), lambda b, i: (b, 0, 0))
    full = lambda a: pl.BlockSpec(a.shape, lambda b, i: (0,) * a.ndim)
    nq = IDX_HEADS * IDX_DIM
    return pl.pallas_call(
        _in_kernel,
        grid=(B, S // tm),
        in_specs=[tok(D), per_b, per_b, full(g), full(w_in_p), full(qg), full(kvg), full(wq)],
        out_specs=[tok(C_CONV + C_POOL), tok(N_HEADS * D_KV_LAT), tok(D_KV_LAT), tok(nq), tok(LANES)],
        out_shape=[
            jax.ShapeDtypeStruct((B, S, C_CONV + C_POOL), F32),
            jax.ShapeDtypeStruct((B, S, N_HEADS * D_KV_LAT), BF16),
            jax.ShapeDtypeStruct((B, S, D_KV_LAT), BF16),
            jax.ShapeDtypeStruct((B, S, nq), BF16),
            jax.ShapeDtypeStruct((B, S, LANES), F32),
        ],
        compiler_params=_cparams(("arbitrary", "arbitrary")),
        name="in_proj",
    )(x, sc, sh, g, w_in_p, qg, kvg, wq)


def _cp_kernel(cu_ref, ck_ref, cb_ref, lg_ref, lb_ref, pw_ref, ps_ref, o_ref, pad_ref):
    S = cu_ref.shape[1]
    TT = CONV_TILE
    pad_ref[0:HALO, :] = jnp.zeros((HALO, C_CONV + C_POOL), F32)
    pad_ref[HALO:HALO + S, :] = cu_ref[0]
    lane = lax.broadcasted_iota(jnp.int32, (TT, C_POOL), 1)
    row = lax.broadcasted_iota(jnp.int32, (TT, C_POOL), 0)
    win_len = jnp.where(lane < POOL_GROUP, 2,
                        jnp.where(lane < 2 * POOL_GROUP, 4, jnp.where(lane < 3 * POOL_GROUP, 8, 16)))

    def tile(i, carry):
        t0 = pl.multiple_of(i * TT, TT)
        win = pad_ref[pl.ds(t0, TT + HALO), :]
        wc = win[:, 0:C_CONV]
        wp = win[:, C_CONV:C_CONV + C_POOL]
        acc = jnp.zeros((TT, C_CONV), F32)
        for s in range(8):
            ws = wc if s == 0 else pltpu.roll(wc, TT + HALO - s, 0)
            for j in range(CONV_WIDTH):
                o = HALO - (CONV_WIDTH - 1) + j
                if o % 8 == s:
                    acc = acc + ck_ref[j:j + 1, :] * ws[o - s:o - s + TT, :]
        v = acc + cb_ref[...]
        mu = jnp.mean(v, axis=-1, keepdims=True)
        vc = v - mu
        var = jnp.mean(vc * vc, axis=-1, keepdims=True)
        yn = vc * lax.rsqrt(var + EPS) * lg_ref[...] + lb_ref[...]
        o_ref[0, pl.ds(t0, TT), 0:C_CONV] = (yn * jax.nn.sigmoid(yn)).astype(BF16)
        lo = HALO - 15
        s2 = wp[lo + 1:, :] + wp[lo:-1, :]
        s4 = s2[2:, :] + s2[:-2, :]
        s8 = s4[4:, :] + s4[:-4, :]
        s16 = s8[8:, :] + s8[:-8, :]
        u = wp[HALO:, :]
        sw = jnp.where(lane < POOL_GROUP, s2[14:, :],
                       jnp.where(lane < 2 * POOL_GROUP, s4[12:, :],
                                 jnp.where(lane < 3 * POOL_GROUP, s8[8:, :], s16)))
        n_pos = jnp.minimum(t0 + row + 1, win_len).astype(F32)
        d = (sw / n_pos - u).astype(BF16)
        yp = jnp.dot(d, pw_ref[...], preferred_element_type=F32) * ps_ref[...]
        o_ref[0, pl.ds(t0, TT), C_CONV:C_CONV + C_POOL] = yp.astype(BF16)
        return carry

    lax.fori_loop(0, S // TT, tile, 0)


def _conv_pool(cu, ck, cb, lg, lb, pw_bd, ps):
    B, S, W = cu.shape
    full = lambda a: pl.BlockSpec(a.shape, lambda b: (0,) * a.ndim)
    return pl.pallas_call(
        _cp_kernel,
        grid=(B,),
        in_specs=[pl.BlockSpec((1, S, W), lambda b: (b, 0, 0)),
                  full(ck), full(cb), full(lg), full(lb), full(pw_bd), full(ps)],
        out_specs=pl.BlockSpec((1, S, W), lambda b: (b, 0, 0)),
        out_shape=jax.ShapeDtypeStruct((B, S, W), BF16),
        scratch_shapes=[pltpu.VMEM((S + HALO, W), F32)],
        compiler_params=_cparams(("arbitrary",)),
        name="conv_pool",
    )(cu, ck, cb, lg, lb, pw_bd, ps)


def _attn_kernel(qi_ref, kwq_ref, kwk_ref, qabs_ref, ckv_ref, o_ref, key_ref, aux_ref, bias_ref, *, topk, q0):
    G, QB = qi_ref.shape[0], qi_ref.shape[1]
    LK = ckv_ref.shape[1]
    R = G * QB
    dn_nt = (((1,), (1,)), ((), ()))

    def visible(rows):
        kidx = lax.broadcasted_iota(jnp.int32, (rows, LK), 1)
        qpos = q0 + lax.broadcasted_iota(jnp.int32, (rows, 1), 0) % QB
        return kidx, kidx < (qpos // CHUNK + 1) * CHUNK

    if LK > topk:
        _, allowed = visible(QB)
        for g in range(G):
            ki = kwk_ref[g, :, 0:IDX_DIM].astype(BF16)
            wi = kwq_ref[g, :, IDX_DIM:IDX_DIM + IDX_HEADS]
            qi = qi_ref[g]
            score = jnp.zeros((QB, LK), F32)
            for h in range(IDX_HEADS):
                rel = lax.dot_general(qi[:, h * IDX_DIM:(h + 1) * IDX_DIM], ki, dn_nt,
                                      preferred_element_type=F32)
                score = score + jnp.maximum(rel, 0.0) * wi[:, h:h + 1]
            score = score * ((IDX_HEADS * IDX_DIM) ** -0.5)
            score = jnp.where(score == 0.0, 0.0, score)
            score = jnp.where(allowed, score, -jnp.inf)
            bits = lax.bitcast_convert_type(score, jnp.int32)
            key_ref[g * QB:(g + 1) * QB, :] = bits ^ ((bits >> 31) & jnp.int32(0x7FFFFFFF))

        kf = jnp.float32(topk)

        H = R // 2

        def partial_counts(r0, cand):
            acc = jnp.zeros((H, LANES), F32)
            for c in range(LK // LANES):
                k = key_ref[r0:r0 + H, c * LANES:(c + 1) * LANES]
                acc = acc + jnp.where(k >= cand, 1.0, 0.0)
            return acc

        def settle(acc, cand, pre):
            return jnp.where(jnp.sum(acc, axis=-1, keepdims=True) >= kf, cand, pre)

        def bit(i):
            return jnp.where(i <= 31, jnp.int32(1) << jnp.maximum(31 - i, 0), 0)

        def bit_step(i, carry):
            pre_a, pre_b, acc_b = carry
            cand_a = pre_a + bit(i)
            acc_a = partial_counts(0, cand_a)
            pre_b = settle(acc_b, pre_b + bit(i), pre_b)
            acc_b = partial_counts(H, pre_b + bit(i + 1))
            pre_a = settle(acc_a, cand_a, pre_a)
            return pre_a, pre_b, acc_b

        lowest = jnp.full((H, 1), jnp.iinfo(jnp.int32).min, jnp.int32)
        pre_a, pre_b, _ = lax.fori_loop(0, 32, bit_step,
                                        (lowest, lowest, partial_counts(H, lowest + bit(0))))
        thr = jnp.concatenate([pre_a, pre_b], axis=0)
        reached = key_ref[...] >= thr
        bias_ref[...] = jnp.where(reached, 0.0, -jnp.inf)
        has_excess = jnp.max(jnp.sum(jnp.where(reached, 1.0, 0.0), axis=-1, keepdims=True)) > kf

        @pl.when(has_excess)
        def _():
            kidx, allowed_r = visible(R)
            key = key_ref[...]
            n_gt = jnp.sum(jnp.where(key > thr, 1.0, 0.0), axis=-1, keepdims=True)
            need = kf - n_gt
            aux_ref[...] = jnp.where(key == thr, kidx, jnp.int32(LK))
            nbits = (LK - 1).bit_length()

            def idx_step(i, p):
                cand = p | (jnp.int32(1) << (nbits - 1 - i))
                cnt = jnp.sum(jnp.where(aux_ref[...] < cand, 1.0, 0.0), axis=-1, keepdims=True)
                return jnp.where(cnt < need, cand, p)

            cut = lax.fori_loop(0, nbits, idx_step, jnp.zeros((R, 1), jnp.int32))
            sel = ((key_ref[...] > thr) | (aux_ref[...] <= cut)) & allowed_r
            bias_ref[...] = jnp.where(sel, 0.0, -jnp.inf)

    for g in range(G):
        if LK > topk:
            bias = bias_ref[g * QB:(g + 1) * QB, :]
        else:
            bias = jnp.where(visible(QB)[1], 0.0, -jnp.inf)
        ckv = ckv_ref[g]
        for h in range(N_HEADS):
            qa = qabs_ref[g, :, h * D_KV_LAT:(h + 1) * D_KV_LAT]
            logit = lax.dot_general(qa, ckv, dn_nt, preferred_element_type=F32) + bias
            m = jnp.max(logit, axis=-1, keepdims=True)
            p = jnp.exp2(logit - m)
            l = jnp.sum(p, axis=-1, keepdims=True)
            ctx = jnp.dot(p.astype(BF16), ckv, preferred_element_type=F32)
            o_ref[g, :, h * D_KV_LAT:(h + 1) * D_KV_LAT] = (ctx / l).astype(BF16)


def _attention(qi, kw, qabs, ckv, qb):
    B, S, _ = ckv.shape
    topk = min(TOPK_MAX, S // 4)
    buf = qabs
    for j in range(S // qb):
        lk = (j + 1) * qb
        G = ATTN_BATCH if (B % ATTN_BATCH == 0 and lk <= ATTN_BATCH_MAX_KEYS) else 1
        blk = lambda n, j=j: pl.BlockSpec((G, qb, n), lambda b: (b, j, 0))
        keys = lambda n, lk=lk: pl.BlockSpec((G, lk, n), lambda b: (b, 0, 0))
        buf = pl.pallas_call(
            functools.partial(_attn_kernel, topk=topk, q0=j * qb),
            grid=(B // G,),
            in_specs=[blk(IDX_HEADS * IDX_DIM), blk(LANES), keys(LANES), blk(N_HEADS * D_KV_LAT),
                      keys(D_KV_LAT)],
            out_specs=blk(N_HEADS * D_KV_LAT),
            out_shape=jax.ShapeDtypeStruct(buf.shape, BF16),
            scratch_shapes=[pltpu.VMEM((G * qb, lk), jnp.int32), pltpu.VMEM((G * qb, lk), jnp.int32),
                            pltpu.VMEM((G * qb, lk), F32)],
            input_output_aliases={3: 0},
            compiler_params=_cparams(("arbitrary",)),
            name=f"dsa_attention_q{j}",
        )(qi, kw, kw, buf, ckv)
    return buf


def _out_kernel(ycp_ref, ctx_ref, x_ref, g1_ref, sc_ref, sh_ref, ng_ref, wcp_ref, wvo_ref,
                rw_ref, rb_ref, tri_ref, x1_ref, h2_ref, route_ref, cnt_ref, carry_ref):
    first = (pl.program_id(0) == 0) & (pl.program_id(1) == 0)

    @pl.when(first)
    def _():
        carry_ref[...] = jnp.zeros_like(carry_ref)

    mix = jnp.dot(ycp_ref[0], wcp_ref[...], preferred_element_type=F32)
    mix = mix + jnp.dot(ctx_ref[0], wvo_ref[...], preferred_element_type=F32)
    x1 = x_ref[0] + g1_ref[0] * mix
    x1_ref[0] = x1
    h2 = _rms(x1) * ng_ref[...] * (1.0 + sc_ref[0]) + sh_ref[0]
    _rows_to_tiles(h2_ref.at[0], h2)
    rw = rw_ref[...]
    rw_hi = rw.astype(BF16)
    rw_lo = (rw - rw_hi.astype(F32)).astype(BF16)
    h_hi = h2.astype(BF16)
    h_lo = (h2 - h_hi.astype(F32)).astype(BF16)
    logits = (jnp.dot(h_hi, rw_hi, preferred_element_type=F32)
              + (jnp.dot(h_lo, rw_hi, preferred_element_type=F32)
                 + jnp.dot(h_hi, rw_lo, preferred_element_type=F32))) + rb_ref[...]

    lt = logits.T
    tm = lt.shape[1]
    rowi = lax.broadcasted_iota(jnp.int32, (LANES, tm), 0)
    rowf = rowi.astype(F32)
    ninf = -jnp.inf
    big = jnp.float32(LANES)
    glog = jnp.where(rowi < N_GROUPS, lt, ninf)
    gmax = jnp.max(glog, axis=0, keepdims=True)
    gidx = jnp.min(jnp.where(glog == gmax, rowf, big), axis=0, keepdims=True)
    g_p = 1.0 / jnp.sum(jnp.exp(glog - gmax), axis=0, keepdims=True)
    lo = N_GROUPS + EXPERTS_PER_GROUP * gidx
    elog = jnp.where((rowf >= lo) & (rowf < lo + EXPERTS_PER_GROUP), lt, ninf)
    m1 = jnp.max(elog, axis=0, keepdims=True)
    i1 = jnp.min(jnp.where(elog == m1, rowf, big), axis=0, keepdims=True)
    elog2 = jnp.where(rowf == i1, ninf, elog)
    m2 = jnp.max(elog2, axis=0, keepdims=True)
    i2 = jnp.min(jnp.where(elog2 == m2, rowf, big), axis=0, keepdims=True)
    r = jnp.exp(m2 - m1)
    gate1 = g_p / (1.0 + r)
    gate2 = g_p * r / (1.0 + r)
    e1 = i1 - N_GROUPS
    e2 = i2 - N_GROUPS
    oh1 = rowf == e1
    oh2 = rowf == e2
    oh1f = jnp.where(oh1, 1.0, 0.0)
    oh2f = jnp.where(oh2, 1.0, 0.0)
    pre1 = jnp.dot(oh1f.astype(BF16), tri_ref[...], preferred_element_type=F32)
    pre2 = jnp.dot(oh2f.astype(BF16), tri_ref[...], preferred_element_type=F32)
    carry = carry_ref[:, 0:1]
    cnt1 = jnp.sum(oh1f, axis=1, keepdims=True)
    cnt2 = jnp.sum(oh2f, axis=1, keepdims=True)
    rank1 = jnp.sum(jnp.where(oh1, carry + pre1, 0.0), axis=0, keepdims=True)
    rank2 = jnp.sum(jnp.where(oh2, carry + cnt1 + pre2, 0.0), axis=0, keepdims=True)
    total = carry + cnt1 + cnt2
    carry_ref[...] = jnp.broadcast_to(total, carry_ref.shape)
    cnt_ref[...] = jnp.broadcast_to(total, cnt_ref.shape)
    route_t = jnp.where(rowi == 0, e1, jnp.where(rowi == 1, e2, jnp.where(rowi == 2, gate1, jnp.where(
        rowi == 3, gate2, jnp.where(rowi == 4, rank1, jnp.where(rowi == 5, rank2, 0.0))))))
    route_ref[0] = route_t.T


def _out_proj(ycp, ctx, x, g1, sc2, sh2, ng, wcp, wvo, rw, rb, tm):
    B, S, D = x.shape
    tok = lambda n: pl.BlockSpec((1, tm, n), lambda b, i: (b, i, 0))
    per_b = pl.BlockSpec((1, 1, D), lambda b, i: (b, 0, 0))
    full = lambda a: pl.BlockSpec(a.shape, lambda b, i: (0,) * a.ndim)
    tri = jnp.triu(jnp.ones((tm, tm), BF16), 1)
    return pl.pallas_call(
        _out_kernel,
        grid=(B, S // tm),
        in_specs=[tok(C_CONV + C_POOL), tok(N_HEADS * D_KV_LAT), tok(D), per_b, per_b, per_b,
                  full(ng), full(wcp), full(wvo), full(rw), full(rb), full(tri)],
        out_specs=[tok(D), pl.BlockSpec((1, tm * ROW_TILE, LANES), lambda b, i: (b, i, 0)), tok(LANES),
                   pl.BlockSpec((LANES, LANES), lambda b, i: (0, 0))],
        out_shape=[
            jax.ShapeDtypeStruct((B, S, D), F32),
            jax.ShapeDtypeStruct((B, S * ROW_TILE, LANES), F32),
            jax.ShapeDtypeStruct((B, S, LANES), F32),
            jax.ShapeDtypeStruct((LANES, LANES), F32),
        ],
        scratch_shapes=[pltpu.VMEM((LANES, LANES), F32)],
        compiler_params=_cparams(("arbitrary", "arbitrary")),
        name="out_proj_router",
    )(ycp, ctx, x, g1, sc2, sh2, ng, wcp, wvo, rw, rb, tri)


def _row_copy(src, i, dst, k, sem):
    return pltpu.make_async_copy(src.at[pl.ds(pl.multiple_of(i * ROW_TILE, ROW_TILE), ROW_TILE), :],
                                 dst.at[pl.ds(pl.multiple_of(k * ROW_TILE, ROW_TILE), ROW_TILE), :], sem)


def _rows_to_tiles(ref, val):
    n = val.shape[0]
    for s in range(ROW_TILE):
        ref[pl.ds(s, n, stride=ROW_TILE), :] = val[:, s * LANES:(s + 1) * LANES]


def _tiles_to_rows(ref, n):
    return jnp.concatenate([ref[pl.ds(s, n, stride=ROW_TILE), :] for s in range(ROW_TILE)], axis=-1)


DISPATCH_SLOTS = 3


def _dispatch_kernel(dst_ref, dst_prev_ref, h_ref, xs_in_ref, xs_ref, hbuf, load_sem, sem):
    del xs_in_ref
    rows = hbuf.shape[1]
    tm = rows // ROW_TILE
    i = pl.program_id(0)
    n = pl.num_programs(0)
    slot = i % DISPATCH_SLOTS

    def load(t, s):
        return pltpu.make_async_copy(h_ref.at[pl.ds(pl.multiple_of(t * rows, rows), rows), :], hbuf.at[s],
                                     load_sem.at[s])

    @pl.when(i == 0)
    def _():
        load(0, 0).start()

    @pl.when(i + 1 < n)
    def _():
        load(i + 1, (i + 1) % DISPATCH_SLOTS).start()

    load(i, slot).wait()

    def copies(idx_ref, s, r):
        return (_row_copy(hbuf.at[s], r, xs_ref, idx_ref[2 * r], sem.at[s]),
                _row_copy(hbuf.at[s], r, xs_ref, idx_ref[2 * r + 1], sem.at[s]))

    def issue(r, c):
        for prio, cp in enumerate(copies(dst_ref, slot, r)):
            cp.start(priority=prio)
        return c

    lax.fori_loop(0, tm, issue, 0, unroll=ROW_DMA_UNROLL)

    def drain(idx_ref, s):
        def body(r, c):
            for cp in copies(idx_ref, s, r):
                cp.wait()
            return c
        lax.fori_loop(0, tm, body, 0, unroll=ROW_DMA_UNROLL)

    @pl.when(i >= 1)
    def _():
        drain(dst_prev_ref, (i + DISPATCH_SLOTS - 1) % DISPATCH_SLOTS)

    @pl.when(i == n - 1)
    def _():
        drain(dst_ref, slot)


def _dispatch(dst_flat, h2, xs0, tm):
    T = h2.shape[0] // ROW_TILE
    return pl.pallas_call(
        _dispatch_kernel,
        grid=(T // tm,),
        in_specs=[pl.BlockSpec((2 * tm,), lambda i: (i,), memory_space=pltpu.SMEM),
                  pl.BlockSpec((2 * tm,), lambda i: (jnp.maximum(i - 1, 0),), memory_space=pltpu.SMEM),
                  pl.BlockSpec(memory_space=pl.ANY),
                  pl.BlockSpec(memory_space=pl.ANY)],
        out_specs=pl.BlockSpec(memory_space=pl.ANY),
        out_shape=jax.ShapeDtypeStruct(xs0.shape, F32),
        scratch_shapes=[pltpu.VMEM((DISPATCH_SLOTS, tm * ROW_TILE, LANES), F32),
                        pltpu.SemaphoreType.DMA((DISPATCH_SLOTS,)),
                        pltpu.SemaphoreType.DMA((DISPATCH_SLOTS,))],
        input_output_aliases={3: 0},
        compiler_params=_cparams(("arbitrary",), row_dma=True),
        name="moe_dispatch",
    )(dst_flat, dst_flat, h2, xs0)


def _ffn_kernel(be_ref, nu_ref, xs_ref, w1_ref, w3_ref, w2_ref, y_ref, w13_scr, w2_scr):
    i = pl.program_id(0)
    new_expert = (i == 0) | (be_ref[i] != be_ref[jnp.maximum(i - 1, 0)])

    @pl.when(new_expert)
    def _():
        w13_scr[:, 0:D_EXPERT] = w1_ref[0, 0].astype(BF16)
        w13_scr[:, D_EXPERT:2 * D_EXPERT] = w3_ref[0, 0].astype(BF16)
        w2_scr[...] = w2_ref[0, 0].astype(BF16)

    @pl.when(i < nu_ref[0])
    def _():
        xb = _tiles_to_rows(xs_ref, FFN_BLOCK).astype(BF16)
        h13 = jnp.dot(xb, w13_scr[...], preferred_element_type=F32)
        h1 = h13[:, 0:D_EXPERT]
        h3 = h13[:, D_EXPERT:2 * D_EXPERT]
        act = (h1 * jax.nn.sigmoid(h1)) * h3
        _rows_to_tiles(y_ref, jnp.dot(act.astype(BF16), w2_scr[...], preferred_element_type=F32))

    @pl.when(i >= nu_ref[0])
    def _():
        y_ref[...] = jnp.zeros_like(y_ref)


def _expert_ffn(blk_exp, n_used, xs, w1, w3, w2, layer):
    D = w1.shape[2]
    rows_blk = FFN_BLOCK * ROW_TILE
    nblk = xs.shape[0] // rows_blk
    grid_spec = pltpu.PrefetchScalarGridSpec(
        num_scalar_prefetch=2,
        grid=(nblk,),
        in_specs=[
            pl.BlockSpec((rows_blk, LANES), lambda i, be, nu: (jnp.maximum(jnp.minimum(i, nu[0] - 1), 0), 0)),
            pl.BlockSpec((1, 1, D, D_EXPERT), lambda i, be, nu: (layer, be[i], 0, 0)),
            pl.BlockSpec((1, 1, D, D_EXPERT), lambda i, be, nu: (layer, be[i], 0, 0)),
            pl.BlockSpec((1, 1, D_EXPERT, D), lambda i, be, nu: (layer, be[i], 0, 0)),
        ],
        out_specs=pl.BlockSpec((rows_blk, LANES), lambda i, be, nu: (i, 0)),
        scratch_shapes=[pltpu.VMEM((D, 2 * D_EXPERT), BF16), pltpu.VMEM((D_EXPERT, D), BF16)],
    )
    return pl.pallas_call(
        _ffn_kernel,
        grid_spec=grid_spec,
        out_shape=jax.ShapeDtypeStruct(xs.shape, F32),
        compiler_params=_cparams(("arbitrary",)),
        name="expert_ffn",
    )(blk_exp, n_used, xs, w1, w3, w2)


def _combine_kernel(dst_ref, dst_next_ref, x1_ref, route_ref, g2_ref, fg_ref, y_ref, o_ref, ya_ref, yb_ref, sem,
                    *, final_norm):
    tm = x1_ref.shape[1]
    step = pl.program_id(0) * pl.num_programs(1) + pl.program_id(1)
    n_steps = pl.num_programs(0) * pl.num_programs(1)
    slot = step % 2

    def copies(idx_ref, s, r):
        return (_row_copy(y_ref, idx_ref[2 * r], ya_ref.at[s], r, sem.at[0, s]),
                _row_copy(y_ref, idx_ref[2 * r + 1], yb_ref.at[s], r, sem.at[1, s]))

    def gather(idx_ref, s):
        def issue(r, c):
            for cp in copies(idx_ref, s, r):
                cp.start(priority=1)
            return c
        lax.fori_loop(0, tm, issue, 0, unroll=ROW_DMA_UNROLL)

    @pl.when(step == 0)
    def _():
        gather(dst_ref, 0)

    @pl.when(step + 1 < n_steps)
    def _():
        gather(dst_next_ref, 1 - slot)

    def drain(r, c):
        for cp in copies(dst_ref, slot, r):
            cp.wait()
        return c

    lax.fori_loop(0, tm, drain, 0, unroll=ROW_DMA_UNROLL)

    route = route_ref[0]
    moe = (_tiles_to_rows(ya_ref.at[slot], tm) * route[:, 2:3]
           + _tiles_to_rows(yb_ref.at[slot], tm) * route[:, 3:4])
    out = x1_ref[0] + g2_ref[0] * moe
    if final_norm:
        out = _rms(out) * fg_ref[...]
    o_ref[0] = out


def _combine(dst_flat, x1, route, g2, fg, y, tm, final_norm):
    B, S, D = x1.shape
    nt = S // tm
    tok = lambda n: pl.BlockSpec((1, tm, n), lambda b, i: (b, i, 0))
    return pl.pallas_call(
        functools.partial(_combine_kernel, final_norm=final_norm),
        grid=(B, nt),
        in_specs=[pl.BlockSpec((2 * tm,), lambda b, i: (b * nt + i,), memory_space=pltpu.SMEM),
                  pl.BlockSpec((2 * tm,), lambda b, i: (jnp.minimum(b * nt + i + 1, B * nt - 1),),
                               memory_space=pltpu.SMEM),
                  tok(D), tok(LANES),
                  pl.BlockSpec((1, 1, D), lambda b, i: (b, 0, 0)),
                  pl.BlockSpec((1, D), lambda b, i: (0, 0)),
                  pl.BlockSpec(memory_space=pl.ANY)],
        out_specs=tok(D),
        out_shape=jax.ShapeDtypeStruct((B, S, D), F32),
        scratch_shapes=[pltpu.VMEM((2, tm * ROW_TILE, LANES), F32), pltpu.VMEM((2, tm * ROW_TILE, LANES), F32),
                        pltpu.SemaphoreType.DMA((2, 2))],
        compiler_params=_cparams(("arbitrary", "arbitrary"), row_dma=True),
        name="moe_combine",
    )(dst_flat, dst_flat, x1, route, g2, fg, y)


def _pick_tile(n, pref):
    t = min(n, pref)
    while n % t:
        t //= 2
    return t


def kernel(x, c, mod_w, mod_b, norm1_g, w_in, conv_k, conv_b, conv_ln_g, conv_ln_b, q_norm_g, kv_norm_g, w_uq, w_uk, w_uv, pool_w, pool_scale, w_out, norm2_g, router_g_w, router_g_b, router_e_w, router_e_b, exp_w1, exp_w3, exp_w2, final_g):
    B, S, D = x.shape
    L = mod_w.shape[0]
    T = B * S
    tm = _pick_tile(S, 512)
    qb = _pick_tile(S, 256)
    tmc = _pick_tile(S, 512)
    n_rows = 2 * T + N_EXPERTS * FFN_BLOCK
    nblk = n_rows // FFN_BLOCK

    mod = _modulation(c, mod_w, mod_b)
    wq_all, wvo_all = _fold_weights(w_uq, w_uk, w_uv, w_out)
    row = lambda a: a.reshape(1, -1)

    assert D == ROW_TILE * LANES
    slot_buf = jnp.zeros((n_rows * ROW_TILE, LANES), F32)
    for l in range(L):
        sh1, sc1, g1, sh2, sc2, g2 = [mod[l, :, k * D:(k + 1) * D].reshape(B, 1, D) for k in range(6)]
        w_in_p = jnp.concatenate(
            [w_in[l, :, :N_IN - C_POOL], jnp.zeros((D, N_IN_PAD - N_IN), F32), w_in[l, :, N_IN - C_POOL:]],
            axis=1).astype(BF16)
        cu, qabs, ckv, qi, kw = _in_proj(x, sc1, sh1, row(norm1_g[l]), w_in_p, row(q_norm_g[l]),
                                         row(kv_norm_g[l]), wq_all[l], tm)
        pw_bd = jax.scipy.linalg.block_diag(*[pool_w[l, g] for g in range(pool_w.shape[1])]).astype(BF16)
        ycp = _conv_pool(cu, conv_k[l], row(conv_b[l]), row(conv_ln_g[l]), row(conv_ln_b[l]),
                         pw_bd, row(pool_scale[l]))
        ctx = _attention(qi, kw, qabs, ckv, qb)
        wcp = jnp.concatenate([w_out[l, :C_CONV], w_out[l, D - C_POOL:]], axis=0).astype(BF16)
        n_r = N_GROUPS + N_EXPERTS
        rw = jnp.concatenate([router_g_w[l], router_e_w[l].reshape(D, N_EXPERTS),
                              jnp.zeros((D, LANES - n_r), F32)], axis=1)
        rb = jnp.concatenate([router_g_b[l], router_e_b[l].reshape(N_EXPERTS),
                              jnp.zeros((LANES - n_r,), F32)]).reshape(1, LANES)
        x1, h2, route, cnt = _out_proj(ycp, ctx, x, g1, sc2, sh2, row(norm2_g[l]), wcp, wvo_all[l],
                                       rw, rb, tm)
        counts = cnt[:N_EXPERTS, 0].astype(jnp.int32)
        pcounts = (counts + FFN_BLOCK - 1) // FFN_BLOCK * FFN_BLOCK
        pends = jnp.cumsum(pcounts)
        pstarts = pends - pcounts
        route2 = route.reshape(T, LANES)
        e_idx = route2[:, 0:2].astype(jnp.int32)
        e_start = jnp.sum(jnp.where(e_idx[..., None] == jnp.arange(N_EXPERTS, dtype=jnp.int32), pstarts, 0),
                          axis=-1)
        dst = (e_start + route2[:, 4:6].astype(jnp.int32)).reshape(2 * T)
        n_used = (pends[-1] // FFN_BLOCK).astype(jnp.int32).reshape(1)
        blk_start = jnp.arange(nblk, dtype=jnp.int32) * FFN_BLOCK
        blk_exp = jnp.minimum(jnp.sum((pends[None, :] <= blk_start[:, None]).astype(jnp.int32), axis=1),
                              N_EXPERTS - 1)
        xs = _dispatch(dst, h2.reshape(T * ROW_TILE, LANES), slot_buf, tmc)
        y = _expert_ffn(blk_exp, n_used, xs, exp_w1, exp_w3, exp_w2, l)
        slot_buf = y
        x = _combine(dst, x1, route, g2, row(final_g), y, tmc, final_norm=(l == L - 1))
    return x
```
